```python
import math
import jax, jax.numpy as jnp
from jax import lax
import numpy as np

D_MODEL = 1024
BATCH = 8
SEQ = 2048
DEPTH = 2
DEC_BATCH = 128
DEC_SEQ = 8
PAST_LEN = 16384
PAGE_SIZE = 128

GLA_HEADS = 4
GLA_DK = D_MODEL // 16
GLA_DV = D_MODEL // 8
GLA_KW = GLA_HEADS * GLA_DK
GLA_W = GLA_HEADS * GLA_DV
GLA_RANK = 16
GLA_GATE_TAU = 16.0
RG_WIDTH = D_MODEL // 2
RG_BLOCKS = 8
RG_BW = RG_WIDTH // RG_BLOCKS
RG_C = 8.0
CONV_W = 4
ML_HEADS = 4
ML_DH = D_MODEL // 8
ML_W = ML_HEADS * ML_DH
HG_HEADS = 4
HG_DK = D_MODEL // 8
HG_DV = D_MODEL // 8
HG_W = HG_HEADS * HG_DV
N_BRANCH = 4
D_FF = 4 * D_MODEL
CHUNK = 64
EPS = 1e-6
NEG = -1e30

IN_SIZES = (GLA_KW, GLA_KW, GLA_W, GLA_RANK, GLA_W,
            RG_WIDTH, RG_WIDTH,
            ML_W, ML_W, ML_W, ML_W, ML_HEADS, ML_HEADS,
            HG_W, HG_W, HG_W, HG_W,
            N_BRANCH * D_MODEL)
N_IN = sum(IN_SIZES)

kernel_name = "hybrid_gla_rglru_mlstm_hgrn2_step"


def rmsnorm(x, g):
    xf = x.astype(jnp.float32)
    y = xf * lax.rsqrt(jnp.mean(xf * xf, axis=-1, keepdims=True) + EPS)
    return (y * g.astype(jnp.float32)).astype(x.dtype)


def head_rmsnorm(h, g):
    B, T, H, d = h.shape
    hf = h.astype(jnp.float32)
    y = hf * lax.rsqrt(jnp.mean(hf * hf, axis=-1, keepdims=True) + EPS)
    return (y * g.astype(jnp.float32).reshape(H, d)).reshape(B, T, H * d)


def split_cols(z):
    out = []
    o = 0
    for s in IN_SIZES:
        out.append(z[..., o:o + s])
        o += s
    return out


def causal_conv(x, buf, w, b):
    T = x.shape[1]
    xp = jnp.concatenate([buf.astype(x.dtype), x], axis=1)
    y = b.astype(x.dtype)
    for j in range(CONV_W):
        y = y + xp[:, j:j + T] * w[j].astype(x.dtype)
    return y, xp[:, -(CONV_W - 1):]


def chunk_len(T):
    return math.gcd(T, CHUNK)


def to_chunks(z, L):
    B, T, H = z.shape[:3]
    z = z.reshape((B, T // L, L, H) + z.shape[3:])
    perm = (1, 0, 3, 2) + tuple(range(4, z.ndim))
    return z.transpose(perm)


def from_chunks(z):
    NC, B, H, L, d = z.shape
    return z.transpose(1, 0, 3, 2, 4).reshape(B, NC * L, H, d)


def gated_linear_chunked(q, k, v, log_a, S0):
    f32 = jnp.float32
    L = chunk_len(q.shape[1])
    xs = tuple(to_chunks(z.astype(f32), L) for z in (q, k, v, log_a))
    mask = jnp.tril(jnp.ones((L, L), dtype=bool))[:, :, None]

    def step(S, inp):
        qb, kb, vb, ab = inp
        b = jnp.cumsum(ab, axis=2)
        diff = b[:, :, :, None, :] - b[:, :, None, :, :]
        decay = jnp.where(mask, jnp.exp(jnp.where(mask, diff, 0.0)), 0.0)
        A = jnp.einsum('bhtd,bhsd,bhtsd->bhts', qb, kb, decay)
        o = jnp.einsum('bhts,bhsv->bhtv', A, vb) + jnp.einsum('bhtd,bhdv->bhtv', qb * jnp.exp(b), S)
        bL = b[:, :, -1:, :]
        kd = kb * jnp.exp(bL - b)
        S_new = jnp.exp(bL[:, :, 0, :])[..., None] * S + jnp.einsum('bhsd,bhsv->bhdv', kd, vb)
        return S_new, o

    S, oc = lax.scan(step, S0.astype(f32), xs)
    return from_chunks(oc), S


def mlstm_chunked(q, k, v, i_pre, f_pre, C0, n0, m0):
    f32 = jnp.float32
    L = chunk_len(q.shape[1])
    logf = jax.nn.log_sigmoid(f_pre.astype(f32))
    qc, kc, vc = (to_chunks(z.astype(f32), L) for z in (q, k, v))
    ic = to_chunks(i_pre.astype(f32), L)
    fc = to_chunks(logf, L)
    mask = jnp.tril(jnp.ones((L, L), dtype=bool))

    def step(carry, inp):
        C, n, m = carry
        qb, kb, vb, ib, fb = inp
        b = jnp.cumsum(fb, axis=-1)
        logD = jnp.where(mask, b[..., :, None] - b[..., None, :] + ib[..., None, :], NEG)
        prev = b + m[..., None]
        mt = jnp.maximum(prev, jnp.max(logD, axis=-1))
        Dm = jnp.where(mask, jnp.exp(logD - mt[..., None]), 0.0)
        sp = jnp.exp(prev - mt)
        S = jnp.einsum('bhtd,bhsd->bhts', qb, kb) * Dm
        num = jnp.einsum('bhts,bhsv->bhtv', S, vb) + sp[..., None] * jnp.einsum('bhtd,bhdv->bhtv', qb, C)
        den = jnp.sum(S, axis=-1) + sp * jnp.einsum('bhtd,bhd->bht', qb, n)
        h = num / jnp.maximum(jnp.abs(den), jnp.exp(-mt))[..., None]
        bL = b[..., -1]
        wlog = bL[..., None] - b + ib
        m_new = jnp.maximum(bL + m, jnp.max(wlog, axis=-1))
        w = jnp.exp(wlog - m_new[..., None])
        sc = jnp.exp(bL + m - m_new)
        C_new = sc[..., None, None] * C + jnp.einsum('bhs,bhsd,bhsv->bhdv', w, kb, vb)
        n_new = sc[..., None] * n + jnp.einsum('bhs,bhsd->bhd', w, kb)
        return (C_new, n_new, m_new), h

    (C, n, m), hc = lax.scan(step, (C0.astype(f32), n0.astype(f32), m0.astype(f32)), (qc, kc, vc, ic, fc))
    return from_chunks(hc), C, n, m


def rglru(x, h0, wa, ba, wx, bx, lam):
    B, T, W = x.shape
    xb = x.reshape(B, T, RG_BLOCKS, RG_BW)
    r = jax.nn.sigmoid(jnp.einsum('btnd,nde->btne', xb, wa.astype(x.dtype)).reshape(B, T, W) + ba)
    i = jax.nn.sigmoid(jnp.einsum('btnd,nde->btne', xb, wx.astype(x.dtype)).reshape(B, T, W) + bx)
    log_a = RG_C * r * jax.nn.log_sigmoid(lam.astype(jnp.float32))
    a = jnp.exp(log_a)
    u = jnp.sqrt(jnp.maximum(-jnp.expm1(2.0 * log_a), 0.0)) * (i * x)
    u = u.at[:, 0].add(a[:, 0] * h0.astype(jnp.float32))

    def comb(lhs, rhs):
        a1, b1 = lhs
        a2, b2 = rhs
        return a1 * a2, a2 * b1 + b2

    _, h = lax.associative_scan(comb, (a, u), axis=1)
    return h, h[:, -1]


def layer_forward(x, st, l, W, lb):
    S_gla, h_rg, conv_rg, C_ml, n_ml, m_ml, conv_ml, S_hg = st
    B, T, _ = x.shape
    dt = x.dtype
    f32 = jnp.float32
    xn = rmsnorm(x, W['norm_mix'][l])
    z = (xn @ W['w_in'][l]).astype(f32)
    (g_q, g_k, g_v, g_lr, g_g, r_x, r_y, m_q, m_k, m_v, m_o, m_i, m_f,
     h_q, h_f, h_i, h_g, gates) = split_cols(z)

    q = g_q.reshape(B, T, GLA_HEADS, GLA_DK) * (GLA_DK ** -0.5)
    k = g_k.reshape(B, T, GLA_HEADS, GLA_DK)
    v = g_v.reshape(B, T, GLA_HEADS, GLA_DV)
    log_a = jax.nn.log_sigmoid(g_lr @ W['gla_w_lr'][l].astype(f32) + W['gla_b_lr'][l]).reshape(B, T, GLA_HEADS, GLA_DK) / GLA_GATE_TAU
    o, S_gla_new = gated_linear_chunked(q, k, v, log_a, S_gla)
    br_gla = head_rmsnorm(o, W['gla_norm'][l]) * jax.nn.silu(g_g)

    xc, conv_rg_new = causal_conv(r_x, conv_rg, W['rg_conv_w'][l], W['rg_conv_b'][l])
    h, h_rg_new = rglru(xc, h_rg, W['rg_wa'][l], W['rg_ba'][l], W['rg_wx'][l], W['rg_bx'][l], W['rg_lam'][l])
    br_rg = h * jax.nn.gelu(r_y)

    qk, conv_ml_new = causal_conv(jnp.concatenate([m_q, m_k], axis=-1), conv_ml, W['ml_conv_w'][l], W['ml_conv_b'][l])
    qk = jax.nn.silu(qk)
    mq = qk[..., :ML_W].reshape(B, T, ML_HEADS, ML_DH)
    mk = qk[..., ML_W:].reshape(B, T, ML_HEADS, ML_DH) * (ML_DH ** -0.5)
    mv = m_v.reshape(B, T, ML_HEADS, ML_DH)
    hm, C_new, n_new, m_new = mlstm_chunked(mq, mk, mv, m_i + W['ml_b_i'][l], m_f + W['ml_b_f'][l], C_ml, n_ml, m_ml)
    br_ml = head_rmsnorm(jax.nn.sigmoid(m_o).reshape(B, T, ML_HEADS, ML_DH) * hm, W['ml_norm'][l])

    lb_h = lb.astype(f32).reshape(HG_HEADS, HG_DK)
    fp = h_f.reshape(B, T, HG_HEADS, HG_DK)
    ls = jax.nn.log_sigmoid(fp)
    pos = lb_h > 0.0
    log_lb = jnp.log(jnp.where(pos, lb_h, 1.0))
    logf = jnp.where(pos, jnp.logaddexp(log_lb, jnp.log1p(-lb_h) + ls), ls)
    kh = (1.0 - lb_h) * jax.nn.sigmoid(-fp)
    qh = jax.nn.silu(h_q).reshape(B, T, HG_HEADS, HG_DK)
    vh = h_i.reshape(B, T, HG_HEADS, HG_DV)
    oh, S_hg_new = gated_linear_chunked(qh, kh, vh, logf, S_hg)
    br_hg = head_rmsnorm(oh, W['hg_norm'][l]) * jax.nn.silu(h_g)

    gts = jax.nn.sigmoid(gates.reshape(B, T, N_BRANCH, D_MODEL) + W['merge_b'][l].astype(f32))
    merged = (gts[:, :, 0] * (br_gla @ W['w_br_gla'][l].astype(f32))
              + gts[:, :, 1] * (br_rg @ W['w_br_rg'][l].astype(f32))
              + gts[:, :, 2] * (br_ml @ W['w_br_ml'][l].astype(f32))
              + gts[:, :, 3] * (br_hg @ W['w_br_hg'][l].astype(f32)))
    x = x + (merged.astype(dt) @ W['w_out'][l]).astype(dt)

    xn2 = rmsnorm(x, W['norm_ffn'][l])
    hdn = jnp.square(jax.nn.relu(xn2 @ W['w_up'][l]))
    x = x + (hdn @ W['w_down'][l]).astype(dt)
    return x, (S_gla_new, h_rg_new, conv_rg_new, C_new, n_new, m_new, conv_ml_new, S_hg_new)


def trunk(x, states, W, lbs):
    new = [[] for _ in range(8)]
    for l in range(DEPTH):
        st_l = tuple(s[l] for s in states)
        x, ns = layer_forward(x, st_l, l, W, lbs[l])
        for j in range(8):
            new[j].append(ns[j].astype(jnp.float32))
    y = rmsnorm(x, W['norm_final'])
    return y, tuple(jnp.stack(n_) for n_ in new)


def zero_states(B):
    f32 = jnp.float32
    return (jnp.zeros((DEPTH, B, GLA_HEADS, GLA_DK, GLA_DV), f32),
            jnp.zeros((DEPTH, B, RG_WIDTH), f32),
            jnp.zeros((DEPTH, B, CONV_W - 1, RG_WIDTH), f32),
            jnp.zeros((DEPTH, B, ML_HEADS, ML_DH, ML_DH), f32),
            jnp.zeros((DEPTH, B, ML_HEADS, ML_DH), f32),
            jnp.zeros((DEPTH, B, ML_HEADS), f32),
            jnp.zeros((DEPTH, B, CONV_W - 1, 2 * ML_W), f32),
            jnp.zeros((DEPTH, B, HG_HEADS, HG_DK, HG_DV), f32))


def setup_inputs(seed: int = 0) -> dict:
    key = jax.random.key(seed)
    ks = iter(jax.random.split(key, 64))
    f32 = jnp.float32

    def nrm(shape, scale):
        return jax.random.normal(next(ks), shape, f32) * scale

    D = DEPTH
    lam_u = jax.random.uniform(next(ks), (D, RG_WIDTH), f32, 0.9, 0.999)
    return {
        'x_prompt': nrm((BATCH, SEQ, D_MODEL), 1.0),
        'x_sample': nrm((DEC_BATCH, DEC_SEQ, D_MODEL), 1.0),
        'state_gla': nrm((D, DEC_BATCH, GLA_HEADS, GLA_DK, GLA_DV), 0.3),
        'state_rglru_h': nrm((D, DEC_BATCH, RG_WIDTH), 0.5),
        'state_rglru_conv': nrm((D, DEC_BATCH, CONV_W - 1, RG_WIDTH), 1.0),
        'state_mlstm_C': nrm((D, DEC_BATCH, ML_HEADS, ML_DH, ML_DH), 0.3),
        'state_mlstm_n': nrm((D, DEC_BATCH, ML_HEADS, ML_DH), 0.3),
        'state_mlstm_m': nrm((D, DEC_BATCH, ML_HEADS), 1.0),
        'state_mlstm_conv': nrm((D, DEC_BATCH, CONV_W - 1, 2 * ML_W), 1.0),
        'state_hgrn': nrm((D, DEC_BATCH, HG_HEADS, HG_DK, HG_DV), 0.3),
        'norm_mix': 1.0 + nrm((D, D_MODEL), 0.02),
        'w_in': nrm((D, D_MODEL, N_IN), D_MODEL ** -0.5),
        'gla_w_lr': nrm((D, GLA_RANK, GLA_KW), GLA_RANK ** -0.5),
        'gla_b_lr': nrm((D, GLA_KW), 0.1),
        'gla_norm': 1.0 + nrm((D, GLA_W), 0.02),
        'rg_conv_w': nrm((D, CONV_W, RG_WIDTH), 0.5),
        'rg_conv_b': nrm((D, RG_WIDTH), 0.02),
        'rg_wa': nrm((D, RG_BLOCKS, RG_BW, RG_BW), RG_BW ** -0.5),
        'rg_ba': nrm((D, RG_WIDTH), 0.02),
        'rg_wx': nrm((D, RG_BLOCKS, RG_BW, RG_BW), RG_BW ** -0.5),
        'rg_bx': nrm((D, RG_WIDTH), 0.02),
        'rg_lam': jnp.log(lam_u) - jnp.log1p(-lam_u),
        'ml_conv_w': nrm((D, CONV_W, 2 * ML_W), 0.5),
        'ml_conv_b': nrm((D, 2 * ML_W), 0.02),
        'ml_b_i': nrm((D, ML_HEADS), 0.1),
        'ml_b_f': 3.0 + nrm((D, ML_HEADS), 0.5),
        'ml_norm': 1.0 + nrm((D, ML_W), 0.02),
        'hg_gamma': nrm((D, HG_HEADS * HG_DK), 1.0),
        'hg_norm': 1.0 + nrm((D, HG_W), 0.02),
        'merge_b': nrm((D, N_BRANCH, D_MODEL), 0.02),
        'w_br_gla': nrm((D, GLA_W, D_MODEL), GLA_W ** -0.5),
        'w_br_rg': nrm((D, RG_WIDTH, D_MODEL), RG_WIDTH ** -0.5),
        'w_br_ml': nrm((D, ML_W, D_MODEL), ML_W ** -0.5),
        'w_br_hg': nrm((D, HG_W, D_MODEL), HG_W ** -0.5),
        'w_out': nrm((D, D_MODEL, D_MODEL), 0.5 * D_MODEL ** -0.5),
        'norm_ffn': 1.0 + nrm((D, D_MODEL), 0.02),
        'w_up': nrm((D, D_MODEL, D_FF), D_MODEL ** -0.5),
        'w_down': nrm((D, D_FF, D_MODEL), 0.5 * D_FF ** -0.5),
        'norm_final': 1.0 + nrm((D_MODEL,), 0.02),
    }


def reference(x_prompt, x_sample, state_gla, state_rglru_h, state_rglru_conv, state_mlstm_C,
              state_mlstm_n, state_mlstm_m, state_mlstm_conv, state_hgrn,
              norm_mix, w_in, gla_w_lr, gla_b_lr, gla_norm, rg_conv_w, rg_conv_b, rg_wa, rg_ba,
              rg_wx, rg_bx, rg_lam, ml_conv_w, ml_conv_b, ml_b_i, ml_b_f, ml_norm, hg_gamma,
              hg_norm, merge_b, w_br_gla, w_br_rg, w_br_ml, w_br_hg, w_out, norm_ffn, w_up,
              w_down, norm_final):
    W = dict(norm_mix=norm_mix, w_in=w_in, gla_w_lr=gla_w_lr, gla_b_lr=gla_b_lr, gla_norm=gla_norm,
             rg_conv_w=rg_conv_w, rg_conv_b=rg_conv_b, rg_wa=rg_wa, rg_ba=rg_ba, rg_wx=rg_wx,
             rg_bx=rg_bx, rg_lam=rg_lam, ml_conv_w=ml_conv_w, ml_conv_b=ml_conv_b, ml_b_i=ml_b_i,
             ml_b_f=ml_b_f, ml_norm=ml_norm, hg_norm=hg_norm, merge_b=merge_b,
             w_br_gla=w_br_gla, w_br_rg=w_br_rg, w_br_ml=w_br_ml, w_br_hg=w_br_hg, w_out=w_out,
             norm_ffn=norm_ffn, w_up=w_up, w_down=w_down, norm_final=norm_final)
    sm = jax.nn.softmax(hg_gamma.astype(jnp.float32), axis=0)
    lbs = jnp.concatenate([jnp.zeros_like(sm[:1]), jnp.cumsum(sm, axis=0)[:-1]], axis=0)
    lbs = jnp.clip(lbs, 0.0, 1.0 - 1e-6)

    y_prompt, p_st = trunk(x_prompt, zero_states(x_prompt.shape[0]), W, lbs)
    s_in = (state_gla, state_rglru_h, state_rglru_conv, state_mlstm_C, state_mlstm_n,
            state_mlstm_m, state_mlstm_conv, state_hgrn)
    y_sample, s_st = trunk(x_sample, s_in, W, lbs)
    return (y_prompt, y_sample,
            p_st[0], p_st[1], p_st[2], p_st[3], p_st[4], p_st[5], p_st[6], p_st[7],
            s_st[0], s_st[1], s_st[2], s_st[3], s_st[4], s_st[5], s_st[6], s_st[7])
```

```python
import collections
import functools

import jax
import jax.numpy as jnp
from jax import lax
from jax.experimental import pallas as pl
from jax.experimental.pallas import tpu as pltpu

F32 = jnp.float32
BF16 = jnp.bfloat16

D_MODEL = 1024
DEPTH = 2
GLA_HEADS, GLA_DK, GLA_DV, GLA_RANK = 4, 64, 128, 16
GLA_KW, GLA_W = GLA_HEADS * GLA_DK, GLA_HEADS * GLA_DV
GLA_GATE_TAU = 16.0
RG_WIDTH, RG_BLOCKS, RG_C = 512, 8, 8.0
CONV_W = 4
ML_HEADS, ML_DH = 4, 128
ML_W = ML_HEADS * ML_DH
HG_HEADS, HG_DK, HG_DV = 4, 128, 128
HG_W = HG_HEADS * HG_DV
N_BRANCH = 4
D_FF = 4 * D_MODEL
EPS = 1e-6
NEG = -1e30

LANES = 128
SUBLANES = 8
MXU_DIM = 256
ROW_TILE = 256
BAND = 16
VMEM_LIMIT = 56 * 1024 * 1024

_OFF = {}
_o = 0
for _name, _size in (("g_q", GLA_KW), ("g_k", GLA_KW), ("g_v", GLA_W), ("g_lr", GLA_RANK), ("g_g", GLA_W),
                     ("r_x", RG_WIDTH), ("r_y", RG_WIDTH),
                     ("m_q", ML_W), ("m_k", ML_W), ("m_v", ML_W), ("m_o", ML_W), ("m_i", ML_HEADS), ("m_f", ML_HEADS),
                     ("h_q", HG_W), ("h_f", HG_W), ("h_i", HG_W), ("h_g", HG_W),
                     ("gates", N_BRANCH * D_MODEL)):
    _OFF[_name] = (_o, _o + _size)
    _o += _size

Geom = collections.namedtuple("Geom", "B T Bb Tt R SEG NSEG NT")


def _geom(B, T):
    if T >= ROW_TILE:
        Bb, Tt = 1, ROW_TILE
    else:
        Bb, Tt = ROW_TILE // T, T
    assert T % Tt == 0 and B % Bb == 0 and Tt % SUBLANES == 0 and Tt >= SUBLANES
    return Geom(B, T, Bb, Tt, Bb * Tt, Tt, Bb, T // Tt)


def _dot(a, b):
    return jnp.dot(a, b, preferred_element_type=F32)


def _dot_nt(a, b):
    return lax.dot_general(a, b, (((1,), (1,)), ((), ())), preferred_element_type=F32)


def _dot3(m01, x):
    hi = x.astype(BF16)
    r1 = x - hi.astype(F32)
    mid = r1.astype(BF16)
    lo = (r1 - mid.astype(F32)).astype(BF16)
    return _dot(m01, hi) + (_dot(m01, mid) + _dot(m01, lo))


def _iota(shape, dim):
    return lax.broadcasted_iota(jnp.int32, shape, dim)


def _div(x, n):
    assert n & (n - 1) == 0
    return x >> (n.bit_length() - 1)


def _mod(x, n):
    assert n & (n - 1) == 0
    return x & (n - 1)


def _log_sigmoid(x):
    return jnp.minimum(x, 0.0) - jnp.log1p(jnp.exp(-jnp.abs(x)))


def _silu(x):
    return x * jax.nn.sigmoid(x)


def _gelu_tanh(x):
    return 0.5 * x * (1.0 + jnp.tanh(0.7978845608028654 * (x + 0.044715 * (x * x * x))))


def _rmsnorm(x, g):
    return x * lax.rsqrt(jnp.mean(x * x, axis=-1, keepdims=True) + EPS) * g


def _head_rmsnorm(o, gn, H, d):
    outs = []
    for h in range(H):
        oh = o[:, h * d:(h + 1) * d]
        outs.append(oh * lax.rsqrt(jnp.mean(oh * oh, axis=-1, keepdims=True) + EPS) * gn[:, h * d:(h + 1) * d])
    return jnp.concatenate(outs, axis=1)


def _block_last(c, w):
    R, C = c.shape
    if w == R:
        return jnp.broadcast_to(c[R - 1:R, :], (R, C))
    c3 = c.reshape(R // w, w, C)
    return jnp.broadcast_to(c3[:, w - 1:w, :], (R // w, w, C)).reshape(R, C)


def _block_prev_last(c, w):
    R, C = c.shape
    c3 = c.reshape(R // w, w, C)
    ends = c3[:, w - 1:w, :]
    prev = jnp.concatenate([jnp.zeros((1, 1, C), c.dtype), ends[:-1]], axis=0)
    return jnp.broadcast_to(prev, (R // w, w, C)).reshape(R, C)


def _seg_tri(g):
    row = _iota((g.R, g.R), 0)
    col = _iota((g.R, g.R), 1)
    keep = (col <= row) & (_div(row, g.SEG) == _div(col, g.SEG))
    return keep, keep.astype(BF16)


def _seg_rows(ref3, g):
    v = ref3[...]
    C = v.shape[-1]
    return jnp.broadcast_to(v, (g.NSEG, g.SEG, C)).reshape(g.R, C)


def _causal_conv(x, prev8, w_ref, b_ref, g):
    R = g.R
    sub = _mod(_iota((R, 1), 0), SUBLANES)
    y = b_ref[...] + x * w_ref[CONV_W - 1:CONV_W, :]
    for s in range(1, CONV_W):
        xs = jnp.where(sub >= s, pltpu.roll(x, s, 0), pltpu.roll(prev8, R + s - SUBLANES, 0))
        y = y + xs * w_ref[CONV_W - 1 - s:CONV_W - s, :]
    return y


def _conv_io(x, conv_in_ref, conv_out_ref, carry_ref, g):
    R = g.R
    ti = pl.program_id(1)
    if g.NSEG == 1:
        @pl.when(ti == 0)
        def _():
            carry_ref[...] = conv_in_ref[0]
        prev8 = jnp.concatenate([carry_ref[...], x[:R - SUBLANES, :]], axis=0)
        last8 = x[R - SUBLANES:, :]
        carry_ref[...] = last8
        conv_out_ref[0] = last8
    else:
        C = x.shape[-1]
        prev8 = conv_in_ref[...].reshape(R, C)
        conv_out_ref[...] = x.reshape(g.NSEG, g.SEG, C)
    return prev8


def _gl_core(q, k, v, la, s_ref, scr, g, H, dk, dv):
    qd_s, kdT_s, v_s, o_s, decT_s = scr
    R, SEG, NSEG = g.R, g.SEG, g.NSEG
    HK = H * dk
    row = _iota((R, 1), 0)
    _, tri = _seg_tri(g)
    c = _dot3(tri, la)
    cl = _block_last(c, SEG)

    bw = min(BAND, SEG)
    hpg = MXU_DIM // dk
    e_r = _div(_iota((MXU_DIM, hpg * dv), 0), dk)
    e_c = _div(_iota((MXU_DIM, hpg * dv), 1), dv)
    ones_bd = (e_r == e_c).astype(BF16)
    sub = _mod(row, bw)
    o = jnp.zeros((R, H * dv), F32)
    for d in range(bw):
        if d == 0:
            p, vd = q * k, v
        else:
            valid = sub >= d
            e = jnp.exp(jnp.where(valid, c - pltpu.roll(c, d, 0), 0.0))
            p = jnp.where(valid, q * pltpu.roll(k, d, 0) * e, 0.0)
            vd = pltpu.roll(v, d, 0)
        pb = p.astype(BF16)
        parts = [_dot(pb[:, i * MXU_DIM:(i + 1) * MXU_DIM], ones_bd) for i in range(HK // MXU_DIM)]
        o = o + (parts[0] if len(parts) == 1 else jnp.concatenate(parts, axis=1)) * vd

    vb = v.astype(BF16)
    w = bw
    a_off = [None] * H
    rowq = _iota((R, R), 0)
    colq = _iota((R, R), 1)
    while 2 * w <= SEG:
        qw = (q * jnp.exp(c - _block_prev_last(c, w))).astype(BF16)
        kw = (k * jnp.exp(_block_last(c, w) - c)).astype(BF16)
        rb = _div(rowq, w)
        m = (_mod(rb, 2) == 1) & (_div(colq, w) == rb - 1)
        for h in range(H):
            a = jnp.where(m, _dot_nt(qw[:, h * dk:(h + 1) * dk], kw[:, h * dk:(h + 1) * dk]), 0.0)
            a_off[h] = a if a_off[h] is None else a_off[h] + a
        w *= 2
    if a_off[0] is not None:
        o = o + jnp.concatenate([_dot(a_off[h].astype(BF16), vb[:, h * dv:(h + 1) * dv]) for h in range(H)], axis=1)

    qd_s[...] = q * jnp.exp(c)
    kdT_s[...] = (k * jnp.exp(cl - c)).T.astype(BF16)
    v_s[...] = vb
    decT_s[...] = jnp.exp(cl).T
    o_s[...] = o
    lane = _iota((1, R), 1)

    def seg_body(sg, carry):
        r0 = sg * SEG if isinstance(sg, int) else pl.multiple_of(sg * SEG, SEG)
        for h in range(H):
            S = s_ref[sg, h]
            qd = qd_s[pl.ds(r0, SEG), h * dk:(h + 1) * dk].astype(BF16)
            o_s[pl.ds(r0, SEG), h * dv:(h + 1) * dv] += _dot(qd, S.astype(BF16))
            kt = kdT_s[h * dk:(h + 1) * dk, :]
            if NSEG > 1:
                kt = jnp.where(_div(lane, SEG) == sg, kt, jnp.zeros_like(kt))
            U = _dot(kt, v_s[:, h * dv:(h + 1) * dv])
            dec = jnp.sum(jnp.where(lane == r0, decT_s[h * dk:(h + 1) * dk, :], 0.0), axis=1, keepdims=True)
            s_ref[sg, h] = dec * S + U
        return carry

    if NSEG == 1:
        seg_body(0, 0)
    else:
        lax.fori_loop(0, NSEG, seg_body, 0)
    return o_s[...]


def _gl_scratch(g, H, dk, dv):
    return [pltpu.VMEM((g.R, H * dk), F32), pltpu.VMEM((H * dk, g.R), BF16), pltpu.VMEM((g.R, H * dv), BF16),
            pltpu.VMEM((g.R, H * dv), F32), pltpu.VMEM((H * dk, g.R), F32)]


def _gla_kernel(g, x_ref, gm_ref, w_ref, wlr_ref, blr_ref, gn_ref, sin_ref, br_ref, sout_ref, *scr):
    @pl.when(pl.program_id(1) == 0)
    def _():
        sout_ref[...] = sin_ref[...]
    xn = _rmsnorm(x_ref[...], gm_ref[...]).astype(BF16)
    z = _dot(xn, w_ref[...])
    q = z[:, 0:GLA_KW] * (GLA_DK ** -0.5)
    k = z[:, GLA_KW:2 * GLA_KW]
    v = z[:, 2 * GLA_KW:2 * GLA_KW + GLA_W]
    gate = z[:, 2 * GLA_KW + GLA_W:2 * GLA_KW + 2 * GLA_W]
    lr = z[:, 2 * GLA_KW + 2 * GLA_W:]
    la = _log_sigmoid(_dot(lr.astype(BF16), wlr_ref[...]) + blr_ref[...]) * (1.0 / GLA_GATE_TAU)
    o = _gl_core(q, k, v, la, sout_ref, scr, g, GLA_HEADS, GLA_DK, GLA_DV)
    br_ref[...] = (_head_rmsnorm(o, gn_ref[...], GLA_HEADS, GLA_DV) * _silu(gate)).astype(BF16)


def _hgrn_kernel(g, x_ref, gm_ref, w_ref, lb_ref, gn_ref, sin_ref, br_ref, sout_ref, *scr):
    @pl.when(pl.program_id(1) == 0)
    def _():
        sout_ref[...] = sin_ref[...]
    xn = _rmsnorm(x_ref[...], gm_ref[...]).astype(BF16)
    z = _dot(xn, w_ref[...])
    hq = z[:, 0:HG_W]
    fp = z[:, HG_W:2 * HG_W]
    v = z[:, 2 * HG_W:3 * HG_W]
    gate = z[:, 3 * HG_W:4 * HG_W]
    lb = lb_ref[...]
    ls = _log_sigmoid(fp)
    pos = lb > 0.0
    a_ = jnp.log(jnp.where(pos, lb, 1.0))
    b_ = jnp.log1p(-lb) + ls
    lae = jnp.maximum(a_, b_) + jnp.log1p(jnp.exp(-jnp.abs(a_ - b_)))
    la = jnp.where(pos, lae, ls)
    k = (1.0 - lb) * jax.nn.sigmoid(-fp)
    q = _silu(hq)
    o = _gl_core(q, k, v, la, sout_ref, scr, g, HG_HEADS, HG_DK, HG_DV)
    br_ref[...] = (_head_rmsnorm(o, gn_ref[...], HG_HEADS, HG_DV) * _silu(gate)).astype(BF16)


def _rglru_kernel(g, x_ref, gm_ref, w_ref, cw_ref, cb_ref, wa_ref, ba_ref, wx_ref, bx_ref, lam_ref,
                  hin_ref, cin_ref, br_ref, hout_ref, cout_ref, carry_ref):
    R, SEG = g.R, g.SEG

    @pl.when(pl.program_id(1) == 0)
    def _():
        hout_ref[...] = hin_ref[...]
    xn = _rmsnorm(x_ref[...], gm_ref[...]).astype(BF16)
    z = _dot(xn, w_ref[...])
    rx = z[:, :RG_WIDTH]
    ry = z[:, RG_WIDTH:]
    prev8 = _conv_io(rx, cin_ref, cout_ref, carry_ref, g)
    xc = _causal_conv(rx, prev8, cw_ref, cb_ref, g)
    xcb = xc.astype(BF16)
    r = jax.nn.sigmoid(_dot(xcb, wa_ref[...]) + ba_ref[...])
    i = jax.nn.sigmoid(_dot(xcb, wx_ref[...]) + bx_ref[...])
    log_a = RG_C * r * _log_sigmoid(lam_ref[...])
    a = jnp.exp(log_a)
    u = jnp.sqrt(jnp.maximum(1.0 - jnp.exp(2.0 * log_a), 0.0)) * (i * xc)
    t = _mod(_iota((R, 1), 0), SEG)
    d = 1
    while d < SEG:
        ok = t >= d
        a_s = jnp.where(ok, pltpu.roll(a, d, 0), 1.0)
        u_s = jnp.where(ok, pltpu.roll(u, d, 0), 0.0)
        u = a * u_s + u
        a = a * a_s
        d *= 2
    hcur = u + a * _seg_rows(hout_ref, g)
    hout_ref[...] = hcur.reshape(g.NSEG, SEG, RG_WIDTH)[:, SEG - 1:SEG, :]
    br_ref[...] = (hcur * _gelu_tanh(ry)).astype(BF16)


def _mlstm_kernel(g, x_ref, gm_ref, w_ref, cw_ref, cb_ref, bi_ref, bf_ref, gn_ref,
                  cin_ref, nin_ref, min_ref, convin_ref,
                  br_ref, cout_ref, nout_ref, mout_ref, convout_ref,
                  carry_ref, q_s, wkT_s, wk_s, v_s, sc_s, qc_s, qn_s, hm_s):
    R, SEG, NSEG = g.R, g.SEG, g.NSEG
    H, dh = ML_HEADS, ML_DH

    @pl.when(pl.program_id(1) == 0)
    def _():
        cout_ref[...] = cin_ref[...]
        nout_ref[...] = nin_ref[...]
        mout_ref[...] = min_ref[...]
    xn = _rmsnorm(x_ref[...], gm_ref[...]).astype(BF16)
    z = _dot(xn, w_ref[...])
    qk_pre = z[:, 0:2 * ML_W]
    mv = z[:, 2 * ML_W:3 * ML_W]
    mo = z[:, 3 * ML_W:4 * ML_W]
    ipre = z[:, 4 * ML_W:4 * ML_W + LANES] + bi_ref[...]
    fpre = z[:, 4 * ML_W + LANES:4 * ML_W + 2 * LANES] + bf_ref[...]
    prev8 = _conv_io(qk_pre, convin_ref, convout_ref, carry_ref, g)
    qk = _silu(_causal_conv(qk_pre, prev8, cw_ref, cb_ref, g))
    mq = qk[:, :ML_W]
    mk = qk[:, ML_W:] * (dh ** -0.5)

    keep, tri = _seg_tri(g)
    b = _dot3(tri, _log_sigmoid(fpre))
    m_rows = _seg_rows(mout_ref, g)
    prev = b + m_rows
    gT = (ipre - b).T
    bl = _block_last(b, SEG)
    wlog = bl - b + ipre
    wmax = jnp.broadcast_to(jnp.max(wlog.reshape(NSEG, SEG, LANES), axis=1, keepdims=True),
                            (NSEG, SEG, LANES)).reshape(R, LANES)
    m_new = jnp.maximum(bl + m_rows, wmax)
    wgt = jnp.exp(wlog - m_new)
    sc_s[...] = jnp.exp(bl + m_rows - m_new)
    mout_ref[...] = m_new.reshape(NSEG, SEG, LANES)[:, 0:1, :]

    q_s[...] = mq
    v_s[...] = mv.astype(BF16)
    wk = jnp.concatenate([wgt[:, h:h + 1] * mk[:, h * dh:(h + 1) * dh] for h in range(H)], axis=1)
    wk_s[...] = wk
    wkT_s[...] = wk.T.astype(BF16)
    lane = _iota((1, R), 1)

    for h in range(H):
        hs = slice(h * dh, (h + 1) * dh)
        logd = jnp.where(keep, b[:, h:h + 1] + gT[h:h + 1, :], NEG)
        mt = jnp.maximum(prev[:, h:h + 1], jnp.max(logd, axis=1, keepdims=True))
        dm = jnp.where(keep, jnp.exp(logd - mt), 0.0)
        sp = jnp.exp(prev[:, h:h + 1] - mt)
        s_mat = _dot_nt(mq[:, hs].astype(BF16), mk[:, hs].astype(BF16)) * dm
        num = _dot(s_mat.astype(BF16), v_s[:, hs])
        den = jnp.sum(s_mat, axis=1, keepdims=True)

        def seg_body(sg, carry, h=h, hs=hs):
            r0 = sg * SEG if isinstance(sg, int) else pl.multiple_of(sg * SEG, SEG)
            C = cout_ref[sg, h]
            n = nout_ref[sg, pl.ds(h, 1), :]
            qb = q_s[pl.ds(r0, SEG), hs]
            qc_s[pl.ds(r0, SEG), :] = _dot(qb.astype(BF16), C.astype(BF16))
            qn_s[pl.ds(r0, SEG), :] = jnp.broadcast_to(jnp.sum(qb * n, axis=1, keepdims=True), (SEG, LANES))
            kt = wkT_s[hs, :]
            if NSEG > 1:
                kt = jnp.where(_div(lane, SEG) == sg, kt, jnp.zeros_like(kt))
            U = _dot(kt, v_s[:, hs])
            sc = sc_s[pl.ds(r0, 1), h:h + 1]
            cout_ref[sg, h] = sc * C + U
            nout_ref[sg, pl.ds(h, 1), :] = sc * n + jnp.sum(wk_s[pl.ds(r0, SEG), hs], axis=0, keepdims=True)
            return carry

        if NSEG == 1:
            seg_body(0, 0)
        else:
            lax.fori_loop(0, NSEG, seg_body, 0)
        num = num + sp * qc_s[...]
        den = den + sp * qn_s[:, 0:1]
        hm_s[:, hs] = num / jnp.maximum(jnp.abs(den), jnp.exp(-mt))

    hm = jax.nn.sigmoid(mo) * hm_s[...]
    br_ref[...] = _head_rmsnorm(hm, gn_ref[...], H, dh).astype(BF16)


def _merge_kernel(x_ref, gm_ref, wg_ref, mb_ref, b0_ref, b1_ref, b2_ref, b3_ref,
                  w0_ref, w1_ref, w2_ref, w3_ref, wo_ref, y_ref):
    x = x_ref[...]
    xn = _rmsnorm(x, gm_ref[...]).astype(BF16)
    merged = None
    for j, (b_ref, w_ref) in enumerate(((b0_ref, w0_ref), (b1_ref, w1_ref), (b2_ref, w2_ref), (b3_ref, w3_ref))):
        gt = jax.nn.sigmoid(_dot(xn, wg_ref[:, j * D_MODEL:(j + 1) * D_MODEL]) + mb_ref[j:j + 1, :])
        term = gt * _dot(b_ref[...], w_ref[...])
        merged = term if merged is None else merged + term
    y_ref[...] = x + _dot(merged.astype(BF16), wo_ref[...])


def _mlp_kernel(final, x_ref, gm_ref, wu_ref, wd_ref, gf_ref, y_ref):
    x = x_ref[...]
    xn = _rmsnorm(x, gm_ref[...]).astype(BF16)
    hdn = jnp.square(jnp.maximum(_dot(xn, wu_ref[...]), 0.0))
    y = x + _dot(hdn.astype(BF16), wd_ref[...])
    if final:
        y = _rmsnorm(y, gf_ref[...])
    y_ref[...] = y


def _const(shape):
    nd = len(shape)
    return pl.BlockSpec(shape, lambda bi, ti, _nd=nd: (0,) * _nd)


def _rows(g, C):
    return pl.BlockSpec((g.R, C), lambda bi, ti, _nt=g.NT: (bi * _nt + ti, 0))


def _state(g, tail):
    nd = len(tail)
    return pl.BlockSpec((g.Bb,) + tuple(tail), lambda bi, ti, _nd=nd: (bi,) + (0,) * _nd)


def _params():
    return pltpu.CompilerParams(dimension_semantics=("arbitrary", "arbitrary"), vmem_limit_bytes=VMEM_LIMIT)


def _call_gl(kern, name, g, x, consts, state, H, dk, dv):
    n = g.B * g.T
    in_specs = [_rows(g, D_MODEL)] + [_const(c.shape) for c in consts] + [_state(g, (H, dk, dv))]
    return pl.pallas_call(
        functools.partial(kern, g),
        grid=(g.B // g.Bb, g.NT),
        in_specs=in_specs,
        out_specs=[_rows(g, H * dv), _state(g, (H, dk, dv))],
        out_shape=[jax.ShapeDtypeStruct((n, H * dv), BF16), jax.ShapeDtypeStruct((g.B, H, dk, dv), F32)],
        scratch_shapes=_gl_scratch(g, H, dk, dv),
        compiler_params=_params(),
        name=name,
    )(x, *consts, state)


def _call_rglru(g, x, consts, h0, conv8):
    n = g.B * g.T
    return pl.pallas_call(
        functools.partial(_rglru_kernel, g),
        grid=(g.B // g.Bb, g.NT),
        in_specs=[_rows(g, D_MODEL)] + [_const(c.shape) for c in consts]
        + [_state(g, (1, RG_WIDTH)), _state(g, (SUBLANES, RG_WIDTH))],
        out_specs=[_rows(g, RG_WIDTH), _state(g, (1, RG_WIDTH)), _state(g, (SUBLANES, RG_WIDTH))],
        out_shape=[jax.ShapeDtypeStruct((n, RG_WIDTH), BF16), jax.ShapeDtypeStruct((g.B, 1, RG_WIDTH), F32),
                   jax.ShapeDtypeStruct((g.B, SUBLANES, RG_WIDTH), F32)],
        scratch_shapes=[pltpu.VMEM((SUBLANES, RG_WIDTH), F32)],
        compiler_params=_params(),
        name="rglru",
    )(x, *consts, h0, conv8)


def _call_mlstm(g, x, consts, C0, n0, m0, conv8):
    n = g.B * g.T
    H, dh, R = ML_HEADS, ML_DH, g.R
    return pl.pallas_call(
        functools.partial(_mlstm_kernel, g),
        grid=(g.B // g.Bb, g.NT),
        in_specs=[_rows(g, D_MODEL)] + [_const(c.shape) for c in consts]
        + [_state(g, (H, dh, dh)), _state(g, (H, dh)), _state(g, (1, LANES)), _state(g, (SUBLANES, 2 * ML_W))],
        out_specs=[_rows(g, ML_W), _state(g, (H, dh, dh)), _state(g, (H, dh)), _state(g, (1, LANES)),
                   _state(g, (SUBLANES, 2 * ML_W))],
        out_shape=[jax.ShapeDtypeStruct((n, ML_W), BF16), jax.ShapeDtypeStruct((g.B, H, dh, dh), F32),
                   jax.ShapeDtypeStruct((g.B, H, dh), F32), jax.ShapeDtypeStruct((g.B, 1, LANES), F32),
                   jax.ShapeDtypeStruct((g.B, SUBLANES, 2 * ML_W), F32)],
        scratch_shapes=[pltpu.VMEM((SUBLANES, 2 * ML_W), F32), pltpu.VMEM((R, ML_W), F32),
                        pltpu.VMEM((ML_W, R), BF16), pltpu.VMEM((R, ML_W), F32), pltpu.VMEM((R, ML_W), BF16),
                        pltpu.VMEM((R, LANES), F32), pltpu.VMEM((R, dh), F32), pltpu.VMEM((R, LANES), F32),
                        pltpu.VMEM((R, ML_W), F32)],
        compiler_params=_params(),
        name="mlstm",
    )(x, *consts, C0, n0, m0, conv8)


def _tok_spec(C):
    return pl.BlockSpec((ROW_TILE, C), lambda i: (i, 0))


def _tok_const(shape):
    nd = len(shape)
    return pl.BlockSpec(shape, lambda i, _nd=nd: (0,) * _nd)


def _call_merge(x, gm, wg, mb, brs, wbrs, wo):
    n = x.shape[0]
    consts_a = [gm, wg, mb]
    consts_b = list(wbrs) + [wo]
    return pl.pallas_call(
        _merge_kernel,
        grid=(n // ROW_TILE,),
        in_specs=[_tok_spec(D_MODEL)] + [_tok_const(c.shape) for c in consts_a]
        + [_tok_spec(b.shape[1]) for b in brs] + [_tok_const(c.shape) for c in consts_b],
        out_specs=_tok_spec(D_MODEL),
        out_shape=jax.ShapeDtypeStruct((n, D_MODEL), F32),
        compiler_params=pltpu.CompilerParams(dimension_semantics=("arbitrary",), vmem_limit_bytes=VMEM_LIMIT),
        name="merge",
    )(x, *consts_a, *brs, *consts_b)


def _call_mlp(x, gm, wu, wd, gf, final):
    n = x.shape[0]
    consts = [gm, wu, wd, gf]
    return pl.pallas_call(
        functools.partial(_mlp_kernel, final),
        grid=(n // ROW_TILE,),
        in_specs=[_tok_spec(D_MODEL)] + [_tok_const(c.shape) for c in consts],
        out_specs=_tok_spec(D_MODEL),
        out_shape=jax.ShapeDtypeStruct((n, D_MODEL), F32),
        compiler_params=pltpu.CompilerParams(dimension_semantics=("arbitrary",), vmem_limit_bytes=VMEM_LIMIT),
        name="mlp",
    )(x, *consts)


def _cols(w, *names):
    return [w[:, _OFF[n][0]:_OFF[n][1]] for n in names]


def _pad_cols(w, width):
    return jnp.pad(w, ((0, 0), (0, width - w.shape[1])))


def _block_diag(w):
    nb, d, e = w.shape
    return (jnp.eye(nb, dtype=w.dtype)[:, None, :, None] * w[:, :, None, :]).reshape(nb * d, nb * e)


def _row(v):
    return v.reshape(1, -1).astype(F32)


def _pad8(conv_state):
    return jnp.pad(conv_state, ((0, 0), (SUBLANES - (CONV_W - 1), 0), (0, 0)))


def _layer_weights(l, P, lbs):
    w = P["w_in"][l]
    W = {}
    W["gm"] = _row(P["norm_mix"][l])
    W["w_gla"] = jnp.concatenate(_cols(w, "g_q", "g_k", "g_v", "g_g") + [_pad_cols(_cols(w, "g_lr")[0], LANES)],
                                 axis=1).astype(BF16)
    W["wlr"] = jnp.pad(P["gla_w_lr"][l], ((0, LANES - GLA_RANK), (0, 0))).astype(BF16)
    W["blr"] = _row(P["gla_b_lr"][l])
    W["gla_norm"] = _row(P["gla_norm"][l])
    W["w_rg"] = jnp.concatenate(_cols(w, "r_x", "r_y"), axis=1).astype(BF16)
    W["rg_cw"] = P["rg_conv_w"][l].astype(F32)
    W["rg_cb"] = _row(P["rg_conv_b"][l])
    W["rg_wa"] = _block_diag(P["rg_wa"][l]).astype(BF16)
    W["rg_ba"] = _row(P["rg_ba"][l])
    W["rg_wx"] = _block_diag(P["rg_wx"][l]).astype(BF16)
    W["rg_bx"] = _row(P["rg_bx"][l])
    W["rg_lam"] = _row(P["rg_lam"][l])
    W["w_ml"] = jnp.concatenate(_cols(w, "m_q", "m_k", "m_v", "m_o")
                                + [_pad_cols(_cols(w, "m_i")[0], LANES), _pad_cols(_cols(w, "m_f")[0], LANES)],
                                axis=1).astype(BF16)
    W["ml_cw"] = P["ml_conv_w"][l].astype(F32)
    W["ml_cb"] = _row(P["ml_conv_b"][l])
    W["ml_bi"] = _pad_cols(_row(P["ml_b_i"][l]), LANES)
    W["ml_bf"] = _pad_cols(_row(P["ml_b_f"][l]), LANES)
    W["ml_norm"] = _row(P["ml_norm"][l])
    W["w_hg"] = jnp.concatenate(_cols(w, "h_q", "h_f", "h_i", "h_g"), axis=1).astype(BF16)
    W["hg_lb"] = _row(lbs[l])
    W["hg_norm"] = _row(P["hg_norm"][l])
    W["w_gates"] = _cols(w, "gates")[0].astype(BF16)
    W["merge_b"] = P["merge_b"][l].astype(F32)
    W["w_br"] = [P[n][l].astype(BF16) for n in ("w_br_gla", "w_br_rg", "w_br_ml", "w_br_hg")]
    W["w_out"] = P["w_out"][l].astype(BF16)
    W["gf"] = _row(P["norm_ffn"][l])
    W["w_up"] = P["w_up"][l].astype(BF16)
    W["w_down"] = P["w_down"][l].astype(BF16)
    return W


def _trunk(x3, states, LW, norm_final):
    B, T, _ = x3.shape
    g = _geom(B, T)
    x = x3.reshape(B * T, D_MODEL)
    s_gla, h_rg, conv_rg, c_ml, n_ml, m_ml, conv_ml, s_hg = states
    new = [[] for _ in range(8)]
    for l in range(DEPTH):
        W = LW[l]
        br_gla, s_gla_new = _call_gl(_gla_kernel, "gla", g, x,
                                     [W["gm"], W["w_gla"], W["wlr"], W["blr"], W["gla_norm"]],
                                     s_gla[l], GLA_HEADS, GLA_DK, GLA_DV)
        br_rg, h_new, conv_rg_new = _call_rglru(
            g, x, [W["gm"], W["w_rg"], W["rg_cw"], W["rg_cb"], W["rg_wa"], W["rg_ba"], W["rg_wx"], W["rg_bx"],
                   W["rg_lam"]],
            h_rg[l][:, None, :], _pad8(conv_rg[l]))
        br_ml, c_new, n_new, m_new, conv_ml_new = _call_mlstm(
            g, x, [W["gm"], W["w_ml"], W["ml_cw"], W["ml_cb"], W["ml_bi"], W["ml_bf"], W["ml_norm"]],
            c_ml[l], n_ml[l], _pad_cols(m_ml[l], LANES)[:, None, :], _pad8(conv_ml[l]))
        br_hg, s_hg_new = _call_gl(_hgrn_kernel, "hgrn", g, x,
                                   [W["gm"], W["w_hg"], W["hg_lb"], W["hg_norm"]],
                                   s_hg[l], HG_HEADS, HG_DK, HG_DV)
        x = _call_merge(x, W["gm"], W["w_gates"], W["merge_b"], [br_gla, br_rg, br_ml, br_hg], W["w_br"], W["w_out"])
        x = _call_mlp(x, W["gf"], W["w_up"], W["w_down"], _row(norm_final), final=(l == DEPTH - 1))
        outs = (s_gla_new, h_new[:, 0, :], conv_rg_new[:, SUBLANES - (CONV_W - 1):, :], c_new, n_new,
                m_new[:, 0, :ML_HEADS], conv_ml_new[:, SUBLANES - (CONV_W - 1):, :], s_hg_new)
        for j in range(8):
            new[j].append(outs[j])
    return x.reshape(B, T, D_MODEL), tuple(jnp.stack(n_) for n_ in new)


def _zero_states(B):
    return (jnp.zeros((DEPTH, B, GLA_HEADS, GLA_DK, GLA_DV), F32),
            jnp.zeros((DEPTH, B, RG_WIDTH), F32),
            jnp.zeros((DEPTH, B, CONV_W - 1, RG_WIDTH), F32),
            jnp.zeros((DEPTH, B, ML_HEADS, ML_DH, ML_DH), F32),
            jnp.zeros((DEPTH, B, ML_HEADS, ML_DH), F32),
            jnp.zeros((DEPTH, B, ML_HEADS), F32),
            jnp.zeros((DEPTH, B, CONV_W - 1, 2 * ML_W), F32),
            jnp.zeros((DEPTH, B, HG_HEADS, HG_DK, HG_DV), F32))


def kernel(x_prompt, x_sample, state_gla, state_rglru_h, state_rglru_conv, state_mlstm_C, state_mlstm_n, state_mlstm_m, state_mlstm_conv, state_hgrn, norm_mix, w_in, gla_w_lr, gla_b_lr, gla_norm, rg_conv_w, rg_conv_b, rg_wa, rg_ba, rg_wx, rg_bx, rg_lam, ml_conv_w, ml_conv_b, ml_b_i, ml_b_f, ml_norm, hg_gamma, hg_norm, merge_b, w_br_gla, w_br_rg, w_br_ml, w_br_hg, w_out, norm_ffn, w_up, w_down, norm_final):
    P = dict(norm_mix=norm_mix, w_in=w_in, gla_w_lr=gla_w_lr, gla_b_lr=gla_b_lr, gla_norm=gla_norm,
             rg_conv_w=rg_conv_w, rg_conv_b=rg_conv_b, rg_wa=rg_wa, rg_ba=rg_ba, rg_wx=rg_wx, rg_bx=rg_bx,
             rg_lam=rg_lam, ml_conv_w=ml_conv_w, ml_conv_b=ml_conv_b, ml_b_i=ml_b_i, ml_b_f=ml_b_f,
             ml_norm=ml_norm, hg_norm=hg_norm, merge_b=merge_b, w_br_gla=w_br_gla, w_br_rg=w_br_rg,
             w_br_ml=w_br_ml, w_br_hg=w_br_hg, w_out=w_out, norm_ffn=norm_ffn, w_up=w_up, w_down=w_down)
    sm = jax.nn.softmax(hg_gamma.astype(F32), axis=0)
    lbs = jnp.concatenate([jnp.zeros_like(sm[:1]), jnp.cumsum(sm, axis=0)[:-1]], axis=0)
    lbs = jnp.clip(lbs, 0.0, 1.0 - 1e-6)
    LW = [_layer_weights(l, P, lbs) for l in range(DEPTH)]

    y_prompt, p_st = _trunk(x_prompt, _zero_states(x_prompt.shape[0]), LW, norm_final)
    s_in = (state_gla, state_rglru_h, state_rglru_conv, state_mlstm_C, state_mlstm_n, state_mlstm_m,
            state_mlstm_conv, state_hgrn)
    y_sample, s_st = _trunk(x_sample, s_in, LW, norm_final)
    return (y_prompt, y_sample) + tuple(p_st) + tuple(s_st)
```

```python
import collections
import functools

import jax
import jax.numpy as jnp
from jax import lax
from jax.experimental import pallas as pl
from jax.experimental.pallas import tpu as pltpu

F32 = jnp.float32
BF16 = jnp.bfloat16

D_MODEL = 1024
DEPTH = 2
GLA_HEADS, GLA_DK, GLA_DV, GLA_RANK = 4, 64, 128, 16
GLA_KW, GLA_W = GLA_HEADS * GLA_DK, GLA_HEADS * GLA_DV
GLA_GATE_TAU = 16.0
RG_WIDTH, RG_BLOCKS, RG_C = 512, 8, 8.0
CONV_W = 4
ML_HEADS, ML_DH = 4, 128
ML_W = ML_HEADS * ML_DH
HG_HEADS, HG_DK, HG_DV = 4, 128, 128
HG_W = HG_HEADS * HG_DV
N_BRANCH = 4
D_FF = 4 * D_MODEL
EPS = 1e-6
NEG = -1e30

LANES = 128
SUBLANES = 8
MXU_DIM = 256
ROW_TILE = 256
BAND = SUBLANES
LOG2E = 1.4426950408889634
VMEM_LIMIT = 56 * 1024 * 1024

_OFF = {}
_o = 0
for _name, _size in (("g_q", GLA_KW), ("g_k", GLA_KW), ("g_v", GLA_W), ("g_lr", GLA_RANK), ("g_g", GLA_W),
                     ("r_x", RG_WIDTH), ("r_y", RG_WIDTH),
                     ("m_q", ML_W), ("m_k", ML_W), ("m_v", ML_W), ("m_o", ML_W), ("m_i", ML_HEADS), ("m_f", ML_HEADS),
                     ("h_q", HG_W), ("h_f", HG_W), ("h_i", HG_W), ("h_g", HG_W),
                     ("gates", N_BRANCH * D_MODEL)):
    _OFF[_name] = (_o, _o + _size)
    _o += _size

Geom = collections.namedtuple("Geom", "B T Bb Tt R SEG NSEG NT")


def _geom(B, T):
    if T >= ROW_TILE:
        Bb, Tt = 1, ROW_TILE
    else:
        Bb, Tt = ROW_TILE // T, T
    assert T % Tt == 0 and B % Bb == 0 and Tt % SUBLANES == 0 and Tt >= SUBLANES
    return Geom(B, T, Bb, Tt, Bb * Tt, Tt, Bb, T // Tt)


def _dot(a, b):
    return jnp.dot(a, b, preferred_element_type=F32)


def _dot_nt(a, b):
    return lax.dot_general(a, b, (((1,), (1,)), ((), ())), preferred_element_type=F32)


def _dot3(m01, x):
    hi = x.astype(BF16)
    r1 = x - hi.astype(F32)
    mid = r1.astype(BF16)
    lo = (r1 - mid.astype(F32)).astype(BF16)
    return _dot(m01, hi) + (_dot(m01, mid) + _dot(m01, lo))


def _iota(shape, dim):
    return lax.broadcasted_iota(jnp.int32, shape, dim)


def _div(x, n):
    assert n & (n - 1) == 0
    return x >> (n.bit_length() - 1)


def _mod(x, n):
    assert n & (n - 1) == 0
    return x & (n - 1)


def _log_sigmoid(x):
    return jnp.minimum(x, 0.0) - jnp.log(1.0 + jnp.exp(-jnp.abs(x)))


def _silu(x):
    return x * jax.nn.sigmoid(x)


def _gelu_tanh(x):
    return 0.5 * x * (1.0 + jnp.tanh(0.7978845608028654 * (x + 0.044715 * (x * x * x))))


def _rmsnorm(x, g):
    return x * lax.rsqrt(jnp.mean(x * x, axis=-1, keepdims=True) + EPS) * g


def _head_rmsnorm(o, gn, H, d):
    outs = []
    for h in range(H):
        oh = o[:, h * d:(h + 1) * d]
        outs.append(oh * lax.rsqrt(jnp.mean(oh * oh, axis=-1, keepdims=True) + EPS) * gn[:, h * d:(h + 1) * d])
    return jnp.concatenate(outs, axis=1)


def _block_last(c, w):
    R, C = c.shape
    if w == R:
        return jnp.broadcast_to(c[R - 1:R, :], (R, C))
    c3 = c.reshape(R // w, w, C)
    return jnp.broadcast_to(c3[:, w - 1:w, :], (R // w, w, C)).reshape(R, C)


def _seg_tri(g):
    row = _iota((g.R, g.R), 0)
    col = _iota((g.R, g.R), 1)
    keep = (col <= row) & (_div(row, g.SEG) == _div(col, g.SEG))
    return keep, keep.astype(BF16)


def _seg_rows(ref3, g):
    v = ref3[...]
    C = v.shape[-1]
    return jnp.broadcast_to(v, (g.NSEG, g.SEG, C)).reshape(g.R, C)


def _causal_conv(x, prev8, w_ref, b_ref, g):
    R, C = x.shape
    nb = R // SUBLANES
    sub = _iota((1, SUBLANES, 1), 1)
    x3 = x.reshape(nb, SUBLANES, C)
    p3 = prev8.reshape(nb, SUBLANES, C)
    y = b_ref[...] + x3 * w_ref[CONV_W - 1:CONV_W, :]
    for s in range(1, CONV_W):
        xs = jnp.where(sub >= s, pltpu.roll(x3, s, 1), pltpu.roll(p3, s, 1))
        y = y + xs * w_ref[CONV_W - 1 - s:CONV_W - s, :]
    return y.reshape(R, C)


def _conv_io(x, conv_in_ref, conv_out_ref, carry_ref, g):
    R = g.R
    ti = pl.program_id(1)
    if g.NSEG == 1:
        @pl.when(ti == 0)
        def _():
            carry_ref[...] = conv_in_ref[0]
        prev8 = jnp.concatenate([carry_ref[...], x[:R - SUBLANES, :]], axis=0)
        last8 = x[R - SUBLANES:, :]
        carry_ref[...] = last8
        conv_out_ref[0] = last8
    else:
        C = x.shape[-1]
        prev8 = conv_in_ref[...].reshape(R, C)
        conv_out_ref[...] = x.reshape(g.NSEG, g.SEG, C)
    return prev8


def _gl_core(q, k, v, la, s_ref, scr, g, H, dk, dv):
    qd_s, kdT_s, v_s, o_s, decT_s = scr
    R, SEG, NSEG = g.R, g.SEG, g.NSEG
    HK = H * dk
    _, tri = _seg_tri(g)
    c = _dot3(tri, la)
    cl = _block_last(c, SEG)

    bw = BAND
    HV = H * dv
    c2 = c * LOG2E
    hpg = MXU_DIM // dk
    e_r = _div(_iota((MXU_DIM, hpg * dv), 0), dk)
    e_c = _div(_iota((MXU_DIM, hpg * dv), 1), dv)
    ones_bd = (e_r == e_c).astype(BF16)
    nb = R // bw
    sub = _iota((1, bw, 1), 1)
    q3, k3, c3, v3 = (a.reshape(nb, bw, a.shape[1]) for a in (q, k, c2, v))
    o = None
    for d in range(bw):
        if d == 0:
            p3, vd = q3 * k3, v3
        else:
            diff = jnp.where(sub >= d, c3 - pltpu.roll(c3, d, 1), NEG)
            p3 = q3 * pltpu.roll(k3, d, 1) * jnp.exp2(diff)
            vd = pltpu.roll(v3, d, 1)
        pb = p3.reshape(R, HK).astype(BF16)
        parts = [_dot(pb[:, i * MXU_DIM:(i + 1) * MXU_DIM], ones_bd) for i in range(HK // MXU_DIM)]
        term = (parts[0] if len(parts) == 1 else jnp.concatenate(parts, axis=1)) * vd.reshape(R, HV)
        o = term if o is None else o + term

    vb = v.astype(BF16)
    w = bw
    a_off = [None] * H
    rowq = _iota((R, R), 0)
    colq = _iota((R, R), 1)
    while 2 * w <= SEG:
        nb2 = R // (2 * w)
        c4 = c2.reshape(nb2, 2, w, HK)
        pmid = c4[:, 0:1, w - 1:w, :]
        xk = k.reshape(nb2, 2, w, HK)[:, 0:1] * jnp.exp2(pmid - c4[:, 0:1])
        xq = q.reshape(nb2, 2, w, HK)[:, 1:2] * jnp.exp2(c4[:, 1:2] - pmid)
        x = jnp.concatenate([xk, xq], axis=1).reshape(R, HK).astype(BF16)
        rb = _div(rowq, w)
        m = (_mod(rb, 2) == 1) & (_div(colq, w) == rb - 1)
        for h in range(H):
            xh = x[:, h * dk:(h + 1) * dk]
            a = _dot_nt(xh, xh)
            a_off[h] = jnp.where(m, a, 0.0 if a_off[h] is None else a_off[h])
        w *= 2
    if a_off[0] is not None:
        o = o + jnp.concatenate([_dot(a_off[h].astype(BF16), vb[:, h * dv:(h + 1) * dv]) for h in range(H)], axis=1)

    qd_s[...] = q * jnp.exp(c)
    kdT_s[...] = (k * jnp.exp(cl - c)).T.astype(BF16)
    v_s[...] = vb
    decT_s[...] = jnp.exp(cl).T
    o_s[...] = o
    lane = _iota((1, R), 1)

    def seg_body(sg, carry):
        r0 = sg * SEG if isinstance(sg, int) else pl.multiple_of(sg * SEG, SEG)
        for h in range(H):
            S = s_ref[sg, h]
            qd = qd_s[pl.ds(r0, SEG), h * dk:(h + 1) * dk].astype(BF16)
            o_s[pl.ds(r0, SEG), h * dv:(h + 1) * dv] += _dot(qd, S.astype(BF16))
            kt = kdT_s[h * dk:(h + 1) * dk, :]
            if NSEG > 1:
                kt = jnp.where(_div(lane, SEG) == sg, kt, jnp.zeros_like(kt))
            U = _dot(kt, v_s[:, h * dv:(h + 1) * dv])
            dec = jnp.sum(jnp.where(lane == r0, decT_s[h * dk:(h + 1) * dk, :], 0.0), axis=1, keepdims=True)
            s_ref[sg, h] = dec * S + U
        return carry

    if NSEG == 1:
        seg_body(0, 0)
    else:
        lax.fori_loop(0, NSEG, seg_body, 0)
    return o_s[...]


def _gl_scratch(g, H, dk, dv):
    return [pltpu.VMEM((g.R, H * dk), F32), pltpu.VMEM((H * dk, g.R), BF16), pltpu.VMEM((g.R, H * dv), BF16),
            pltpu.VMEM((g.R, H * dv), F32), pltpu.VMEM((H * dk, g.R), F32)]


def _gla_kernel(g, x_ref, gm_ref, w_ref, wlr_ref, blr_ref, gn_ref, sin_ref, br_ref, sout_ref, *scr):
    @pl.when(pl.program_id(1) == 0)
    def _():
        sout_ref[...] = sin_ref[...]
    xn = _rmsnorm(x_ref[...], gm_ref[...]).astype(BF16)
    z = _dot(xn, w_ref[...])
    q = z[:, 0:GLA_KW] * (GLA_DK ** -0.5)
    k = z[:, GLA_KW:2 * GLA_KW]
    v = z[:, 2 * GLA_KW:2 * GLA_KW + GLA_W]
    gate = z[:, 2 * GLA_KW + GLA_W:2 * GLA_KW + 2 * GLA_W]
    lr = z[:, 2 * GLA_KW + 2 * GLA_W:]
    la = _log_sigmoid(_dot(lr.astype(BF16), wlr_ref[...]) + blr_ref[...]) * (1.0 / GLA_GATE_TAU)
    o = _gl_core(q, k, v, la, sout_ref, scr, g, GLA_HEADS, GLA_DK, GLA_DV)
    br_ref[...] = (_head_rmsnorm(o, gn_ref[...], GLA_HEADS, GLA_DV) * _silu(gate)).astype(BF16)


def _hgrn_kernel(g, x_ref, gm_ref, w_ref, lb_ref, gn_ref, sin_ref, br_ref, sout_ref, *scr):
    @pl.when(pl.program_id(1) == 0)
    def _():
        sout_ref[...] = sin_ref[...]
    xn = _rmsnorm(x_ref[...], gm_ref[...]).astype(BF16)
    z = _dot(xn, w_ref[...])
    hq = z[:, 0:HG_W]
    fp = z[:, HG_W:2 * HG_W]
    v = z[:, 2 * HG_W:3 * HG_W]
    gate = z[:, 3 * HG_W:4 * HG_W]
    lb = lb_ref[...]
    ls = _log_sigmoid(fp)
    pos = lb > 0.0
    a_ = jnp.log(jnp.where(pos, lb, 1.0))
    b_ = jnp.log1p(-lb) + ls
    lae = jnp.maximum(a_, b_) + jnp.log(1.0 + jnp.exp(-jnp.abs(a_ - b_)))
    la = jnp.where(pos, lae, ls)
    k = (1.0 - lb) * jax.nn.sigmoid(-fp)
    q = _silu(hq)
    o = _gl_core(q, k, v, la, sout_ref, scr, g, HG_HEADS, HG_DK, HG_DV)
    br_ref[...] = (_head_rmsnorm(o, gn_ref[...], HG_HEADS, HG_DV) * _silu(gate)).astype(BF16)


def _rglru_kernel(g, x_ref, gm_ref, w_ref, cw_ref, cb_ref, wa_ref, ba_ref, wx_ref, bx_ref, lam_ref,
                  hin_ref, cin_ref, br_ref, hout_ref, cout_ref, carry_ref):
    R, SEG = g.R, g.SEG

    @pl.when(pl.program_id(1) == 0)
    def _():
        hout_ref[...] = hin_ref[...]
    xn = _rmsnorm(x_ref[...], gm_ref[...]).astype(BF16)
    z = _dot(xn, w_ref[...])
    rx = z[:, :RG_WIDTH]
    ry = z[:, RG_WIDTH:]
    prev8 = _conv_io(rx, cin_ref, cout_ref, carry_ref, g)
    xc = _causal_conv(rx, prev8, cw_ref, cb_ref, g)
    xcb = xc.astype(BF16)
    r = jax.nn.sigmoid(_dot(xcb, wa_ref[...]) + ba_ref[...])
    i = jax.nn.sigmoid(_dot(xcb, wx_ref[...]) + bx_ref[...])
    log_a = RG_C * r * _log_sigmoid(lam_ref[...])
    a = jnp.exp(log_a)
    u = jnp.sqrt(jnp.maximum(1.0 - a * a, 0.0)) * (i * xc)
    nbs = SEG // SUBLANES
    sub = _iota((1, SUBLANES, 1), 1)
    a3 = a.reshape(R // SUBLANES, SUBLANES, RG_WIDTH)
    u3 = u.reshape(R // SUBLANES, SUBLANES, RG_WIDTH)
    d = 1
    while d < SUBLANES:
        ok = sub >= d
        a_s = jnp.where(ok, pltpu.roll(a3, d, 1), 1.0)
        u_s = jnp.where(ok, pltpu.roll(u3, d, 1), 0.0)
        u3 = a3 * u_s + u3
        a3 = a3 * a_s
        d *= 2
    a4 = a3.reshape(g.NSEG, nbs, SUBLANES, RG_WIDTH)
    u4 = u3.reshape(g.NSEG, nbs, SUBLANES, RG_WIDTH)
    hc = hout_ref[...]
    blocks = []
    for j in range(nbs):
        hb = u4[:, j] + a4[:, j] * hc
        blocks.append(hb)
        hc = hb[:, SUBLANES - 1:SUBLANES, :]
    hout_ref[...] = hc
    hcur = (blocks[0] if nbs == 1 else jnp.stack(blocks, axis=1)).reshape(R, RG_WIDTH)
    br_ref[...] = (hcur * _gelu_tanh(ry)).astype(BF16)


def _mlstm_kernel(g, x_ref, gm_ref, w_ref, cw_ref, cb_ref, bi_ref, bf_ref, gn_ref,
                  cin_ref, nin_ref, min_ref, convin_ref,
                  br_ref, cout_ref, nout_ref, mout_ref, convout_ref,
                  carry_ref, q_s, wkT_s, wk_s, v_s, sc_s, qc_s, qn_s, hm_s):
    R, SEG, NSEG = g.R, g.SEG, g.NSEG
    H, dh = ML_HEADS, ML_DH

    @pl.when(pl.program_id(1) == 0)
    def _():
        cout_ref[...] = cin_ref[...]
        nout_ref[...] = nin_ref[...]
        mout_ref[...] = min_ref[...]
    xn = _rmsnorm(x_ref[...], gm_ref[...]).astype(BF16)
    z = _dot(xn, w_ref[...])
    qk_pre = z[:, 0:2 * ML_W]
    mv = z[:, 2 * ML_W:3 * ML_W]
    mo = z[:, 3 * ML_W:4 * ML_W]
    ipre = z[:, 4 * ML_W:4 * ML_W + LANES] + bi_ref[...]
    fpre = z[:, 4 * ML_W + LANES:4 * ML_W + 2 * LANES] + bf_ref[...]
    prev8 = _conv_io(qk_pre, convin_ref, convout_ref, carry_ref, g)
    qk = _silu(_causal_conv(qk_pre, prev8, cw_ref, cb_ref, g))
    mq = qk[:, :ML_W]
    mk = qk[:, ML_W:] * (dh ** -0.5)

    keep, tri = _seg_tri(g)
    b = _dot3(tri, _log_sigmoid(fpre))
    m_rows = _seg_rows(mout_ref, g)
    prev = b + m_rows
    gT = (ipre - b).T
    bl = _block_last(b, SEG)
    wlog = bl - b + ipre
    wmax = jnp.broadcast_to(jnp.max(wlog.reshape(NSEG, SEG, LANES), axis=1, keepdims=True),
                            (NSEG, SEG, LANES)).reshape(R, LANES)
    m_new = jnp.maximum(bl + m_rows, wmax)
    wgt = jnp.exp(wlog - m_new)
    sc_s[...] = jnp.exp(bl + m_rows - m_new)
    mout_ref[...] = m_new.reshape(NSEG, SEG, LANES)[:, 0:1, :]

    q_s[...] = mq
    v_s[...] = mv.astype(BF16)
    wk = jnp.concatenate([wgt[:, h:h + 1] * mk[:, h * dh:(h + 1) * dh] for h in range(H)], axis=1)
    wk_s[...] = wk
    wkT_s[...] = wk.T.astype(BF16)
    lane = _iota((1, R), 1)

    for h in range(H):
        hs = slice(h * dh, (h + 1) * dh)
        logd = jnp.where(keep, b[:, h:h + 1] + gT[h:h + 1, :], NEG)
        mt = jnp.maximum(prev[:, h:h + 1], jnp.max(logd, axis=1, keepdims=True))
        dm = jnp.where(keep, jnp.exp(logd - mt), 0.0)
        sp = jnp.exp(prev[:, h:h + 1] - mt)
        s_mat = _dot_nt(mq[:, hs].astype(BF16), mk[:, hs].astype(BF16)) * dm
        num = _dot(s_mat.astype(BF16), v_s[:, hs])
        den = jnp.sum(s_mat, axis=1, keepdims=True)

        def seg_body(sg, carry, h=h, hs=hs):
            r0 = sg * SEG if isinstance(sg, int) else pl.multiple_of(sg * SEG, SEG)
            C = cout_ref[sg, h]
            n = nout_ref[sg, pl.ds(h, 1), :]
            qb = q_s[pl.ds(r0, SEG), hs]
            qc_s[pl.ds(r0, SEG), :] = _dot(qb.astype(BF16), C.astype(BF16))
            qn_s[pl.ds(r0, SEG), :] = jnp.broadcast_to(jnp.sum(qb * n, axis=1, keepdims=True), (SEG, LANES))
            kt = wkT_s[hs, :]
            if NSEG > 1:
                kt = jnp.where(_div(lane, SEG) == sg, kt, jnp.zeros_like(kt))
            U = _dot(kt, v_s[:, hs])
            sc = sc_s[pl.ds(r0, 1), h:h + 1]
            cout_ref[sg, h] = sc * C + U
            nout_ref[sg, pl.ds(h, 1), :] = sc * n + jnp.sum(wk_s[pl.ds(r0, SEG), hs], axis=0, keepdims=True)
            return carry

        if NSEG == 1:
            seg_body(0, 0)
        else:
            lax.fori_loop(0, NSEG, seg_body, 0)
        num = num + sp * qc_s[...]
        den = den + sp * qn_s[:, 0:1]
        hm_s[:, hs] = num / jnp.maximum(jnp.abs(den), jnp.exp(-mt))

    hm = jax.nn.sigmoid(mo) * hm_s[...]
    br_ref[...] = _head_rmsnorm(hm, gn_ref[...], H, dh).astype(BF16)


def _merge_kernel(x_ref, gm_ref, wg_ref, mb_ref, b0_ref, b1_ref, b2_ref, b3_ref,
                  w0_ref, w1_ref, w2_ref, w3_ref, wo_ref, y_ref):
    x = x_ref[...]
    xn = _rmsnorm(x, gm_ref[...]).astype(BF16)
    merged = None
    for j, (b_ref, w_ref) in enumerate(((b0_ref, w0_ref), (b1_ref, w1_ref), (b2_ref, w2_ref), (b3_ref, w3_ref))):
        gt = jax.nn.sigmoid(_dot(xn, wg_ref[:, j * D_MODEL:(j + 1) * D_MODEL]) + mb_ref[j:j + 1, :])
        term = gt * _dot(b_ref[...], w_ref[...])
        merged = term if merged is None else merged + term
    y_ref[...] = x + _dot(merged.astype(BF16), wo_ref[...])


def _mlp_kernel(final, x_ref, gm_ref, wu_ref, wd_ref, gf_ref, y_ref):
    x = x_ref[...]
    xn = _rmsnorm(x, gm_ref[...]).astype(BF16)
    hdn = jnp.square(jnp.maximum(_dot(xn, wu_ref[...]), 0.0))
    y = x + _dot(hdn.astype(BF16), wd_ref[...])
    if final:
        y = _rmsnorm(y, gf_ref[...])
    y_ref[...] = y


def _const(shape):
    nd = len(shape)
    return pl.BlockSpec(shape, lambda bi, ti, _nd=nd: (0,) * _nd)


def _rows(g, C):
    return pl.BlockSpec((g.R, C), lambda bi, ti, _nt=g.NT: (bi * _nt + ti, 0))


def _state(g, tail):
    nd = len(tail)
    return pl.BlockSpec((g.Bb,) + tuple(tail), lambda bi, ti, _nd=nd: (bi,) + (0,) * _nd)


def _params():
    return pltpu.CompilerParams(dimension_semantics=("arbitrary", "arbitrary"), vmem_limit_bytes=VMEM_LIMIT)


def _call_gl(kern, name, g, x, consts, state, H, dk, dv):
    n = g.B * g.T
    in_specs = [_rows(g, D_MODEL)] + [_const(c.shape) for c in consts] + [_state(g, (H, dk, dv))]
    return pl.pallas_call(
        functools.partial(kern, g),
        grid=(g.B // g.Bb, g.NT),
        in_specs=in_specs,
        out_specs=[_rows(g, H * dv), _state(g, (H, dk, dv))],
        out_shape=[jax.ShapeDtypeStruct((n, H * dv), BF16), jax.ShapeDtypeStruct((g.B, H, dk, dv), F32)],
        scratch_shapes=_gl_scratch(g, H, dk, dv),
        compiler_params=_params(),
        name=name,
    )(x, *consts, state)


def _call_rglru(g, x, consts, h0, conv8):
    n = g.B * g.T
    return pl.pallas_call(
        functools.partial(_rglru_kernel, g),
        grid=(g.B // g.Bb, g.NT),
        in_specs=[_rows(g, D_MODEL)] + [_const(c.shape) for c in consts]
        + [_state(g, (1, RG_WIDTH)), _state(g, (SUBLANES, RG_WIDTH))],
        out_specs=[_rows(g, RG_WIDTH), _state(g, (1, RG_WIDTH)), _state(g, (SUBLANES, RG_WIDTH))],
        out_shape=[jax.ShapeDtypeStruct((n, RG_WIDTH), BF16), jax.ShapeDtypeStruct((g.B, 1, RG_WIDTH), F32),
                   jax.ShapeDtypeStruct((g.B, SUBLANES, RG_WIDTH), F32)],
        scratch_shapes=[pltpu.VMEM((SUBLANES, RG_WIDTH), F32)],
        compiler_params=_params(),
        name="rglru",
    )(x, *consts, h0, conv8)


def _call_mlstm(g, x, consts, C0, n0, m0, conv8):
    n = g.B * g.T
    H, dh, R = ML_HEADS, ML_DH, g.R
    return pl.pallas_call(
        functools.partial(_mlstm_kernel, g),
        grid=(g.B // g.Bb, g.NT),
        in_specs=[_rows(g, D_MODEL)] + [_const(c.shape) for c in consts]
        + [_state(g, (H, dh, dh)), _state(g, (H, dh)), _state(g, (1, LANES)), _state(g, (SUBLANES, 2 * ML_W))],
        out_specs=[_rows(g, ML_W), _state(g, (H, dh, dh)), _state(g, (H, dh)), _state(g, (1, LANES)),
                   _state(g, (SUBLANES, 2 * ML_W))],
        out_shape=[jax.ShapeDtypeStruct((n, ML_W), BF16), jax.ShapeDtypeStruct((g.B, H, dh, dh), F32),
                   jax.ShapeDtypeStruct((g.B, H, dh), F32), jax.ShapeDtypeStruct((g.B, 1, LANES), F32),
                   jax.ShapeDtypeStruct((g.B, SUBLANES, 2 * ML_W), F32)],
        scratch_shapes=[pltpu.VMEM((SUBLANES, 2 * ML_W), F32), pltpu.VMEM((R, ML_W), F32),
                        pltpu.VMEM((ML_W, R), BF16), pltpu.VMEM((R, ML_W), F32), pltpu.VMEM((R, ML_W), BF16),
                        pltpu.VMEM((R, LANES), F32), pltpu.VMEM((R, dh), F32), pltpu.VMEM((R, LANES), F32),
                        pltpu.VMEM((R, ML_W), F32)],
        compiler_params=_params(),
        name="mlstm",
    )(x, *consts, C0, n0, m0, conv8)


def _tok_spec(C):
    return pl.BlockSpec((ROW_TILE, C), lambda i: (i, 0))


def _tok_const(shape):
    nd = len(shape)
    return pl.BlockSpec(shape, lambda i, _nd=nd: (0,) * _nd)


def _call_merge(x, gm, wg, mb, brs, wbrs, wo):
    n = x.shape[0]
    consts_a = [gm, wg, mb]
    consts_b = list(wbrs) + [wo]
    return pl.pallas_call(
        _merge_kernel,
        grid=(n // ROW_TILE,),
        in_specs=[_tok_spec(D_MODEL)] + [_tok_const(c.shape) for c in consts_a]
        + [_tok_spec(b.shape[1]) for b in brs] + [_tok_const(c.shape) for c in consts_b],
        out_specs=_tok_spec(D_MODEL),
        out_shape=jax.ShapeDtypeStruct((n, D_MODEL), F32),
        compiler_params=pltpu.CompilerParams(dimension_semantics=("arbitrary",), vmem_limit_bytes=VMEM_LIMIT),
        name="merge",
    )(x, *consts_a, *brs, *consts_b)


def _call_mlp(x, gm, wu, wd, gf, final):
    n = x.shape[0]
    consts = [gm, wu, wd, gf]
    return pl.pallas_call(
        functools.partial(_mlp_kernel, final),
        grid=(n // ROW_TILE,),
        in_specs=[_tok_spec(D_MODEL)] + [_tok_const(c.shape) for c in consts],
        out_specs=_tok_spec(D_MODEL),
        out_shape=jax.ShapeDtypeStruct((n, D_MODEL), F32),
        compiler_params=pltpu.CompilerParams(dimension_semantics=("arbitrary",), vmem_limit_bytes=VMEM_LIMIT),
        name="mlp",
    )(x, *consts)


def _cols(w, *names):
    return [w[:, _OFF[n][0]:_OFF[n][1]] for n in names]


def _pad_cols(w, width):
    return jnp.pad(w, ((0, 0), (0, width - w.shape[1])))


def _block_diag(w):
    nb, d, e = w.shape
    return (jnp.eye(nb, dtype=w.dtype)[:, None, :, None] * w[:, :, None, :]).reshape(nb * d, nb * e)


def _row(v):
    return v.reshape(1, -1).astype(F32)


def _pad8(conv_state):
    return jnp.pad(conv_state, ((0, 0), (SUBLANES - (CONV_W - 1), 0), (0, 0)))


def _layer_weights(l, P, lbs):
    w = P["w_in"][l]
    W = {}
    W["gm"] = _row(P["norm_mix"][l])
    W["w_gla"] = jnp.concatenate(_cols(w, "g_q", "g_k", "g_v", "g_g") + [_pad_cols(_cols(w, "g_lr")[0], LANES)],
                                 axis=1).astype(BF16)
    W["wlr"] = jnp.pad(P["gla_w_lr"][l], ((0, LANES - GLA_RANK), (0, 0))).astype(BF16)
    W["blr"] = _row(P["gla_b_lr"][l])
    W["gla_norm"] = _row(P["gla_norm"][l])
    W["w_rg"] = jnp.concatenate(_cols(w, "r_x", "r_y"), axis=1).astype(BF16)
    W["rg_cw"] = P["rg_conv_w"][l].astype(F32)
    W["rg_cb"] = _row(P["rg_conv_b"][l])
    W["rg_wa"] = _block_diag(P["rg_wa"][l]).astype(BF16)
    W["rg_ba"] = _row(P["rg_ba"][l])
    W["rg_wx"] = _block_diag(P["rg_wx"][l]).astype(BF16)
    W["rg_bx"] = _row(P["rg_bx"][l])
    W["rg_lam"] = _row(P["rg_lam"][l])
    W["w_ml"] = jnp.concatenate(_cols(w, "m_q", "m_k", "m_v", "m_o")
                                + [_pad_cols(_cols(w, "m_i")[0], LANES), _pad_cols(_cols(w, "m_f")[0], LANES)],
                                axis=1).astype(BF16)
    W["ml_cw"] = P["ml_conv_w"][l].astype(F32)
    W["ml_cb"] = _row(P["ml_conv_b"][l])
    W["ml_bi"] = _pad_cols(_row(P["ml_b_i"][l]), LANES)
    W["ml_bf"] = _pad_cols(_row(P["ml_b_f"][l]), LANES)
    W["ml_norm"] = _row(P["ml_norm"][l])
    W["w_hg"] = jnp.concatenate(_cols(w, "h_q", "h_f", "h_i", "h_g"), axis=1).astype(BF16)
    W["hg_lb"] = _row(lbs[l])
    W["hg_norm"] = _row(P["hg_norm"][l])
    W["w_gates"] = _cols(w, "gates")[0].astype(BF16)
    W["merge_b"] = P["merge_b"][l].astype(F32)
    W["w_br"] = [P[n][l].astype(BF16) for n in ("w_br_gla", "w_br_rg", "w_br_ml", "w_br_hg")]
    W["w_out"] = P["w_out"][l].astype(BF16)
    W["gf"] = _row(P["norm_ffn"][l])
    W["w_up"] = P["w_up"][l].astype(BF16)
    W["w_down"] = P["w_down"][l].astype(BF16)
    return W


def _trunk(x3, states, LW, norm_final):
    B, T, _ = x3.shape
    g = _geom(B, T)
    x = x3.reshape(B * T, D_MODEL)
    s_gla, h_rg, conv_rg, c_ml, n_ml, m_ml, conv_ml, s_hg = states
    new = [[] for _ in range(8)]
    for l in range(DEPTH):
        W = LW[l]
        br_gla, s_gla_new = _call_gl(_gla_kernel, "gla", g, x,
                                     [W["gm"], W["w_gla"], W["wlr"], W["blr"], W["gla_norm"]],
                                     s_gla[l], GLA_HEADS, GLA_DK, GLA_DV)
        br_rg, h_new, conv_rg_new = _call_rglru(
            g, x, [W["gm"], W["w_rg"], W["rg_cw"], W["rg_cb"], W["rg_wa"], W["rg_ba"], W["rg_wx"], W["rg_bx"],
                   W["rg_lam"]],
            h_rg[l][:, None, :], _pad8(conv_rg[l]))
        br_ml, c_new, n_new, m_new, conv_ml_new = _call_mlstm(
            g, x, [W["gm"], W["w_ml"], W["ml_cw"], W["ml_cb"], W["ml_bi"], W["ml_bf"], W["ml_norm"]],
            c_ml[l], n_ml[l], _pad_cols(m_ml[l], LANES)[:, None, :], _pad8(conv_ml[l]))
        br_hg, s_hg_new = _call_gl(_hgrn_kernel, "hgrn", g, x,
                                   [W["gm"], W["w_hg"], W["hg_lb"], W["hg_norm"]],
                                   s_hg[l], HG_HEADS, HG_DK, HG_DV)
        x = _call_merge(x, W["gm"], W["w_gates"], W["merge_b"], [br_gla, br_rg, br_ml, br_hg], W["w_br"], W["w_out"])
        x = _call_mlp(x, W["gf"], W["w_up"], W["w_down"], _row(norm_final), final=(l == DEPTH - 1))
        outs = (s_gla_new, h_new[:, 0, :], conv_rg_new[:, SUBLANES - (CONV_W - 1):, :], c_new, n_new,
                m_new[:, 0, :ML_HEADS], conv_ml_new[:, SUBLANES - (CONV_W - 1):, :], s_hg_new)
        for j in range(8):
            new[j].append(outs[j])
    return x.reshape(B, T, D_MODEL), tuple(jnp.stack(n_) for n_ in new)


def _zero_states(B):
    return (jnp.zeros((DEPTH, B, GLA_HEADS, GLA_DK, GLA_DV), F32),
            jnp.zeros((DEPTH, B, RG_WIDTH), F32),
            jnp.zeros((DEPTH, B, CONV_W - 1, RG_WIDTH), F32),
            jnp.zeros((DEPTH, B, ML_HEADS, ML_DH, ML_DH), F32),
            jnp.zeros((DEPTH, B, ML_HEADS, ML_DH), F32),
            jnp.zeros((DEPTH, B, ML_HEADS), F32),
            jnp.zeros((DEPTH, B, CONV_W - 1, 2 * ML_W), F32),
            jnp.zeros((DEPTH, B, HG_HEADS, HG_DK, HG_DV), F32))


def kernel(x_prompt, x_sample, state_gla, state_rglru_h, state_rglru_conv, state_mlstm_C, state_mlstm_n, state_mlstm_m, state_mlstm_conv, state_hgrn, norm_mix, w_in, gla_w_lr, gla_b_lr, gla_norm, rg_conv_w, rg_conv_b, rg_wa, rg_ba, rg_wx, rg_bx, rg_lam, ml_conv_w, ml_conv_b, ml_b_i, ml_b_f, ml_norm, hg_gamma, hg_norm, merge_b, w_br_gla, w_br_rg, w_br_ml, w_br_hg, w_out, norm_ffn, w_up, w_down, norm_final):
    P = dict(norm_mix=norm_mix, w_in=w_in, gla_w_lr=gla_w_lr, gla_b_lr=gla_b_lr, gla_norm=gla_norm,
             rg_conv_w=rg_conv_w, rg_conv_b=rg_conv_b, rg_wa=rg_wa, rg_ba=rg_ba, rg_wx=rg_wx, rg_bx=rg_bx,
             rg_lam=rg_lam, ml_conv_w=ml_conv_w, ml_conv_b=ml_conv_b, ml_b_i=ml_b_i, ml_b_f=ml_b_f,
             ml_norm=ml_norm, hg_norm=hg_norm, merge_b=merge_b, w_br_gla=w_br_gla, w_br_rg=w_br_rg,
             w_br_ml=w_br_ml, w_br_hg=w_br_hg, w_out=w_out, norm_ffn=norm_ffn, w_up=w_up, w_down=w_down)
    sm = jax.nn.softmax(hg_gamma.astype(F32), axis=0)
    lbs = jnp.concatenate([jnp.zeros_like(sm[:1]), jnp.cumsum(sm, axis=0)[:-1]], axis=0)
    lbs = jnp.clip(lbs, 0.0, 1.0 - 1e-6)
    LW = [_layer_weights(l, P, lbs) for l in range(DEPTH)]

    y_prompt, p_st = _trunk(x_prompt, _zero_states(x_prompt.shape[0]), LW, norm_final)
    s_in = (state_gla, state_rglru_h, state_rglru_conv, state_mlstm_C, state_mlstm_n, state_mlstm_m,
            state_mlstm_conv, state_hgrn)
    y_sample, s_st = _trunk(x_sample, s_in, LW, norm_final)
    return (y_prompt, y_sample) + tuple(p_st) + tuple(s_st)
```

```python
import collections
import functools

import jax
import jax.numpy as jnp
from jax import lax
from jax.experimental import pallas as pl
from jax.experimental.pallas import tpu as pltpu

F32 = jnp.float32
BF16 = jnp.bfloat16

D_MODEL = 1024
DEPTH = 2
GLA_HEADS, GLA_DK, GLA_DV, GLA_RANK = 4, 64, 128, 16
GLA_KW, GLA_W = GLA_HEADS * GLA_DK, GLA_HEADS * GLA_DV
GLA_GATE_TAU = 16.0
RG_WIDTH, RG_BLOCKS, RG_C = 512, 8, 8.0
CONV_W = 4
ML_HEADS, ML_DH = 4, 128
ML_W = ML_HEADS * ML_DH
HG_HEADS, HG_DK, HG_DV = 4, 128, 128
HG_W = HG_HEADS * HG_DV
N_BRANCH = 4
D_FF = 4 * D_MODEL
EPS = 1e-6
NEG = -1e30

LANES = 128
SUBLANES = 8
MXU_DIM = 256
ROW_TILE = 256
BAND = SUBLANES
LOG2E = 1.4426950408889634
VMEM_LIMIT = 56 * 1024 * 1024

_OFF = {}
_o = 0
for _name, _size in (("g_q", GLA_KW), ("g_k", GLA_KW), ("g_v", GLA_W), ("g_lr", GLA_RANK), ("g_g", GLA_W),
                     ("r_x", RG_WIDTH), ("r_y", RG_WIDTH),
                     ("m_q", ML_W), ("m_k", ML_W), ("m_v", ML_W), ("m_o", ML_W), ("m_i", ML_HEADS), ("m_f", ML_HEADS),
                     ("h_q", HG_W), ("h_f", HG_W), ("h_i", HG_W), ("h_g", HG_W),
                     ("gates", N_BRANCH * D_MODEL)):
    _OFF[_name] = (_o, _o + _size)
    _o += _size

Geom = collections.namedtuple("Geom", "B T Bb Tt R SEG NSEG NT")


def _geom(B, T):
    if T >= ROW_TILE:
        Bb, Tt = 1, ROW_TILE
    else:
        Bb, Tt = ROW_TILE // T, T
    assert T % Tt == 0 and B % Bb == 0 and Tt % SUBLANES == 0 and Tt >= SUBLANES
    return Geom(B, T, Bb, Tt, Bb * Tt, Tt, Bb, T // Tt)


def _dot(a, b):
    return jnp.dot(a, b, preferred_element_type=F32)


def _dot_nt(a, b):
    return lax.dot_general(a, b, (((1,), (1,)), ((), ())), preferred_element_type=F32)


def _dot3(m01, x):
    hi = x.astype(BF16)
    r1 = x - hi.astype(F32)
    mid = r1.astype(BF16)
    lo = (r1 - mid.astype(F32)).astype(BF16)
    return _dot(m01, hi) + (_dot(m01, mid) + _dot(m01, lo))


def _iota(shape, dim):
    return lax.broadcasted_iota(jnp.int32, shape, dim)


def _div(x, n):
    assert n & (n - 1) == 0
    return x >> (n.bit_length() - 1)


def _mod(x, n):
    assert n & (n - 1) == 0
    return x & (n - 1)


def _log_sigmoid(x):
    return jnp.minimum(x, 0.0) - jnp.log(1.0 + jnp.exp(-jnp.abs(x)))


def _silu(x):
    return x * jax.nn.sigmoid(x)


def _gelu_tanh(x):
    return 0.5 * x * (1.0 + jnp.tanh(0.7978845608028654 * (x + 0.044715 * (x * x * x))))


def _rmsnorm(x, g):
    return x * lax.rsqrt(jnp.mean(x * x, axis=-1, keepdims=True) + EPS) * g


def _head_rmsnorm(o, gn, H, d):
    outs = []
    for h in range(H):
        oh = o[:, h * d:(h + 1) * d]
        outs.append(oh * lax.rsqrt(jnp.mean(oh * oh, axis=-1, keepdims=True) + EPS) * gn[:, h * d:(h + 1) * d])
    return jnp.concatenate(outs, axis=1)


def _block_last(c, w):
    R, C = c.shape
    if w == R:
        return jnp.broadcast_to(c[R - 1:R, :], (R, C))
    c3 = c.reshape(R // w, w, C)
    return jnp.broadcast_to(c3[:, w - 1:w, :], (R // w, w, C)).reshape(R, C)


def _seg_tri(g):
    row = _iota((g.R, g.R), 0)
    col = _iota((g.R, g.R), 1)
    keep = (col <= row) & (_div(row, g.SEG) == _div(col, g.SEG))
    return keep, keep.astype(BF16)


def _seg_rows(ref3, g):
    v = ref3[...]
    C = v.shape[-1]
    return jnp.broadcast_to(v, (g.NSEG, g.SEG, C)).reshape(g.R, C)


def _causal_conv(x, prev8, w_ref, b_ref, g):
    R, C = x.shape
    nb = R // SUBLANES
    sub = _iota((1, SUBLANES, 1), 1)
    x3 = x.reshape(nb, SUBLANES, C)
    p3 = prev8.reshape(nb, SUBLANES, C)
    y = b_ref[...] + x3 * w_ref[CONV_W - 1:CONV_W, :]
    for s in range(1, CONV_W):
        xs = jnp.where(sub >= s, pltpu.roll(x3, s, 1), pltpu.roll(p3, s, 1))
        y = y + xs * w_ref[CONV_W - 1 - s:CONV_W - s, :]
    return y.reshape(R, C)


def _conv_io(x, conv_in_ref, conv_out_ref, carry_ref, g):
    R = g.R
    ti = pl.program_id(1)
    if g.NSEG == 1:
        @pl.when(ti == 0)
        def _():
            carry_ref[...] = conv_in_ref[0]
        prev8 = jnp.concatenate([carry_ref[...], x[:R - SUBLANES, :]], axis=0)
        last8 = x[R - SUBLANES:, :]
        carry_ref[...] = last8
        conv_out_ref[0] = last8
    else:
        C = x.shape[-1]
        prev8 = conv_in_ref[...].reshape(R, C)
        conv_out_ref[...] = x.reshape(g.NSEG, g.SEG, C)
    return prev8


def _gl_core(q, k, v, la, s_ref, scr, g, H, dk, dv):
    qd_s, kdT_s, v_s, o_s, decT_s = scr
    R, SEG, NSEG = g.R, g.SEG, g.NSEG
    HK = H * dk
    _, tri = _seg_tri(g)
    c = _dot3(tri, la)
    cl = _block_last(c, SEG)

    bw = BAND
    HV = H * dv
    c2 = c * LOG2E
    hpg = MXU_DIM // dk
    e_r = _div(_iota((MXU_DIM, hpg * dv), 0), dk)
    e_c = _div(_iota((MXU_DIM, hpg * dv), 1), dv)
    ones_bd = (e_r == e_c).astype(BF16)
    nb = R // bw
    sub = _iota((1, bw, 1), 1)
    q3, k3, c3, v3 = (a.reshape(nb, bw, a.shape[1]) for a in (q, k, c2, v))
    o = None
    for d in range(bw):
        if d == 0:
            p3, vd = q3 * k3, v3
        else:
            diff = jnp.where(sub >= d, c3 - pltpu.roll(c3, d, 1), NEG)
            p3 = q3 * pltpu.roll(k3, d, 1) * jnp.exp2(diff)
            vd = pltpu.roll(v3, d, 1)
        pb = p3.reshape(R, HK).astype(BF16)
        parts = [_dot(pb[:, i * MXU_DIM:(i + 1) * MXU_DIM], ones_bd) for i in range(HK // MXU_DIM)]
        term = (parts[0] if len(parts) == 1 else jnp.concatenate(parts, axis=1)) * vd.reshape(R, HV)
        o = term if o is None else o + term

    vb = v.astype(BF16)
    w = bw
    a_off = [None] * H
    rowq = _iota((R, R), 0)
    colq = _iota((R, R), 1)
    while 2 * w <= SEG:
        nb2 = R // (2 * w)
        c4 = c2.reshape(nb2, 2, w, HK)
        pmid = c4[:, 0:1, w - 1:w, :]
        xk = k.reshape(nb2, 2, w, HK)[:, 0:1] * jnp.exp2(pmid - c4[:, 0:1])
        xq = q.reshape(nb2, 2, w, HK)[:, 1:2] * jnp.exp2(c4[:, 1:2] - pmid)
        x = jnp.concatenate([xk, xq], axis=1).reshape(R, HK).astype(BF16)
        rb = _div(rowq, w)
        m = (_mod(rb, 2) == 1) & (_div(colq, w) == rb - 1)
        for h in range(H):
            xh = x[:, h * dk:(h + 1) * dk]
            a = _dot_nt(xh, xh)
            a_off[h] = jnp.where(m, a, 0.0 if a_off[h] is None else a_off[h])
        w *= 2
    if a_off[0] is not None:
        o = o + jnp.concatenate([_dot(a_off[h].astype(BF16), vb[:, h * dv:(h + 1) * dv]) for h in range(H)], axis=1)

    qd_s[...] = q * jnp.exp(c)
    kdT_s[...] = (k * jnp.exp(cl - c)).T.astype(BF16)
    v_s[...] = vb
    decT_s[...] = jnp.exp(cl).T
    o_s[...] = o
    lane = _iota((1, R), 1)

    def seg_body(sg, carry):
        r0 = sg * SEG if isinstance(sg, int) else pl.multiple_of(sg * SEG, SEG)
        for h in range(H):
            S = s_ref[sg, h]
            qd = qd_s[pl.ds(r0, SEG), h * dk:(h + 1) * dk].astype(BF16)
            o_s[pl.ds(r0, SEG), h * dv:(h + 1) * dv] += _dot(qd, S.astype(BF16))
            kt = kdT_s[h * dk:(h + 1) * dk, :]
            if NSEG > 1:
                kt = jnp.where(_div(lane, SEG) == sg, kt, jnp.zeros_like(kt))
            U = _dot(kt, v_s[:, h * dv:(h + 1) * dv])
            dec = jnp.sum(jnp.where(lane == r0, decT_s[h * dk:(h + 1) * dk, :], 0.0), axis=1, keepdims=True)
            s_ref[sg, h] = dec * S + U
        return carry

    if NSEG == 1:
        seg_body(0, 0)
    else:
        lax.fori_loop(0, NSEG, seg_body, 0)
    return o_s[...]


def _gl_scratch(g, H, dk, dv):
    return [pltpu.VMEM((g.R, H * dk), F32), pltpu.VMEM((H * dk, g.R), BF16), pltpu.VMEM((g.R, H * dv), BF16),
            pltpu.VMEM((g.R, H * dv), F32), pltpu.VMEM((H * dk, g.R), F32)]


def _gla_branch(g, xn, r):
    z = _dot(xn, r.w_gla[...])
    q = z[:, 0:GLA_KW] * (GLA_DK ** -0.5)
    k = z[:, GLA_KW:2 * GLA_KW]
    v = z[:, 2 * GLA_KW:2 * GLA_KW + GLA_W]
    gate = z[:, 2 * GLA_KW + GLA_W:2 * GLA_KW + 2 * GLA_W]
    lr = z[:, 2 * GLA_KW + 2 * GLA_W:]
    la = _log_sigmoid(_dot(lr.astype(BF16), r.wlr[...]) + r.blr[...]) * (1.0 / GLA_GATE_TAU)
    o = _gl_core(q, k, v, la, r.s_gla_o, r.gla_scr, g, GLA_HEADS, GLA_DK, GLA_DV)
    return _head_rmsnorm(o, r.gla_norm[...], GLA_HEADS, GLA_DV) * _silu(gate)


def _hgrn_branch(g, xn, r):
    z = _dot(xn, r.w_hg[...])
    hq = z[:, 0:HG_W]
    fp = z[:, HG_W:2 * HG_W]
    v = z[:, 2 * HG_W:3 * HG_W]
    gate = z[:, 3 * HG_W:4 * HG_W]
    lb = r.hg_lb[...]
    ls = _log_sigmoid(fp)
    pos = lb > 0.0
    a_ = jnp.log(jnp.where(pos, lb, 1.0))
    b_ = jnp.log1p(-lb) + ls
    lae = jnp.maximum(a_, b_) + jnp.log(1.0 + jnp.exp(-jnp.abs(a_ - b_)))
    la = jnp.where(pos, lae, ls)
    k = (1.0 - lb) * jax.nn.sigmoid(-fp)
    q = _silu(hq)
    o = _gl_core(q, k, v, la, r.s_hg_o, r.hg_scr, g, HG_HEADS, HG_DK, HG_DV)
    return _head_rmsnorm(o, r.hg_norm[...], HG_HEADS, HG_DV) * _silu(gate)


def _rglru_branch(g, xn, r):
    R, SEG = g.R, g.SEG
    z = _dot(xn, r.w_rg[...])
    rx = z[:, :RG_WIDTH]
    ry = z[:, RG_WIDTH:]
    prev8 = _conv_io(rx, r.conv_rg, r.conv_rg_o, r.rg_carry, g)
    xc = _causal_conv(rx, prev8, r.rg_cw, r.rg_cb, g)
    xcb = xc.astype(BF16)
    rg = jax.nn.sigmoid(_dot(xcb, r.rg_wa[...]) + r.rg_ba[...])
    ig = jax.nn.sigmoid(_dot(xcb, r.rg_wx[...]) + r.rg_bx[...])
    log_a = RG_C * rg * _log_sigmoid(r.rg_lam[...])
    a = jnp.exp(log_a)
    u = jnp.sqrt(jnp.maximum(1.0 - a * a, 0.0)) * (ig * xc)
    nbs = SEG // SUBLANES
    sub = _iota((1, SUBLANES, 1), 1)
    a3 = a.reshape(R // SUBLANES, SUBLANES, RG_WIDTH)
    u3 = u.reshape(R // SUBLANES, SUBLANES, RG_WIDTH)
    d = 1
    while d < SUBLANES:
        ok = sub >= d
        a_s = jnp.where(ok, pltpu.roll(a3, d, 1), 1.0)
        u_s = jnp.where(ok, pltpu.roll(u3, d, 1), 0.0)
        u3 = a3 * u_s + u3
        a3 = a3 * a_s
        d *= 2
    a4 = a3.reshape(g.NSEG, nbs, SUBLANES, RG_WIDTH)
    u4 = u3.reshape(g.NSEG, nbs, SUBLANES, RG_WIDTH)
    hc = r.h_rg_o[...]
    blocks = []
    for j in range(nbs):
        hb = u4[:, j] + a4[:, j] * hc
        blocks.append(hb)
        hc = hb[:, SUBLANES - 1:SUBLANES, :]
    r.h_rg_o[...] = hc
    hcur = (blocks[0] if nbs == 1 else jnp.stack(blocks, axis=1)).reshape(R, RG_WIDTH)
    return hcur * _gelu_tanh(ry)


def _mlstm_branch(g, xn, r):
    R, SEG, NSEG = g.R, g.SEG, g.NSEG
    H, dh = ML_HEADS, ML_DH
    q_s, wkT_s, wk_s, v_s, sc_s, qc_s, qn_s, hm_s = r.ml_scr
    cout_ref, nout_ref, mout_ref = r.c_ml_o, r.n_ml_o, r.m_ml_o
    z = _dot(xn, r.w_ml[...])
    qk_pre = z[:, 0:2 * ML_W]
    mv = z[:, 2 * ML_W:3 * ML_W]
    mo = z[:, 3 * ML_W:4 * ML_W]
    ipre = z[:, 4 * ML_W:4 * ML_W + LANES] + r.ml_bi[...]
    fpre = z[:, 4 * ML_W + LANES:4 * ML_W + 2 * LANES] + r.ml_bf[...]
    prev8 = _conv_io(qk_pre, r.conv_ml, r.conv_ml_o, r.ml_carry, g)
    qk = _silu(_causal_conv(qk_pre, prev8, r.ml_cw, r.ml_cb, g))
    mq = qk[:, :ML_W]
    mk = qk[:, ML_W:] * (dh ** -0.5)

    keep, tri = _seg_tri(g)
    b = _dot3(tri, _log_sigmoid(fpre))
    m_rows = _seg_rows(mout_ref, g)
    prev = b + m_rows
    gT = (ipre - b).T
    bl = _block_last(b, SEG)
    wlog = bl - b + ipre
    wmax = jnp.broadcast_to(jnp.max(wlog.reshape(NSEG, SEG, LANES), axis=1, keepdims=True),
                            (NSEG, SEG, LANES)).reshape(R, LANES)
    m_new = jnp.maximum(bl + m_rows, wmax)
    wgt = jnp.exp(wlog - m_new)
    sc_s[...] = jnp.exp(bl + m_rows - m_new)
    mout_ref[...] = m_new.reshape(NSEG, SEG, LANES)[:, 0:1, :]

    q_s[...] = mq
    v_s[...] = mv.astype(BF16)
    wk = jnp.concatenate([wgt[:, h:h + 1] * mk[:, h * dh:(h + 1) * dh] for h in range(H)], axis=1)
    wk_s[...] = wk
    wkT_s[...] = wk.T.astype(BF16)
    lane = _iota((1, R), 1)

    for h in range(H):
        hs = slice(h * dh, (h + 1) * dh)
        logd = jnp.where(keep, b[:, h:h + 1] + gT[h:h + 1, :], NEG)
        mt = jnp.maximum(prev[:, h:h + 1], jnp.max(logd, axis=1, keepdims=True))
        dm = jnp.where(keep, jnp.exp(logd - mt), 0.0)
        sp = jnp.exp(prev[:, h:h + 1] - mt)
        s_mat = _dot_nt(mq[:, hs].astype(BF16), mk[:, hs].astype(BF16)) * dm
        num = _dot(s_mat.astype(BF16), v_s[:, hs])
        den = jnp.sum(s_mat, axis=1, keepdims=True)

        def seg_body(sg, carry, h=h, hs=hs):
            r0 = sg * SEG if isinstance(sg, int) else pl.multiple_of(sg * SEG, SEG)
            C = cout_ref[sg, h]
            n = nout_ref[sg, pl.ds(h, 1), :]
            qb = q_s[pl.ds(r0, SEG), hs]
            qc_s[pl.ds(r0, SEG), :] = _dot(qb.astype(BF16), C.astype(BF16))
            qn_s[pl.ds(r0, SEG), :] = jnp.broadcast_to(jnp.sum(qb * n, axis=1, keepdims=True), (SEG, LANES))
            kt = wkT_s[hs, :]
            if NSEG > 1:
                kt = jnp.where(_div(lane, SEG) == sg, kt, jnp.zeros_like(kt))
            U = _dot(kt, v_s[:, hs])
            sc = sc_s[pl.ds(r0, 1), h:h + 1]
            cout_ref[sg, h] = sc * C + U
            nout_ref[sg, pl.ds(h, 1), :] = sc * n + jnp.sum(wk_s[pl.ds(r0, SEG), hs], axis=0, keepdims=True)
            return carry

        if NSEG == 1:
            seg_body(0, 0)
        else:
            lax.fori_loop(0, NSEG, seg_body, 0)
        num = num + sp * qc_s[...]
        den = den + sp * qn_s[:, 0:1]
        hm_s[:, hs] = num / jnp.maximum(jnp.abs(den), jnp.exp(-mt))

    hm = jax.nn.sigmoid(mo) * hm_s[...]
    return _head_rmsnorm(hm, r.ml_norm[...], H, dh)


_BRANCHES = ("gla", "rg", "ml", "hg")
_BRANCH_FN = {"gla": _gla_branch, "rg": _rglru_branch, "ml": _mlstm_branch, "hg": _hgrn_branch}
_BRANCH_CONSTS = {
    "gla": ("w_gla", "wlr", "blr", "gla_norm"),
    "rg": ("w_rg", "rg_cw", "rg_cb", "rg_wa", "rg_ba", "rg_wx", "rg_bx", "rg_lam"),
    "ml": ("w_ml", "ml_cw", "ml_cb", "ml_bi", "ml_bf", "ml_norm"),
    "hg": ("w_hg", "hg_lb", "hg_norm"),
}
_MERGE_CONSTS = ("w_gates", "merge_b", "w_br0", "w_br1", "w_br2", "w_br3", "w_out")
_BRANCH_STATES = {"gla": ("s_gla",), "rg": ("h_rg", "conv_rg"), "ml": ("c_ml", "n_ml", "m_ml", "conv_ml"),
                  "hg": ("s_hg",)}
_CARRIED = ("s_gla", "h_rg", "c_ml", "n_ml", "m_ml", "s_hg")
_BRANCH_WIDTH = {"gla": GLA_W, "rg": RG_WIDTH, "ml": ML_W, "hg": HG_W}


def _state_tails():
    return {"s_gla": (GLA_HEADS, GLA_DK, GLA_DV), "h_rg": (1, RG_WIDTH), "conv_rg": (SUBLANES, RG_WIDTH),
            "c_ml": (ML_HEADS, ML_DH, ML_DH), "n_ml": (ML_HEADS, ML_DH), "m_ml": (1, LANES),
            "conv_ml": (SUBLANES, 2 * ML_W), "s_hg": (HG_HEADS, HG_DK, HG_DV)}


def _branch_scratch(g, b):
    R = g.R
    if b == "gla":
        return {"gla_scr": _gl_scratch(g, GLA_HEADS, GLA_DK, GLA_DV)}
    if b == "hg":
        return {"hg_scr": _gl_scratch(g, HG_HEADS, HG_DK, HG_DV)}
    if b == "rg":
        return {"rg_carry": [pltpu.VMEM((SUBLANES, RG_WIDTH), F32)]}
    return {"ml_carry": [pltpu.VMEM((SUBLANES, 2 * ML_W), F32)],
            "ml_scr": [pltpu.VMEM((R, ML_W), F32), pltpu.VMEM((ML_W, R), BF16), pltpu.VMEM((R, ML_W), F32),
                       pltpu.VMEM((R, ML_W), BF16), pltpu.VMEM((R, LANES), F32), pltpu.VMEM((R, ML_DH), F32),
                       pltpu.VMEM((R, LANES), F32), pltpu.VMEM((R, ML_W), F32)]}


def _mixer_plan(g, branches, merge):
    consts = ("gm",) + sum((_BRANCH_CONSTS[b] for b in branches), ()) + (_MERGE_CONSTS if merge else ())
    states = sum((_BRANCH_STATES[b] for b in branches), ())
    br_in = tuple("br_" + b for b in _BRANCHES if b not in branches) if merge else ()
    outs = (("y",) if merge else tuple("br_" + b + "_o" for b in branches)) + tuple(s + "_o" for s in states)
    scratch = {}
    for b in branches:
        scratch.update(_branch_scratch(g, b))
    return consts, states, br_in, outs, scratch


def _mixer_kernel(g, branches, merge, n_alias, *refs):
    consts, states, br_in, outs, scratch = _mixer_plan(g, branches, merge)
    names = ("x",) + consts + states + br_in
    r = dict(zip(names, refs[:len(names)]))
    pos = len(names) + n_alias
    r.update(zip(outs, refs[pos:pos + len(outs)]))
    pos += len(outs)
    for nm, shapes in scratch.items():
        grp = refs[pos:pos + len(shapes)]
        r[nm] = grp[0] if nm.endswith("carry") else grp
        pos += len(shapes)
    r = collections.namedtuple("Refs", r.keys())(**r)

    @pl.when(pl.program_id(1) == 0)
    def _():
        for nm in states:
            if nm in _CARRIED:
                getattr(r, nm + "_o")[...] = getattr(r, nm)[...]

    x = r.x[...]
    xn = _rmsnorm(x, r.gm[...]).astype(BF16)
    br = {b: _BRANCH_FN[b](g, xn, r).astype(BF16) for b in branches}
    if not merge:
        for b in branches:
            getattr(r, "br_" + b + "_o")[...] = br[b]
        return
    merged = None
    for j, b in enumerate(_BRANCHES):
        bv = br[b] if b in br else getattr(r, "br_" + b)[...]
        gt = jax.nn.sigmoid(_dot(xn, r.w_gates[:, j * D_MODEL:(j + 1) * D_MODEL]) + r.merge_b[j:j + 1, :])
        term = gt * _dot(bv, getattr(r, "w_br%d" % j)[...])
        merged = term if merged is None else merged + term
    r.y[...] = x + _dot(merged.astype(BF16), r.w_out[...])


def _mlp_kernel(final, x_ref, gm_ref, wu_ref, wd_ref, gf_ref, y_ref):
    x = x_ref[...]
    xn = _rmsnorm(x, gm_ref[...]).astype(BF16)
    hdn = jnp.square(jnp.maximum(_dot(xn, wu_ref[...]), 0.0))
    y = x + _dot(hdn.astype(BF16), wd_ref[...])
    if final:
        y = _rmsnorm(y, gf_ref[...])
    y_ref[...] = y


def _const(shape):
    nd = len(shape)
    return pl.BlockSpec(shape, lambda bi, ti, _nd=nd: (0,) * _nd)


def _rows(g, C):
    return pl.BlockSpec((g.R, C), lambda bi, ti, _nt=g.NT: (bi * _nt + ti, 0))


def _layer_state(g, l, tail):
    nd = len(tail)
    return pl.BlockSpec((None, g.Bb) + tuple(tail), lambda bi, ti, _l=l, _nd=nd: (_l, bi) + (0,) * _nd)


def _call_mixer(g, l, x, W, states, prev_outs, branches, merge, br_in=None):
    n = g.B * g.T
    tails = _state_tails()
    consts, snames, br_names, outs, scratch = _mixer_plan(g, branches, merge)
    ins = [x] + [W[c] for c in consts] + [states[s] for s in snames] + [br_in[b[3:]] for b in br_names]
    in_specs = ([_rows(g, D_MODEL)] + [_const(W[c].shape) for c in consts]
                + [_layer_state(g, l, tails[s]) for s in snames] + [_rows(g, _BRANCH_WIDTH[b[3:]]) for b in br_names])
    n_main = 1 if merge else len(branches)
    aliases = {}
    if prev_outs is not None:
        for j, s in enumerate(snames):
            aliases[len(ins)] = n_main + j
            ins.append(prev_outs[s])
            in_specs.append(pl.BlockSpec(memory_space=pl.ANY))
    if merge:
        main_specs = [_rows(g, D_MODEL)]
        main_shapes = [jax.ShapeDtypeStruct((n, D_MODEL), F32)]
    else:
        main_specs = [_rows(g, _BRANCH_WIDTH[b]) for b in branches]
        main_shapes = [jax.ShapeDtypeStruct((n, _BRANCH_WIDTH[b]), BF16) for b in branches]
    out = pl.pallas_call(
        functools.partial(_mixer_kernel, g, branches, merge, len(aliases)),
        grid=(g.B // g.Bb, g.NT),
        in_specs=in_specs,
        out_specs=main_specs + [_layer_state(g, l, tails[s]) for s in snames],
        out_shape=main_shapes + [jax.ShapeDtypeStruct((DEPTH, g.B) + tails[s], F32) for s in snames],
        scratch_shapes=sum(scratch.values(), []),
        input_output_aliases=aliases,
        compiler_params=pltpu.CompilerParams(dimension_semantics=("arbitrary", "arbitrary"),
                                             vmem_limit_bytes=VMEM_LIMIT),
        name="mixer_" + "_".join(branches + (("merge",) if merge else ())),
    )(*ins)
    main = out[0] if merge else dict(zip(branches, out[:n_main]))
    return main, dict(zip(snames, out[n_main:]))


def _call_mlp(x, gm, wu, wd, gf, final):
    n = x.shape[0]
    consts = [gm, wu, wd, gf]
    tok = pl.BlockSpec((ROW_TILE, D_MODEL), lambda i: (i, 0))
    return pl.pallas_call(
        functools.partial(_mlp_kernel, final),
        grid=(n // ROW_TILE,),
        in_specs=[tok] + [pl.BlockSpec(c.shape, lambda i: (0, 0)) for c in consts],
        out_specs=tok,
        out_shape=jax.ShapeDtypeStruct((n, D_MODEL), F32),
        compiler_params=pltpu.CompilerParams(dimension_semantics=("arbitrary",), vmem_limit_bytes=VMEM_LIMIT),
        name="mlp",
    )(x, *consts)


def _cols(w, *names):
    return [w[:, _OFF[n][0]:_OFF[n][1]] for n in names]


def _pad_cols(w, width):
    return jnp.pad(w, [(0, 0)] * (w.ndim - 1) + [(0, width - w.shape[-1])])


def _block_diag(w):
    nb, d, e = w.shape
    return (jnp.eye(nb, dtype=w.dtype)[:, None, :, None] * w[:, :, None, :]).reshape(nb * d, nb * e)


def _row(v):
    return v.reshape(1, -1).astype(F32)


def _pad8(conv_state):
    return jnp.pad(conv_state, ((0, 0), (0, 0), (SUBLANES - (CONV_W - 1), 0), (0, 0)))


def _layer_weights(l, P, lbs):
    w = P["w_in"][l]
    W = {}
    W["gm"] = _row(P["norm_mix"][l])
    W["w_gla"] = jnp.concatenate(_cols(w, "g_q", "g_k", "g_v", "g_g") + [_pad_cols(_cols(w, "g_lr")[0], LANES)],
                                 axis=1).astype(BF16)
    W["wlr"] = jnp.pad(P["gla_w_lr"][l], ((0, LANES - GLA_RANK), (0, 0))).astype(BF16)
    W["blr"] = _row(P["gla_b_lr"][l])
    W["gla_norm"] = _row(P["gla_norm"][l])
    W["w_rg"] = jnp.concatenate(_cols(w, "r_x", "r_y"), axis=1).astype(BF16)
    W["rg_cw"] = P["rg_conv_w"][l].astype(F32)
    W["rg_cb"] = _row(P["rg_conv_b"][l])
    W["rg_wa"] = _block_diag(P["rg_wa"][l]).astype(BF16)
    W["rg_ba"] = _row(P["rg_ba"][l])
    W["rg_wx"] = _block_diag(P["rg_wx"][l]).astype(BF16)
    W["rg_bx"] = _row(P["rg_bx"][l])
    W["rg_lam"] = _row(P["rg_lam"][l])
    W["w_ml"] = jnp.concatenate(_cols(w, "m_q", "m_k", "m_v", "m_o")
                                + [_pad_cols(_cols(w, "m_i")[0], LANES), _pad_cols(_cols(w, "m_f")[0], LANES)],
                                axis=1).astype(BF16)
    W["ml_cw"] = P["ml_conv_w"][l].astype(F32)
    W["ml_cb"] = _row(P["ml_conv_b"][l])
    W["ml_bi"] = _pad_cols(_row(P["ml_b_i"][l]), LANES)
    W["ml_bf"] = _pad_cols(_row(P["ml_b_f"][l]), LANES)
    W["ml_norm"] = _row(P["ml_norm"][l])
    W["w_hg"] = jnp.concatenate(_cols(w, "h_q", "h_f", "h_i", "h_g"), axis=1).astype(BF16)
    W["hg_lb"] = _row(lbs[l])
    W["hg_norm"] = _row(P["hg_norm"][l])
    W["w_gates"] = _cols(w, "gates")[0].astype(BF16)
    W["merge_b"] = P["merge_b"][l].astype(F32)
    for j, nm in enumerate(("w_br_gla", "w_br_rg", "w_br_ml", "w_br_hg")):
        W["w_br%d" % j] = P[nm][l].astype(BF16)
    W["w_out"] = P["w_out"][l].astype(BF16)
    W["gf"] = _row(P["norm_ffn"][l])
    W["w_up"] = P["w_up"][l].astype(BF16)
    W["w_down"] = P["w_down"][l].astype(BF16)
    return W


def _trunk(x3, st, LW, norm_final):
    B, T, _ = x3.shape
    g = _geom(B, T)
    x = x3.reshape(B * T, D_MODEL)
    s_gla, h_rg, conv_rg, c_ml, n_ml, m_ml, conv_ml, s_hg = st
    states = {"s_gla": s_gla, "h_rg": h_rg[:, :, None, :], "conv_rg": _pad8(conv_rg), "c_ml": c_ml, "n_ml": n_ml,
              "m_ml": _pad_cols(m_ml, LANES)[:, :, None, :], "conv_ml": _pad8(conv_ml), "s_hg": s_hg}
    outs = None
    for l in range(DEPTH):
        W = LW[l]
        if g.NSEG == 1:
            x, outs = _call_mixer(g, l, x, W, states, outs, _BRANCHES, True)
        else:
            new_outs, br = {}, {}
            for b in _BRANCHES:
                res, st_b = _call_mixer(g, l, x, W, states, outs, (b,), False)
                br.update(res)
                new_outs.update(st_b)
            x, _ = _call_mixer(g, l, x, W, states, None, (), True, br)
            outs = new_outs
        x = _call_mlp(x, W["gf"], W["w_up"], W["w_down"], _row(norm_final), final=(l == DEPTH - 1))
    tail3 = SUBLANES - (CONV_W - 1)
    new = (outs["s_gla"], outs["h_rg"][:, :, 0, :], outs["conv_rg"][:, :, tail3:, :], outs["c_ml"], outs["n_ml"],
           outs["m_ml"][:, :, 0, :ML_HEADS], outs["conv_ml"][:, :, tail3:, :], outs["s_hg"])
    return x.reshape(B, T, D_MODEL), new


def _zero_states(B):
    return (jnp.zeros((DEPTH, B, GLA_HEADS, GLA_DK, GLA_DV), F32),
            jnp.zeros((DEPTH, B, RG_WIDTH), F32),
            jnp.zeros((DEPTH, B, CONV_W - 1, RG_WIDTH), F32),
            jnp.zeros((DEPTH, B, ML_HEADS, ML_DH, ML_DH), F32),
            jnp.zeros((DEPTH, B, ML_HEADS, ML_DH), F32),
            jnp.zeros((DEPTH, B, ML_HEADS), F32),
            jnp.zeros((DEPTH, B, CONV_W - 1, 2 * ML_W), F32),
            jnp.zeros((DEPTH, B, HG_HEADS, HG_DK, HG_DV), F32))


def kernel(x_prompt, x_sample, state_gla, state_rglru_h, state_rglru_conv, state_mlstm_C, state_mlstm_n, state_mlstm_m, state_mlstm_conv, state_hgrn, norm_mix, w_in, gla_w_lr, gla_b_lr, gla_norm, rg_conv_w, rg_conv_b, rg_wa, rg_ba, rg_wx, rg_bx, rg_lam, ml_conv_w, ml_conv_b, ml_b_i, ml_b_f, ml_norm, hg_gamma, hg_norm, merge_b, w_br_gla, w_br_rg, w_br_ml, w_br_hg, w_out, norm_ffn, w_up, w_down, norm_final):
    P = dict(norm_mix=norm_mix, w_in=w_in, gla_w_lr=gla_w_lr, gla_b_lr=gla_b_lr, gla_norm=gla_norm,
             rg_conv_w=rg_conv_w, rg_conv_b=rg_conv_b, rg_wa=rg_wa, rg_ba=rg_ba, rg_wx=rg_wx, rg_bx=rg_bx,
             rg_lam=rg_lam, ml_conv_w=ml_conv_w, ml_conv_b=ml_conv_b, ml_b_i=ml_b_i, ml_b_f=ml_b_f,
             ml_norm=ml_norm, hg_norm=hg_norm, merge_b=merge_b, w_br_gla=w_br_gla, w_br_rg=w_br_rg,
             w_br_ml=w_br_ml, w_br_hg=w_br_hg, w_out=w_out, norm_ffn=norm_ffn, w_up=w_up, w_down=w_down)
    sm = jax.nn.softmax(hg_gamma.astype(F32), axis=0)
    lbs = jnp.concatenate([jnp.zeros_like(sm[:1]), jnp.cumsum(sm, axis=0)[:-1]], axis=0)
    lbs = jnp.clip(lbs, 0.0, 1.0 - 1e-6)
    LW = [_layer_weights(l, P, lbs) for l in range(DEPTH)]

    y_prompt, p_st = _trunk(x_prompt, _zero_states(x_prompt.shape[0]), LW, norm_final)
    s_in = (state_gla, state_rglru_h, state_rglru_conv, state_mlstm_C, state_mlstm_n, state_mlstm_m,
            state_mlstm_conv, state_hgrn)
    y_sample, s_st = _trunk(x_sample, s_in, LW, norm_final)
    return (y_prompt, y_sample) + tuple(p_st) + tuple(s_st)
```

```python
import collections
import functools

import jax
import jax.numpy as jnp
import numpy as np
from jax import lax
from jax.experimental import pallas as pl
from jax.experimental.pallas import tpu as pltpu

F32 = jnp.float32
BF16 = jnp.bfloat16

D_MODEL = 1024
DEPTH = 2
GLA_HEADS, GLA_DK, GLA_DV, GLA_RANK = 4, 64, 128, 16
GLA_KW, GLA_W = GLA_HEADS * GLA_DK, GLA_HEADS * GLA_DV
GLA_GATE_TAU = 16.0
RG_WIDTH, RG_BLOCKS, RG_C = 512, 8, 8.0
CONV_W = 4
ML_HEADS, ML_DH = 4, 128
ML_W = ML_HEADS * ML_DH
HG_HEADS, HG_DK, HG_DV = 4, 128, 128
HG_W = HG_HEADS * HG_DV
N_BRANCH = 4
D_FF = 4 * D_MODEL
EPS = 1e-6
NEG = -1e30

LANES = 128
SUBLANES = 8
MXU_DIM = 256
ROW_TILE = 256
BAND = SUBLANES
LOG2E = 1.4426950408889634
SEG_UNROLL = 4
VMEM_LIMIT = 56 * 1024 * 1024

_OFF = {}
_o = 0
for _name, _size in (("g_q", GLA_KW), ("g_k", GLA_KW), ("g_v", GLA_W), ("g_lr", GLA_RANK), ("g_g", GLA_W),
                     ("r_x", RG_WIDTH), ("r_y", RG_WIDTH),
                     ("m_q", ML_W), ("m_k", ML_W), ("m_v", ML_W), ("m_o", ML_W), ("m_i", ML_HEADS), ("m_f", ML_HEADS),
                     ("h_q", HG_W), ("h_f", HG_W), ("h_i", HG_W), ("h_g", HG_W),
                     ("gates", N_BRANCH * D_MODEL)):
    _OFF[_name] = (_o, _o + _size)
    _o += _size

Geom = collections.namedtuple("Geom", "B T Bb Tt R SEG NSEG NT")


def _geom(B, T):
    if T >= ROW_TILE:
        Bb, Tt = 1, ROW_TILE
    else:
        Bb, Tt = ROW_TILE // T, T
    assert T % Tt == 0 and B % Bb == 0 and Tt % SUBLANES == 0 and Tt >= SUBLANES
    return Geom(B, T, Bb, Tt, Bb * Tt, Tt, Bb, T // Tt)


def _dot(a, b):
    return jnp.dot(a, b, preferred_element_type=F32)


def _dot_nt(a, b):
    return lax.dot_general(a, b, (((1,), (1,)), ((), ())), preferred_element_type=F32)


def _dot3(m01, x):
    hi = x.astype(BF16)
    r1 = x - hi.astype(F32)
    mid = r1.astype(BF16)
    lo = (r1 - mid.astype(F32)).astype(BF16)
    return _dot(m01, hi) + (_dot(m01, mid) + _dot(m01, lo))


def _iota(shape, dim):
    return lax.broadcasted_iota(jnp.int32, shape, dim)


def _div(x, n):
    assert n & (n - 1) == 0
    return x >> (n.bit_length() - 1)


def _log_sigmoid(x):
    return jnp.minimum(x, 0.0) - jnp.log(1.0 + jnp.exp(-jnp.abs(x)))


def _silu(x):
    h = 0.5 * x
    return h + h * jnp.tanh(h)


def _gelu_tanh(x):
    return 0.5 * x * (1.0 + jnp.tanh(0.7978845608028654 * (x + 0.044715 * (x * x * x))))


def _rmsnorm(x, g):
    return x * lax.rsqrt(jnp.mean(x * x, axis=-1, keepdims=True) + EPS) * g


def _head_rmsnorm(o, gn, H, d):
    outs = []
    for h in range(H):
        oh = o[:, h * d:(h + 1) * d]
        outs.append(oh * lax.rsqrt(jnp.mean(oh * oh, axis=-1, keepdims=True) + EPS) * gn[:, h * d:(h + 1) * d])
    return jnp.concatenate(outs, axis=1)


def _block_last(c, w):
    R, C = c.shape
    if w == R:
        return jnp.broadcast_to(c[R - 1:R, :], (R, C))
    c3 = c.reshape(R // w, w, C)
    return jnp.broadcast_to(c3[:, w - 1:w, :], (R // w, w, C)).reshape(R, C)


def _seg_rows(ref3, g):
    v = ref3[...]
    C = v.shape[-1]
    return jnp.broadcast_to(v, (g.NSEG, g.SEG, C)).reshape(g.R, C)


def _causal_conv(x, prev8, w_ref, b_ref, g):
    R, C = x.shape
    nb = R // SUBLANES
    sub = _iota((1, SUBLANES, 1), 1)
    x3 = x.reshape(nb, SUBLANES, C)
    p3 = prev8.reshape(nb, SUBLANES, C)
    y = b_ref[...] + x3 * w_ref[CONV_W - 1:CONV_W, :]
    for s in range(1, CONV_W):
        xs = jnp.where(sub >= s, pltpu.roll(x3, s, 1), pltpu.roll(p3, s, 1))
        y = y + xs * w_ref[CONV_W - 1 - s:CONV_W - s, :]
    return y.reshape(R, C)


def _conv_io(x, conv_in_ref, conv_out_ref, carry_ref, g):
    R = g.R
    ti = pl.program_id(1)
    if g.NSEG == 1:
        @pl.when(ti == 0)
        def _():
            carry_ref[...] = conv_in_ref[0]
        prev8 = jnp.concatenate([carry_ref[...], x[:R - SUBLANES, :]], axis=0)
        last8 = x[R - SUBLANES:, :]
        carry_ref[...] = last8
        conv_out_ref[0] = last8
    else:
        C = x.shape[-1]
        prev8 = conv_in_ref[...].reshape(R, C)
        conv_out_ref[...] = x.reshape(g.NSEG, g.SEG, C)
    return prev8


def _gl_core(q, k, v, la, s_ref, scr, g, H, dk, dv, tri, lvl, ones_bd):
    qd_s, kdT_s, v_s, o_s, decT_s = scr
    R, SEG, NSEG = g.R, g.SEG, g.NSEG
    HK = H * dk
    c = _dot3(tri, la)
    cl = _block_last(c, SEG)

    bw = BAND
    HV = H * dv
    c2 = c * LOG2E
    nb = R // bw
    sub = _iota((1, bw, 1), 1)
    q3, k3, c3, v3 = (a.reshape(nb, bw, a.shape[1]) for a in (q, k, c2, v))
    o = None
    for d in range(bw):
        if d == 0:
            p3, vd = q3 * k3, v3
        else:
            diff = jnp.where(sub >= d, c3 - pltpu.roll(c3, d, 1), NEG)
            p3 = q3 * pltpu.roll(k3, d, 1) * jnp.exp2(diff)
            vd = pltpu.roll(v3, d, 1)
        pb = p3.reshape(R, HK).astype(BF16)
        parts = [_dot(pb[:, i * MXU_DIM:(i + 1) * MXU_DIM], ones_bd) for i in range(HK // MXU_DIM)]
        term = (parts[0] if len(parts) == 1 else jnp.concatenate(parts, axis=1)) * vd.reshape(R, HV)
        o = term if o is None else o + term

    vb = v.astype(BF16)
    w = bw
    a_off = [None] * H
    level = 0
    while 2 * w <= SEG:
        nb2 = R // (2 * w)
        c4 = c2.reshape(nb2, 2, w, HK)
        pmid = c4[:, 0:1, w - 1:w, :]
        xk = k.reshape(nb2, 2, w, HK)[:, 0:1] * jnp.exp2(pmid - c4[:, 0:1])
        xq = q.reshape(nb2, 2, w, HK)[:, 1:2] * jnp.exp2(c4[:, 1:2] - pmid)
        x = jnp.concatenate([xk, xq], axis=1).reshape(R, HK).astype(BF16)
        level += 1
        m = lvl == level
        for h in range(H):
            xh = x[:, h * dk:(h + 1) * dk]
            a = _dot_nt(xh, xh)
            a_off[h] = jnp.where(m, a, 0.0 if a_off[h] is None else a_off[h])
        w *= 2
    if a_off[0] is not None:
        o = o + jnp.concatenate([_dot(a_off[h].astype(BF16), vb[:, h * dv:(h + 1) * dv]) for h in range(H)], axis=1)

    qd_s[...] = q * jnp.exp(c)
    kdT_s[...] = (k * jnp.exp(cl - c)).T.astype(BF16)
    v_s[...] = vb
    decT_s[...] = jnp.exp(cl).T
    o_s[...] = o
    lane = _iota((1, R), 1)

    def seg_body(sg, carry):
        r0 = sg * SEG if isinstance(sg, int) else pl.multiple_of(sg * SEG, SEG)
        for h in range(H):
            S = s_ref[sg, h]
            qd = qd_s[pl.ds(r0, SEG), h * dk:(h + 1) * dk].astype(BF16)
            o_s[pl.ds(r0, SEG), h * dv:(h + 1) * dv] += _dot(qd, S.astype(BF16))
            kt = kdT_s[h * dk:(h + 1) * dk, :]
            if NSEG > 1:
                kt = jnp.where(_div(lane, SEG) == sg, kt, jnp.zeros_like(kt))
            U = _dot(kt, v_s[:, h * dv:(h + 1) * dv])
            dec = jnp.sum(jnp.where(lane == r0, decT_s[h * dk:(h + 1) * dk, :], 0.0), axis=1, keepdims=True)
            s_ref[sg, h] = dec * S + U
        return carry

    if NSEG == 1:
        seg_body(0, 0)
    else:
        lax.fori_loop(0, NSEG, seg_body, 0, unroll=SEG_UNROLL)
    return o_s[...]


def _gl_scratch(g, H, dk, dv):
    return [pltpu.VMEM((g.R, H * dk), F32), pltpu.VMEM((H * dk, g.R), BF16), pltpu.VMEM((g.R, H * dv), BF16),
            pltpu.VMEM((g.R, H * dv), F32), pltpu.VMEM((H * dk, g.R), F32)]


_PIECES = {"gla": (2 * GLA_KW, GLA_W, GLA_W, LANES), "rg": (RG_WIDTH, RG_WIDTH),
           "ml": (ML_W, ML_W, ML_W, ML_W, 2 * LANES), "hg": (HG_W, HG_W, HG_W, HG_W)}


def _gla_branch(g, z, r):
    qk, v, gate, lr = z
    q = qk[:, 0:GLA_KW] * (GLA_DK ** -0.5)
    k = qk[:, GLA_KW:]
    la = _log_sigmoid(_dot(lr.astype(BF16), r.wlr[...]) + r.blr[...]) * (1.0 / GLA_GATE_TAU)
    o = _gl_core(q, k, v, la, r.s_gla_o, r.gla_scr, g, GLA_HEADS, GLA_DK, GLA_DV, r.tri[...], r.lvl[...],
                 r.ones_gla[...])
    return _head_rmsnorm(o, r.gla_norm[...], GLA_HEADS, GLA_DV) * _silu(gate)


def _hgrn_branch(g, z, r):
    hq, fp, v, gate = z
    lb = r.hg_lb[...]
    ls = _log_sigmoid(fp)
    pos = lb > 0.0
    a_ = jnp.log(jnp.where(pos, lb, 1.0))
    b_ = jnp.log1p(-lb) + ls
    lae = jnp.maximum(a_, b_) + jnp.log(1.0 + jnp.exp(-jnp.abs(a_ - b_)))
    la = jnp.where(pos, lae, ls)
    k = (1.0 - lb) * jax.nn.sigmoid(-fp)
    q = _silu(hq)
    o = _gl_core(q, k, v, la, r.s_hg_o, r.hg_scr, g, HG_HEADS, HG_DK, HG_DV, r.tri[...], r.lvl[...],
                 r.ones_hg[...])
    return _head_rmsnorm(o, r.hg_norm[...], HG_HEADS, HG_DV) * _silu(gate)


def _rglru_branch(g, z, r):
    R, SEG = g.R, g.SEG
    rx, ry = z
    prev8 = _conv_io(rx, r.conv_rg, r.conv_rg_o, r.rg_carry, g)
    xc = _causal_conv(rx, prev8, r.rg_cw, r.rg_cb, g)
    xcb = xc.astype(BF16)
    rg = jax.nn.sigmoid(_dot(xcb, r.rg_wa[...]) + r.rg_ba[...])
    ig = jax.nn.sigmoid(_dot(xcb, r.rg_wx[...]) + r.rg_bx[...])
    log_a = RG_C * rg * _log_sigmoid(r.rg_lam[...])
    a = jnp.exp(log_a)
    u = jnp.sqrt(jnp.maximum(1.0 - a * a, 0.0)) * (ig * xc)
    nbs = SEG // SUBLANES
    sub = _iota((1, SUBLANES, 1), 1)
    a3 = a.reshape(R // SUBLANES, SUBLANES, RG_WIDTH)
    u3 = u.reshape(R // SUBLANES, SUBLANES, RG_WIDTH)
    d = 1
    while d < SUBLANES:
        ok = sub >= d
        a_s = jnp.where(ok, pltpu.roll(a3, d, 1), 1.0)
        u_s = jnp.where(ok, pltpu.roll(u3, d, 1), 0.0)
        u3 = a3 * u_s + u3
        a3 = a3 * a_s
        d *= 2
    a4 = a3.reshape(g.NSEG, nbs, SUBLANES, RG_WIDTH)
    u4 = u3.reshape(g.NSEG, nbs, SUBLANES, RG_WIDTH)
    hc = r.h_rg_o[...]
    blocks = []
    for j in range(nbs):
        hb = u4[:, j] + a4[:, j] * hc
        blocks.append(hb)
        hc = hb[:, SUBLANES - 1:SUBLANES, :]
    r.h_rg_o[...] = hc
    hcur = (blocks[0] if nbs == 1 else jnp.stack(blocks, axis=1)).reshape(R, RG_WIDTH)
    return hcur * _gelu_tanh(ry)


def _mlstm_branch(g, z, r):
    R, SEG, NSEG = g.R, g.SEG, g.NSEG
    H, dh = ML_HEADS, ML_DH
    q_s, wkT_s, wk_s, v_s, sc_s, qc_s, qn_s, hm_s = r.ml_scr
    cout_ref, nout_ref, mout_ref = r.c_ml_o, r.n_ml_o, r.m_ml_o
    zq, zk, mv, mo, zif = z
    qk_pre = jnp.concatenate([zq, zk], axis=1)
    ipre = zif[:, :LANES] + r.ml_bi[...]
    fpre = zif[:, LANES:] + r.ml_bf[...]
    prev8 = _conv_io(qk_pre, r.conv_ml, r.conv_ml_o, r.ml_carry, g)
    qk = _silu(_causal_conv(qk_pre, prev8, r.ml_cw, r.ml_cb, g))
    mq = qk[:, :ML_W]
    mk = qk[:, ML_W:] * (dh ** -0.5)

    keep = r.lvl[...] >= 0
    b = _dot3(r.tri[...], _log_sigmoid(fpre))
    m_rows = _seg_rows(mout_ref, g)
    prev = b + m_rows
    gT = (ipre - b).T
    bl = _block_last(b, SEG)
    wlog = bl - b + ipre
    wmax = jnp.broadcast_to(jnp.max(wlog.reshape(NSEG, SEG, LANES), axis=1, keepdims=True),
                            (NSEG, SEG, LANES)).reshape(R, LANES)
    m_new = jnp.maximum(bl + m_rows, wmax)
    wgt = jnp.exp(wlog - m_new)
    sc_s[...] = jnp.exp(bl + m_rows - m_new)
    mout_ref[...] = m_new.reshape(NSEG, SEG, LANES)[:, 0:1, :]

    q_s[...] = mq
    v_s[...] = mv.astype(BF16)
    wk = jnp.concatenate([wgt[:, h:h + 1] * mk[:, h * dh:(h + 1) * dh] for h in range(H)], axis=1)
    wk_s[...] = wk
    wkT_s[...] = wk.T.astype(BF16)
    lane = _iota((1, R), 1)

    for h in range(H):
        hs = slice(h * dh, (h + 1) * dh)
        ls = slice(h * LANES, (h + 1) * LANES)
        logd = jnp.where(keep, b[:, h:h + 1] + gT[h:h + 1, :], NEG)
        mt = jnp.maximum(prev[:, h:h + 1], jnp.max(logd, axis=1, keepdims=True))
        dm = jnp.where(keep, jnp.exp(logd - mt), 0.0)
        sp = jnp.exp(prev[:, h:h + 1] - mt)
        s_mat = _dot_nt(mq[:, hs].astype(BF16), mk[:, hs].astype(BF16)) * dm
        num = _dot(s_mat.astype(BF16), v_s[:, hs])
        den = jnp.sum(s_mat, axis=1, keepdims=True)

        def seg_body(sg, carry, h=h, hs=hs, ls=ls):
            r0 = sg * SEG if isinstance(sg, int) else pl.multiple_of(sg * SEG, SEG)
            C = cout_ref[sg, h]
            n = nout_ref[sg, pl.ds(h, 1), :]
            qb = q_s[pl.ds(r0, SEG), hs]
            qc_s[pl.ds(r0, SEG), hs] = _dot(qb.astype(BF16), C.astype(BF16))
            qn_s[pl.ds(r0, SEG), ls] = jnp.broadcast_to(jnp.sum(qb * n, axis=1, keepdims=True), (SEG, LANES))
            kt = wkT_s[hs, :]
            if NSEG > 1:
                kt = jnp.where(_div(lane, SEG) == sg, kt, jnp.zeros_like(kt))
            U = _dot(kt, v_s[:, hs])
            sc = sc_s[pl.ds(r0, 1), h:h + 1]
            cout_ref[sg, h] = sc * C + U
            nout_ref[sg, pl.ds(h, 1), :] = sc * n + jnp.sum(wk_s[pl.ds(r0, SEG), hs], axis=0, keepdims=True)
            return carry

        if NSEG == 1:
            seg_body(0, 0)
        else:
            lax.fori_loop(0, NSEG, seg_body, 0, unroll=SEG_UNROLL)
        num = num + sp * qc_s[:, hs]
        den = den + sp * qn_s[:, h * LANES:h * LANES + 1]
        hm_s[:, hs] = num / jnp.maximum(jnp.abs(den), jnp.exp(-mt))

    hm = jax.nn.sigmoid(mo) * hm_s[...]
    return _head_rmsnorm(hm, r.ml_norm[...], H, dh)


_BRANCHES = ("gla", "rg", "ml", "hg")
_BRANCH_FN = {"gla": _gla_branch, "rg": _rglru_branch, "ml": _mlstm_branch, "hg": _hgrn_branch}
_BRANCH_CONSTS = {
    "gla": ("w_gla", "wlr", "blr", "gla_norm", "ones_gla"),
    "rg": ("w_rg", "rg_cw", "rg_cb", "rg_wa", "rg_ba", "rg_wx", "rg_bx", "rg_lam"),
    "ml": ("w_ml", "ml_cw", "ml_cb", "ml_bi", "ml_bf", "ml_norm"),
    "hg": ("w_hg", "hg_lb", "hg_norm", "ones_hg"),
}
_MERGE_CONSTS = ("w_gates", "merge_b", "w_br0", "w_br1", "w_br2", "w_br3", "w_out")
_BRANCH_STATES = {"gla": ("s_gla",), "rg": ("h_rg", "conv_rg"), "ml": ("c_ml", "n_ml", "m_ml", "conv_ml"),
                  "hg": ("s_hg",)}
_CARRIED = ("s_gla", "h_rg", "c_ml", "n_ml", "m_ml", "s_hg")
_BRANCH_WIDTH = {"gla": GLA_W, "rg": RG_WIDTH, "ml": ML_W, "hg": HG_W}


def _state_tails():
    return {"s_gla": (GLA_HEADS, GLA_DK, GLA_DV), "h_rg": (1, RG_WIDTH), "conv_rg": (SUBLANES, RG_WIDTH),
            "c_ml": (ML_HEADS, ML_DH, ML_DH), "n_ml": (ML_HEADS, ML_DH), "m_ml": (1, LANES),
            "conv_ml": (SUBLANES, 2 * ML_W), "s_hg": (HG_HEADS, HG_DK, HG_DV)}


def _branch_scratch(g, b):
    R = g.R
    if b == "gla":
        return {"gla_scr": _gl_scratch(g, GLA_HEADS, GLA_DK, GLA_DV)}
    if b == "hg":
        return {"hg_scr": _gl_scratch(g, HG_HEADS, HG_DK, HG_DV)}
    if b == "rg":
        return {"rg_carry": [pltpu.VMEM((SUBLANES, RG_WIDTH), F32)]}
    return {"ml_carry": [pltpu.VMEM((SUBLANES, 2 * ML_W), F32)],
            "ml_scr": [pltpu.VMEM((R, ML_W), F32), pltpu.VMEM((ML_W, R), BF16), pltpu.VMEM((R, ML_W), F32),
                       pltpu.VMEM((R, ML_W), BF16), pltpu.VMEM((R, LANES), F32), pltpu.VMEM((R, ML_W), F32),
                       pltpu.VMEM((R, ML_HEADS * LANES), F32), pltpu.VMEM((R, ML_W), F32)]}


def _mixer_plan(g, branches, merge):
    consts = (("gm", "tri", "lvl") if branches else ("gm",)) + sum((_BRANCH_CONSTS[b] for b in branches), ())
    consts += _MERGE_CONSTS if merge else ()
    states = sum((_BRANCH_STATES[b] for b in branches), ())
    br_in = tuple("br_" + b for b in _BRANCHES if b not in branches) if merge else ()
    outs = (("y",) if merge else tuple("br_" + b + "_o" for b in branches)) + tuple(s + "_o" for s in states)
    scratch = {}
    for b in branches:
        scratch.update(_branch_scratch(g, b))
    return consts, states, br_in, outs, scratch


def _mixer_kernel(g, branches, merge, n_alias, *refs):
    consts, states, br_in, outs, scratch = _mixer_plan(g, branches, merge)
    names = ("x",) + consts + states + br_in
    r = dict(zip(names, refs[:len(names)]))
    pos = len(names) + n_alias
    r.update(zip(outs, refs[pos:pos + len(outs)]))
    pos += len(outs)
    for nm, shapes in scratch.items():
        grp = refs[pos:pos + len(shapes)]
        r[nm] = grp[0] if nm.endswith("carry") else grp
        pos += len(shapes)
    r = collections.namedtuple("Refs", r.keys())(**r)

    @pl.when(pl.program_id(1) == 0)
    def _():
        for nm in states:
            if nm in _CARRIED:
                getattr(r, nm + "_o")[...] = getattr(r, nm)[...]

    x = r.x[...]
    xn = _rmsnorm(x, r.gm[...]).astype(BF16)

    def inproj_jobs(b):
        w_ref = getattr(r, _BRANCH_CONSTS[b][0])
        z[b] = [None] * len(_PIECES[b])
        jobs, lo = [], 0
        for i, wd in enumerate(_PIECES[b]):
            def job(i=i, lo=lo, hi=lo + wd):
                z[b][i] = _dot(xn, w_ref[:, lo:hi])
            jobs.append(job)
            lo += wd
        return jobs

    def gate_jobs(j):
        half = D_MODEL // 2
        gates[j] = [None, None]
        jobs = []
        for i in range(2):
            def job(i=i, lo=j * D_MODEL + i * half):
                pre = _dot(xn, r.w_gates[:, lo:lo + half]) + r.merge_b[j:j + 1, i * half:(i + 1) * half]
                gates[j][i] = jax.nn.sigmoid(pre).astype(BF16)
            jobs.append(job)
        return jobs

    def proj_job(j, b):
        def job():
            bv = br[b] if b in br else getattr(r, "br_" + b)[...]
            gt = jnp.concatenate(gates[j], axis=1).astype(F32)
            merged.append(gt * _dot(bv, getattr(r, "w_br%d" % j)[...]))
        return job

    z, gates, br, merged = {}, {}, {}, []
    for b in branches:
        for jb in inproj_jobs(b):
            jb()
        br[b] = _BRANCH_FN[b](g, z[b], r).astype(BF16)
    if not merge:
        for b in branches:
            getattr(r, "br_" + b + "_o")[...] = br[b]
        return
    for j, b in enumerate(_BRANCHES):
        for jb in gate_jobs(j):
            jb()
        proj_job(j, b)()
    total = merged[0]
    for term in merged[1:]:
        total = total + term
    r.y[...] = x + _dot(total.astype(BF16), r.w_out[...])


def _mlp_kernel(final, x_ref, gm_ref, wu_ref, wd_ref, gf_ref, y_ref):
    x = x_ref[...]
    xn = _rmsnorm(x, gm_ref[...]).astype(BF16)
    hdn = jnp.square(jnp.maximum(_dot(xn, wu_ref[...]), 0.0))
    y = x + _dot(hdn.astype(BF16), wd_ref[...])
    if final:
        y = _rmsnorm(y, gf_ref[...])
    y_ref[...] = y


def _const(shape):
    nd = len(shape)
    return pl.BlockSpec(shape, lambda bi, ti, _nd=nd: (0,) * _nd)


def _rows(g, C):
    return pl.BlockSpec((g.R, C), lambda bi, ti, _nt=g.NT: (bi * _nt + ti, 0))


def _layer_state(g, l, tail):
    nd = len(tail)
    return pl.BlockSpec((None, g.Bb) + tuple(tail), lambda bi, ti, _l=l, _nd=nd: (_l, bi) + (0,) * _nd)


def _call_mixer(g, l, x, W, states, prev_outs, branches, merge, br_in=None):
    n = g.B * g.T
    tails = _state_tails()
    consts, snames, br_names, outs, scratch = _mixer_plan(g, branches, merge)
    ins = [x] + [W[c] for c in consts] + [states[s] for s in snames] + [br_in[b[3:]] for b in br_names]
    in_specs = ([_rows(g, D_MODEL)] + [_const(W[c].shape) for c in consts]
                + [_layer_state(g, l, tails[s]) for s in snames] + [_rows(g, _BRANCH_WIDTH[b[3:]]) for b in br_names])
    n_main = 1 if merge else len(branches)
    aliases = {}
    if prev_outs is not None:
        for j, s in enumerate(snames):
            aliases[len(ins)] = n_main + j
            ins.append(prev_outs[s])
            in_specs.append(pl.BlockSpec(memory_space=pl.ANY))
    if merge:
        main_specs = [_rows(g, D_MODEL)]
        main_shapes = [jax.ShapeDtypeStruct((n, D_MODEL), F32)]
    else:
        main_specs = [_rows(g, _BRANCH_WIDTH[b]) for b in branches]
        main_shapes = [jax.ShapeDtypeStruct((n, _BRANCH_WIDTH[b]), BF16) for b in branches]
    out = pl.pallas_call(
        functools.partial(_mixer_kernel, g, branches, merge, len(aliases)),
        grid=(g.B // g.Bb, g.NT),
        in_specs=in_specs,
        out_specs=main_specs + [_layer_state(g, l, tails[s]) for s in snames],
        out_shape=main_shapes + [jax.ShapeDtypeStruct((DEPTH, g.B) + tails[s], F32) for s in snames],
        scratch_shapes=sum(scratch.values(), []),
        input_output_aliases=aliases,
        compiler_params=pltpu.CompilerParams(dimension_semantics=("arbitrary", "arbitrary"),
                                             vmem_limit_bytes=VMEM_LIMIT),
        name="mixer_" + "_".join(branches + (("merge",) if merge else ())),
    )(*ins)
    main = out[0] if merge else dict(zip(branches, out[:n_main]))
    return main, dict(zip(snames, out[n_main:]))


def _call_mlp(x, gm, wu, wd, gf, final):
    n = x.shape[0]
    consts = [gm, wu, wd, gf]
    tok = pl.BlockSpec((ROW_TILE, D_MODEL), lambda i: (i, 0))
    return pl.pallas_call(
        functools.partial(_mlp_kernel, final),
        grid=(n // ROW_TILE,),
        in_specs=[tok] + [pl.BlockSpec(c.shape, lambda i: (0, 0)) for c in consts],
        out_specs=tok,
        out_shape=jax.ShapeDtypeStruct((n, D_MODEL), F32),
        compiler_params=pltpu.CompilerParams(dimension_semantics=("arbitrary",), vmem_limit_bytes=VMEM_LIMIT),
        name="mlp",
    )(x, *consts)


def _cols(w, *names):
    return [w[:, _OFF[n][0]:_OFF[n][1]] for n in names]


def _pad_cols(w, width):
    return jnp.pad(w, [(0, 0)] * (w.ndim - 1) + [(0, width - w.shape[-1])])


def _block_diag(w):
    nb, d, e = w.shape
    return (jnp.eye(nb, dtype=w.dtype)[:, None, :, None] * w[:, :, None, :]).reshape(nb * d, nb * e)


def _row(v):
    return v.reshape(1, -1).astype(F32)


def _pad8(conv_state):
    return jnp.pad(conv_state, ((0, 0), (0, 0), (SUBLANES - (CONV_W - 1), 0), (0, 0)))


def _layer_weights(l, P, lbs):
    w = P["w_in"][l]
    W = {}
    W["gm"] = _row(P["norm_mix"][l])
    W["w_gla"] = jnp.concatenate(_cols(w, "g_q", "g_k", "g_v", "g_g") + [_pad_cols(_cols(w, "g_lr")[0], LANES)],
                                 axis=1).astype(BF16)
    W["wlr"] = jnp.pad(P["gla_w_lr"][l], ((0, LANES - GLA_RANK), (0, 0))).astype(BF16)
    W["blr"] = _row(P["gla_b_lr"][l])
    W["gla_norm"] = _row(P["gla_norm"][l])
    W["w_rg"] = jnp.concatenate(_cols(w, "r_x", "r_y"), axis=1).astype(BF16)
    W["rg_cw"] = P["rg_conv_w"][l].astype(F32)
    W["rg_cb"] = _row(P["rg_conv_b"][l])
    W["rg_wa"] = _block_diag(P["rg_wa"][l]).astype(BF16)
    W["rg_ba"] = _row(P["rg_ba"][l])
    W["rg_wx"] = _block_diag(P["rg_wx"][l]).astype(BF16)
    W["rg_bx"] = _row(P["rg_bx"][l])
    W["rg_lam"] = _row(P["rg_lam"][l])
    W["w_ml"] = jnp.concatenate(_cols(w, "m_q", "m_k", "m_v", "m_o")
                                + [_pad_cols(_cols(w, "m_i")[0], LANES), _pad_cols(_cols(w, "m_f")[0], LANES)],
                                axis=1).astype(BF16)
    W["ml_cw"] = P["ml_conv_w"][l].astype(F32)
    W["ml_cb"] = _row(P["ml_conv_b"][l])
    W["ml_bi"] = _pad_cols(_row(P["ml_b_i"][l]), LANES)
    W["ml_bf"] = _pad_cols(_row(P["ml_b_f"][l]), LANES)
    W["ml_norm"] = _row(P["ml_norm"][l])
    W["w_hg"] = jnp.concatenate(_cols(w, "h_q", "h_f", "h_i", "h_g"), axis=1).astype(BF16)
    W["hg_lb"] = _row(lbs[l])
    W["hg_norm"] = _row(P["hg_norm"][l])
    W["w_gates"] = _cols(w, "gates")[0].astype(BF16)
    W["merge_b"] = P["merge_b"][l].astype(F32)
    for j, nm in enumerate(("w_br_gla", "w_br_rg", "w_br_ml", "w_br_hg")):
        W["w_br%d" % j] = P[nm][l].astype(BF16)
    W["w_out"] = P["w_out"][l].astype(BF16)
    W["gf"] = _row(P["norm_ffn"][l])
    W["w_up"] = P["w_up"][l].astype(BF16)
    W["w_down"] = P["w_down"][l].astype(BF16)
    return W


def _ones_block_diag(dk, dv):
    hpg = MXU_DIM // dk
    return (np.arange(MXU_DIM)[:, None] // dk == np.arange(hpg * dv)[None, :] // dv).astype(np.float32)


def _geom_consts(g):
    t = np.arange(g.R)[:, None]
    s = np.arange(g.R)[None, :]
    keep = (s <= t) & (t // g.SEG == s // g.SEG)
    lvl = np.full((g.R, g.R), -1, np.int32)
    lvl[keep & (t // BAND == s // BAND)] = 0
    w, i = BAND, 1
    while 2 * w <= g.SEG:
        lvl[keep & (t // (2 * w) == s // (2 * w)) & (t // w != s // w)] = i
        w, i = 2 * w, i + 1
    return {"tri": jnp.asarray(keep, BF16), "lvl": jnp.asarray(lvl),
            "ones_gla": jnp.asarray(_ones_block_diag(GLA_DK, GLA_DV), BF16),
            "ones_hg": jnp.asarray(_ones_block_diag(HG_DK, HG_DV), BF16)}


def _trunk(x3, st, LW, norm_final):
    B, T, _ = x3.shape
    g = _geom(B, T)
    x = x3.reshape(B * T, D_MODEL)
    s_gla, h_rg, conv_rg, c_ml, n_ml, m_ml, conv_ml, s_hg = st
    states = {"s_gla": s_gla, "h_rg": h_rg[:, :, None, :], "conv_rg": _pad8(conv_rg), "c_ml": c_ml, "n_ml": n_ml,
              "m_ml": _pad_cols(m_ml, LANES)[:, :, None, :], "conv_ml": _pad8(conv_ml), "s_hg": s_hg}
    outs = None
    gc = _geom_consts(g)
    for l in range(DEPTH):
        W = {**LW[l], **gc}
        if g.NSEG == 1:
            x, outs = _call_mixer(g, l, x, W, states, outs, _BRANCHES, True)
        else:
            new_outs, br = {}, {}
            for b in _BRANCHES:
                res, st_b = _call_mixer(g, l, x, W, states, outs, (b,), False)
                br.update(res)
                new_outs.update(st_b)
            x, _ = _call_mixer(g, l, x, W, states, None, (), True, br)
            outs = new_outs
        x = _call_mlp(x, W["gf"], W["w_up"], W["w_down"], _row(norm_final), final=(l == DEPTH - 1))
    tail3 = SUBLANES - (CONV_W - 1)
    new = (outs["s_gla"], outs["h_rg"][:, :, 0, :], outs["conv_rg"][:, :, tail3:, :], outs["c_ml"], outs["n_ml"],
           outs["m_ml"][:, :, 0, :ML_HEADS], outs["conv_ml"][:, :, tail3:, :], outs["s_hg"])
    return x.reshape(B, T, D_MODEL), new


def _zero_states(B):
    return (jnp.zeros((DEPTH, B, GLA_HEADS, GLA_DK, GLA_DV), F32),
            jnp.zeros((DEPTH, B, RG_WIDTH), F32),
            jnp.zeros((DEPTH, B, CONV_W - 1, RG_WIDTH), F32),
            jnp.zeros((DEPTH, B, ML_HEADS, ML_DH, ML_DH), F32),
            jnp.zeros((DEPTH, B, ML_HEADS, ML_DH), F32),
            jnp.zeros((DEPTH, B, ML_HEADS), F32),
            jnp.zeros((DEPTH, B, CONV_W - 1, 2 * ML_W), F32),
            jnp.zeros((DEPTH, B, HG_HEADS, HG_DK, HG_DV), F32))


def kernel(x_prompt, x_sample, state_gla, state_rglru_h, state_rglru_conv, state_mlstm_C, state_mlstm_n, state_mlstm_m, state_mlstm_conv, state_hgrn, norm_mix, w_in, gla_w_lr, gla_b_lr, gla_norm, rg_conv_w, rg_conv_b, rg_wa, rg_ba, rg_wx, rg_bx, rg_lam, ml_conv_w, ml_conv_b, ml_b_i, ml_b_f, ml_norm, hg_gamma, hg_norm, merge_b, w_br_gla, w_br_rg, w_br_ml, w_br_hg, w_out, norm_ffn, w_up, w_down, norm_final):
    P = dict(norm_mix=norm_mix, w_in=w_in, gla_w_lr=gla_w_lr, gla_b_lr=gla_b_lr, gla_norm=gla_norm,
             rg_conv_w=rg_conv_w, rg_conv_b=rg_conv_b, rg_wa=rg_wa, rg_ba=rg_ba, rg_wx=rg_wx, rg_bx=rg_bx,
             rg_lam=rg_lam, ml_conv_w=ml_conv_w, ml_conv_b=ml_conv_b, ml_b_i=ml_b_i, ml_b_f=ml_b_f,
             ml_norm=ml_norm, hg_norm=hg_norm, merge_b=merge_b, w_br_gla=w_br_gla, w_br_rg=w_br_rg,
             w_br_ml=w_br_ml, w_br_hg=w_br_hg, w_out=w_out, norm_ffn=norm_ffn, w_up=w_up, w_down=w_down)
    sm = jax.nn.softmax(hg_gamma.astype(F32), axis=0)
    lbs = jnp.concatenate([jnp.zeros_like(sm[:1]), jnp.cumsum(sm, axis=0)[:-1]], axis=0)
    lbs = jnp.clip(lbs, 0.0, 1.0 - 1e-6)
    LW = [_layer_weights(l, P, lbs) for l in range(DEPTH)]

    y_prompt, p_st = _trunk(x_prompt, _zero_states(x_prompt.shape[0]), LW, norm_final)
    s_in = (state_gla, state_rglru_h, state_rglru_conv, state_mlstm_C, state_mlstm_n, state_mlstm_m,
            state_mlstm_conv, state_hgrn)
    y_sample, s_st = _trunk(x_sample, s_in, LW, norm_final)
    return (y_prompt, y_sample) + tuple(p_st) + tuple(s_st)
```

```python
import collections
import functools

import jax
import jax.numpy as jnp
import numpy as np
from jax import lax
from jax.experimental import pallas as pl
from jax.experimental.pallas import tpu as pltpu

F32 = jnp.float32
BF16 = jnp.bfloat16

D_MODEL = 1024
DEPTH = 2
GLA_HEADS, GLA_DK, GLA_DV, GLA_RANK = 4, 64, 128, 16
GLA_KW, GLA_W = GLA_HEADS * GLA_DK, GLA_HEADS * GLA_DV
GLA_GATE_TAU = 16.0
RG_WIDTH, RG_BLOCKS, RG_C = 512, 8, 8.0
CONV_W = 4
ML_HEADS, ML_DH = 4, 128
ML_W = ML_HEADS * ML_DH
HG_HEADS, HG_DK, HG_DV = 4, 128, 128
HG_W = HG_HEADS * HG_DV
N_BRANCH = 4
D_FF = 4 * D_MODEL
EPS = 1e-6
NEG = -1e30

LANES = 128
SUBLANES = 8
MXU_DIM = 256
ROW_TILE = 256
BAND = 4
LOG2E = 1.4426950408889634
SEG_UNROLL = 4
VMEM_LIMIT = 56 * 1024 * 1024

_OFF = {}
_o = 0
for _name, _size in (("g_q", GLA_KW), ("g_k", GLA_KW), ("g_v", GLA_W), ("g_lr", GLA_RANK), ("g_g", GLA_W),
                     ("r_x", RG_WIDTH), ("r_y", RG_WIDTH),
                     ("m_q", ML_W), ("m_k", ML_W), ("m_v", ML_W), ("m_o", ML_W), ("m_i", ML_HEADS), ("m_f", ML_HEADS),
                     ("h_q", HG_W), ("h_f", HG_W), ("h_i", HG_W), ("h_g", HG_W),
                     ("gates", N_BRANCH * D_MODEL)):
    _OFF[_name] = (_o, _o + _size)
    _o += _size

Geom = collections.namedtuple("Geom", "B T Bb Tt R SEG NSEG NT")


def _geom(B, T):
    if T >= ROW_TILE:
        Bb, Tt = 1, ROW_TILE
    else:
        Bb, Tt = ROW_TILE // T, T
    assert T % Tt == 0 and B % Bb == 0 and Tt % SUBLANES == 0 and Tt >= SUBLANES
    return Geom(B, T, Bb, Tt, Bb * Tt, Tt, Bb, T // Tt)


def _dot(a, b):
    return jnp.dot(a, b, preferred_element_type=F32)


def _dot_nt(a, b):
    return lax.dot_general(a, b, (((1,), (1,)), ((), ())), preferred_element_type=F32)


def _dot3(m01, x):
    hi = x.astype(BF16)
    r1 = x - hi.astype(F32)
    mid = r1.astype(BF16)
    lo = (r1 - mid.astype(F32)).astype(BF16)
    return _dot(m01, hi) + (_dot(m01, mid) + _dot(m01, lo))


def _iota(shape, dim):
    return lax.broadcasted_iota(jnp.int32, shape, dim)


def _div(x, n):
    assert n & (n - 1) == 0
    return x >> (n.bit_length() - 1)


def _log_sigmoid(x):
    return jnp.minimum(x, 0.0) - jnp.log(1.0 + jnp.exp(-jnp.abs(x)))


def _silu(x):
    h = 0.5 * x
    return h + h * jnp.tanh(h)


def _gelu_tanh(x):
    return 0.5 * x * (1.0 + jnp.tanh(0.7978845608028654 * (x + 0.044715 * (x * x * x))))


def _rmsnorm(x, g):
    return x * lax.rsqrt(jnp.mean(x * x, axis=-1, keepdims=True) + EPS) * g


def _head_rmsnorm(o, gn, H, d):
    outs = []
    for h in range(H):
        oh = o[:, h * d:(h + 1) * d]
        outs.append(oh * lax.rsqrt(jnp.mean(oh * oh, axis=-1, keepdims=True) + EPS) * gn[:, h * d:(h + 1) * d])
    return jnp.concatenate(outs, axis=1)


def _block_last(c, w):
    R, C = c.shape
    if w == R:
        return jnp.broadcast_to(c[R - 1:R, :], (R, C))
    c3 = c.reshape(R // w, w, C)
    return jnp.broadcast_to(c3[:, w - 1:w, :], (R // w, w, C)).reshape(R, C)


def _seg_rows(ref3, g):
    v = ref3[...]
    C = v.shape[-1]
    return jnp.broadcast_to(v, (g.NSEG, g.SEG, C)).reshape(g.R, C)


def _causal_conv(x, prev8, w_ref, b_ref, g):
    R, C = x.shape
    nb = R // SUBLANES
    sub = _iota((1, SUBLANES, 1), 1)
    x3 = x.reshape(nb, SUBLANES, C)
    p3 = prev8.reshape(nb, SUBLANES, C)
    y = b_ref[...] + x3 * w_ref[CONV_W - 1:CONV_W, :]
    for s in range(1, CONV_W):
        xs = jnp.where(sub >= s, pltpu.roll(x3, s, 1), pltpu.roll(p3, s, 1))
        y = y + xs * w_ref[CONV_W - 1 - s:CONV_W - s, :]
    return y.reshape(R, C)


def _conv_io(x, conv_in_ref, conv_out_ref, carry_ref, g):
    R = g.R
    ti = pl.program_id(1)
    if g.NSEG == 1:
        @pl.when(ti == 0)
        def _():
            carry_ref[...] = conv_in_ref[0]
        prev8 = jnp.concatenate([carry_ref[...], x[:R - SUBLANES, :]], axis=0)
        last8 = x[R - SUBLANES:, :]
        carry_ref[...] = last8
        conv_out_ref[0] = last8
    else:
        C = x.shape[-1]
        prev8 = conv_in_ref[...].reshape(R, C)
        conv_out_ref[...] = x.reshape(g.NSEG, g.SEG, C)
    return prev8


def _gl_core(q, k, v, la, s_ref, scr, g, H, dk, dv, tri, lvl, ones_bd):
    qd_s, kdT_s, v_s, o_s, decT_s = scr
    R, SEG, NSEG = g.R, g.SEG, g.NSEG
    HK = H * dk
    c = _dot3(tri, la)
    cl = _block_last(c, SEG)

    bw = BAND
    HV = H * dv
    c2 = c * LOG2E
    nb = R // SUBLANES
    sub = _iota((1, SUBLANES, 1), 1)
    sub_b = sub & (bw - 1)
    q3, k3, c3, v3 = (a.reshape(nb, SUBLANES, a.shape[1]) for a in (q, k, c2, v))
    o = None
    for d in range(bw):
        if d == 0:
            p3, vd = q3 * k3, v3
        else:
            diff = jnp.where(sub_b >= d, c3 - pltpu.roll(c3, d, 1), NEG)
            p3 = q3 * pltpu.roll(k3, d, 1) * jnp.exp2(diff)
            vd = pltpu.roll(v3, d, 1)
        pb = p3.reshape(R, HK).astype(BF16)
        parts = [_dot(pb[:, i * MXU_DIM:(i + 1) * MXU_DIM], ones_bd) for i in range(HK // MXU_DIM)]
        term = (parts[0] if len(parts) == 1 else jnp.concatenate(parts, axis=1)) * vd.reshape(R, HV)
        o = term if o is None else o + term

    vb = v.astype(BF16)
    w = bw
    a_off = [None] * H
    level = 0
    while 2 * w <= SEG:
        if w < SUBLANES:
            assert 2 * w == SUBLANES
            first = sub < w
            pmid = c3[:, w - 1:w, :]
            x = jnp.where(first, k3, q3) * jnp.exp2(jnp.where(first, pmid - c3, c3 - pmid))
            x = x.reshape(R, HK).astype(BF16)
        else:
            nb2 = R // (2 * w)
            c4 = c2.reshape(nb2, 2, w, HK)
            pmid = c4[:, 0:1, w - 1:w, :]
            xk = k.reshape(nb2, 2, w, HK)[:, 0:1] * jnp.exp2(pmid - c4[:, 0:1])
            xq = q.reshape(nb2, 2, w, HK)[:, 1:2] * jnp.exp2(c4[:, 1:2] - pmid)
            x = jnp.concatenate([xk, xq], axis=1).reshape(R, HK).astype(BF16)
        level += 1
        m = lvl == level
        for h in range(H):
            xh = x[:, h * dk:(h + 1) * dk]
            a = _dot_nt(xh, xh)
            a_off[h] = jnp.where(m, a, 0.0 if a_off[h] is None else a_off[h])
        w *= 2
    if a_off[0] is not None:
        o = o + jnp.concatenate([_dot(a_off[h].astype(BF16), vb[:, h * dv:(h + 1) * dv]) for h in range(H)], axis=1)

    qd_s[...] = q * jnp.exp(c)
    kdT_s[...] = (k * jnp.exp(cl - c)).T.astype(BF16)
    v_s[...] = vb
    decT_s[...] = jnp.exp(cl).T
    o_s[...] = o
    lane = _iota((1, R), 1)

    def seg_body(sg, carry):
        r0 = sg * SEG if isinstance(sg, int) else pl.multiple_of(sg * SEG, SEG)
        for h in range(H):
            S = s_ref[sg, h]
            qd = qd_s[pl.ds(r0, SEG), h * dk:(h + 1) * dk].astype(BF16)
            o_s[pl.ds(r0, SEG), h * dv:(h + 1) * dv] += _dot(qd, S.astype(BF16))
            kt = kdT_s[h * dk:(h + 1) * dk, :]
            if NSEG > 1:
                kt = jnp.where(_div(lane, SEG) == sg, kt, jnp.zeros_like(kt))
            U = _dot(kt, v_s[:, h * dv:(h + 1) * dv])
            dec = jnp.sum(jnp.where(lane == r0, decT_s[h * dk:(h + 1) * dk, :], 0.0), axis=1, keepdims=True)
            s_ref[sg, h] = dec * S + U
        return carry

    if NSEG == 1:
        seg_body(0, 0)
    else:
        lax.fori_loop(0, NSEG, seg_body, 0, unroll=SEG_UNROLL)
    return o_s[...]


def _gl_scratch(g, H, dk, dv):
    return [pltpu.VMEM((g.R, H * dk), F32), pltpu.VMEM((H * dk, g.R), BF16), pltpu.VMEM((g.R, H * dv), BF16),
            pltpu.VMEM((g.R, H * dv), F32), pltpu.VMEM((H * dk, g.R), F32)]


_PIECES = {"gla": (2 * GLA_KW, GLA_W, GLA_W, LANES), "rg": (RG_WIDTH, RG_WIDTH),
           "ml": (ML_W, ML_W, ML_W, ML_W, 2 * LANES), "hg": (HG_W, HG_W, HG_W, HG_W)}


def _gla_branch(g, z, r):
    qk, v, gate, lr = z
    q = qk[:, 0:GLA_KW] * (GLA_DK ** -0.5)
    k = qk[:, GLA_KW:]
    la = _log_sigmoid(_dot(lr.astype(BF16), r.wlr[...]) + r.blr[...]) * (1.0 / GLA_GATE_TAU)
    o = _gl_core(q, k, v, la, r.s_gla_o, r.gla_scr, g, GLA_HEADS, GLA_DK, GLA_DV, r.tri[...], r.lvl[...],
                 r.ones_gla[...])
    return _head_rmsnorm(o, r.gla_norm[...], GLA_HEADS, GLA_DV) * _silu(gate)


def _hgrn_branch(g, z, r):
    hq, fp, v, gate = z
    lb = r.hg_lb[...]
    ls = _log_sigmoid(fp)
    pos = lb > 0.0
    a_ = jnp.log(jnp.where(pos, lb, 1.0))
    b_ = jnp.log1p(-lb) + ls
    lae = jnp.maximum(a_, b_) + jnp.log(1.0 + jnp.exp(-jnp.abs(a_ - b_)))
    la = jnp.where(pos, lae, ls)
    k = (1.0 - lb) * jax.nn.sigmoid(-fp)
    q = _silu(hq)
    o = _gl_core(q, k, v, la, r.s_hg_o, r.hg_scr, g, HG_HEADS, HG_DK, HG_DV, r.tri[...], r.lvl[...],
                 r.ones_hg[...])
    return _head_rmsnorm(o, r.hg_norm[...], HG_HEADS, HG_DV) * _silu(gate)


def _rglru_branch(g, z, r):
    R, SEG = g.R, g.SEG
    rx, ry = z
    prev8 = _conv_io(rx, r.conv_rg, r.conv_rg_o, r.rg_carry, g)
    xc = _causal_conv(rx, prev8, r.rg_cw, r.rg_cb, g)
    xcb = xc.astype(BF16)
    rg = jax.nn.sigmoid(_dot(xcb, r.rg_wa[...]) + r.rg_ba[...])
    ig = jax.nn.sigmoid(_dot(xcb, r.rg_wx[...]) + r.rg_bx[...])
    log_a = RG_C * rg * _log_sigmoid(r.rg_lam[...])
    a = jnp.exp(log_a)
    u = jnp.sqrt(jnp.maximum(1.0 - a * a, 0.0)) * (ig * xc)
    nbs = SEG // SUBLANES
    sub = _iota((1, SUBLANES, 1), 1)
    a3 = a.reshape(R // SUBLANES, SUBLANES, RG_WIDTH)
    u3 = u.reshape(R // SUBLANES, SUBLANES, RG_WIDTH)
    d = 1
    while d < SUBLANES:
        ok = sub >= d
        a_s = jnp.where(ok, pltpu.roll(a3, d, 1), 1.0)
        u_s = jnp.where(ok, pltpu.roll(u3, d, 1), 0.0)
        u3 = a3 * u_s + u3
        a3 = a3 * a_s
        d *= 2
    a4 = a3.reshape(g.NSEG, nbs, SUBLANES, RG_WIDTH)
    u4 = u3.reshape(g.NSEG, nbs, SUBLANES, RG_WIDTH)
    hc = r.h_rg_o[...]
    blocks = []
    for j in range(nbs):
        hb = u4[:, j] + a4[:, j] * hc
        blocks.append(hb)
        hc = hb[:, SUBLANES - 1:SUBLANES, :]
    r.h_rg_o[...] = hc
    hcur = (blocks[0] if nbs == 1 else jnp.stack(blocks, axis=1)).reshape(R, RG_WIDTH)
    return hcur * _gelu_tanh(ry)


def _mlstm_branch(g, z, r):
    R, SEG, NSEG = g.R, g.SEG, g.NSEG
    H, dh = ML_HEADS, ML_DH
    q_s, wkT_s, wk_s, v_s, sc_s, qc_s, qn_s, hm_s = r.ml_scr
    cout_ref, nout_ref, mout_ref = r.c_ml_o, r.n_ml_o, r.m_ml_o
    zq, zk, mv, mo, zif = z
    qk_pre = jnp.concatenate([zq, zk], axis=1)
    ipre = zif[:, :LANES] + r.ml_bi[...]
    fpre = zif[:, LANES:] + r.ml_bf[...]
    prev8 = _conv_io(qk_pre, r.conv_ml, r.conv_ml_o, r.ml_carry, g)
    qk = _silu(_causal_conv(qk_pre, prev8, r.ml_cw, r.ml_cb, g))
    mq = qk[:, :ML_W]
    mk = qk[:, ML_W:] * (dh ** -0.5)

    keep = r.lvl[...] >= 0
    b = _dot3(r.tri[...], _log_sigmoid(fpre))
    m_rows = _seg_rows(mout_ref, g)
    prev = b + m_rows
    gT = (ipre - b).T
    bl = _block_last(b, SEG)
    wlog = bl - b + ipre
    wmax = jnp.broadcast_to(jnp.max(wlog.reshape(NSEG, SEG, LANES), axis=1, keepdims=True),
                            (NSEG, SEG, LANES)).reshape(R, LANES)
    m_new = jnp.maximum(bl + m_rows, wmax)
    wgt = jnp.exp(wlog - m_new)
    sc_s[...] = jnp.exp(bl + m_rows - m_new)
    mout_ref[...] = m_new.reshape(NSEG, SEG, LANES)[:, 0:1, :]

    q_s[...] = mq
    v_s[...] = mv.astype(BF16)
    wk = jnp.concatenate([wgt[:, h:h + 1] * mk[:, h * dh:(h + 1) * dh] for h in range(H)], axis=1)
    wk_s[...] = wk
    wkT_s[...] = wk.T.astype(BF16)
    lane = _iota((1, R), 1)

    for h in range(H):
        hs = slice(h * dh, (h + 1) * dh)
        ls = slice(h * LANES, (h + 1) * LANES)
        logd = jnp.where(keep, b[:, h:h + 1] + gT[h:h + 1, :], NEG)
        mt = jnp.maximum(prev[:, h:h + 1], jnp.max(logd, axis=1, keepdims=True))
        dm = jnp.where(keep, jnp.exp(logd - mt), 0.0)
        sp = jnp.exp(prev[:, h:h + 1] - mt)
        s_mat = _dot_nt(mq[:, hs].astype(BF16), mk[:, hs].astype(BF16)) * dm
        num = _dot(s_mat.astype(BF16), v_s[:, hs])
        den = jnp.sum(s_mat, axis=1, keepdims=True)

        def seg_body(sg, carry, h=h, hs=hs, ls=ls):
            r0 = sg * SEG if isinstance(sg, int) else pl.multiple_of(sg * SEG, SEG)
            C = cout_ref[sg, h]
            n = nout_ref[sg, pl.ds(h, 1), :]
            qb = q_s[pl.ds(r0, SEG), hs]
            qc_s[pl.ds(r0, SEG), hs] = _dot(qb.astype(BF16), C.astype(BF16))
            qn_s[pl.ds(r0, SEG), ls] = jnp.broadcast_to(jnp.sum(qb * n, axis=1, keepdims=True), (SEG, LANES))
            kt = wkT_s[hs, :]
            if NSEG > 1:
                kt = jnp.where(_div(lane, SEG) == sg, kt, jnp.zeros_like(kt))
            U = _dot(kt, v_s[:, hs])
            sc = sc_s[pl.ds(r0, 1), h:h + 1]
            cout_ref[sg, h] = sc * C + U
            nout_ref[sg, pl.ds(h, 1), :] = sc * n + jnp.sum(wk_s[pl.ds(r0, SEG), hs], axis=0, keepdims=True)
            return carry

        if NSEG == 1:
            seg_body(0, 0)
        else:
            lax.fori_loop(0, NSEG, seg_body, 0, unroll=SEG_UNROLL)
        num = num + sp * qc_s[:, hs]
        den = den + sp * qn_s[:, h * LANES:h * LANES + 1]
        hm_s[:, hs] = num / jnp.maximum(jnp.abs(den), jnp.exp(-mt))

    hm = jax.nn.sigmoid(mo) * hm_s[...]
    return _head_rmsnorm(hm, r.ml_norm[...], H, dh)


_BRANCHES = ("gla", "rg", "ml", "hg")
_BRANCH_FN = {"gla": _gla_branch, "rg": _rglru_branch, "ml": _mlstm_branch, "hg": _hgrn_branch}
_BRANCH_CONSTS = {
    "gla": ("w_gla", "wlr", "blr", "gla_norm", "ones_gla"),
    "rg": ("w_rg", "rg_cw", "rg_cb", "rg_wa", "rg_ba", "rg_wx", "rg_bx", "rg_lam"),
    "ml": ("w_ml", "ml_cw", "ml_cb", "ml_bi", "ml_bf", "ml_norm"),
    "hg": ("w_hg", "hg_lb", "hg_norm", "ones_hg"),
}
_MERGE_CONSTS = ("w_gates", "merge_b", "w_br0", "w_br1", "w_br2", "w_br3", "w_out")
_BRANCH_STATES = {"gla": ("s_gla",), "rg": ("h_rg", "conv_rg"), "ml": ("c_ml", "n_ml", "m_ml", "conv_ml"),
                  "hg": ("s_hg",)}
_CARRIED = ("s_gla", "h_rg", "c_ml", "n_ml", "m_ml", "s_hg")
_BRANCH_WIDTH = {"gla": GLA_W, "rg": RG_WIDTH, "ml": ML_W, "hg": HG_W}


def _state_tails():
    return {"s_gla": (GLA_HEADS, GLA_DK, GLA_DV), "h_rg": (1, RG_WIDTH), "conv_rg": (SUBLANES, RG_WIDTH),
            "c_ml": (ML_HEADS, ML_DH, ML_DH), "n_ml": (ML_HEADS, ML_DH), "m_ml": (1, LANES),
            "conv_ml": (SUBLANES, 2 * ML_W), "s_hg": (HG_HEADS, HG_DK, HG_DV)}


def _branch_scratch(g, b):
    R = g.R
    if b == "gla":
        return {"gla_scr": _gl_scratch(g, GLA_HEADS, GLA_DK, GLA_DV)}
    if b == "hg":
        return {"hg_scr": _gl_scratch(g, HG_HEADS, HG_DK, HG_DV)}
    if b == "rg":
        return {"rg_carry": [pltpu.VMEM((SUBLANES, RG_WIDTH), F32)]}
    return {"ml_carry": [pltpu.VMEM((SUBLANES, 2 * ML_W), F32)],
            "ml_scr": [pltpu.VMEM((R, ML_W), F32), pltpu.VMEM((ML_W, R), BF16), pltpu.VMEM((R, ML_W), F32),
                       pltpu.VMEM((R, ML_W), BF16), pltpu.VMEM((R, LANES), F32), pltpu.VMEM((R, ML_W), F32),
                       pltpu.VMEM((R, ML_HEADS * LANES), F32), pltpu.VMEM((R, ML_W), F32)]}


def _mixer_plan(g, branches, merge):
    consts = (("gm", "tri", "lvl") if branches else ("gm",)) + sum((_BRANCH_CONSTS[b] for b in branches), ())
    consts += _MERGE_CONSTS if merge else ()
    states = sum((_BRANCH_STATES[b] for b in branches), ())
    br_in = tuple("br_" + b for b in _BRANCHES if b not in branches) if merge else ()
    outs = (("y",) if merge else tuple("br_" + b + "_o" for b in branches)) + tuple(s + "_o" for s in states)
    scratch = {}
    for b in branches:
        scratch.update(_branch_scratch(g, b))
    return consts, states, br_in, outs, scratch


def _mixer_kernel(g, branches, merge, n_alias, *refs):
    consts, states, br_in, outs, scratch = _mixer_plan(g, branches, merge)
    names = ("x",) + consts + states + br_in
    r = dict(zip(names, refs[:len(names)]))
    pos = len(names) + n_alias
    r.update(zip(outs, refs[pos:pos + len(outs)]))
    pos += len(outs)
    for nm, shapes in scratch.items():
        grp = refs[pos:pos + len(shapes)]
        r[nm] = grp[0] if nm.endswith("carry") else grp
        pos += len(shapes)
    r = collections.namedtuple("Refs", r.keys())(**r)

    @pl.when(pl.program_id(1) == 0)
    def _():
        for nm in states:
            if nm in _CARRIED:
                getattr(r, nm + "_o")[...] = getattr(r, nm)[...]

    x = r.x[...]
    xn = _rmsnorm(x, r.gm[...]).astype(BF16)

    def inproj_jobs(b):
        w_ref = getattr(r, _BRANCH_CONSTS[b][0])
        z[b] = [None] * len(_PIECES[b])
        jobs, lo = [], 0
        for i, wd in enumerate(_PIECES[b]):
            def job(i=i, lo=lo, hi=lo + wd):
                z[b][i] = _dot(xn, w_ref[:, lo:hi])
            jobs.append(job)
            lo += wd
        return jobs

    def gate_jobs(j):
        half = D_MODEL // 2
        gates[j] = [None, None]
        jobs = []
        for i in range(2):
            def job(i=i, lo=j * D_MODEL + i * half):
                pre = _dot(xn, r.w_gates[:, lo:lo + half]) + r.merge_b[j:j + 1, i * half:(i + 1) * half]
                gates[j][i] = jax.nn.sigmoid(pre).astype(BF16)
            jobs.append(job)
        return jobs

    def proj_job(j, b):
        def job():
            bv = br[b] if b in br else getattr(r, "br_" + b)[...]
            gt = jnp.concatenate(gates[j], axis=1).astype(F32)
            merged.append(gt * _dot(bv, getattr(r, "w_br%d" % j)[...]))
        return job

    z, gates, br, merged = {}, {}, {}, []
    for b in branches:
        for jb in inproj_jobs(b):
            jb()
        br[b] = _BRANCH_FN[b](g, z[b], r).astype(BF16)
    if not merge:
        for b in branches:
            getattr(r, "br_" + b + "_o")[...] = br[b]
        return
    for j, b in enumerate(_BRANCHES):
        for jb in gate_jobs(j):
            jb()
        proj_job(j, b)()
    total = merged[0]
    for term in merged[1:]:
        total = total + term
    r.y[...] = x + _dot(total.astype(BF16), r.w_out[...])


def _mlp_kernel(final, x_ref, gm_ref, wu_ref, wd_ref, gf_ref, y_ref):
    x = x_ref[...]
    xn = _rmsnorm(x, gm_ref[...]).astype(BF16)
    hdn = jnp.square(jnp.maximum(_dot(xn, wu_ref[...]), 0.0))
    y = x + _dot(hdn.astype(BF16), wd_ref[...])
    if final:
        y = _rmsnorm(y, gf_ref[...])
    y_ref[...] = y


def _const(shape):
    nd = len(shape)
    return pl.BlockSpec(shape, lambda bi, ti, _nd=nd: (0,) * _nd)


def _rows(g, C):
    return pl.BlockSpec((g.R, C), lambda bi, ti, _nt=g.NT: (bi * _nt + ti, 0))


def _layer_state(g, l, tail):
    nd = len(tail)
    return pl.BlockSpec((None, g.Bb) + tuple(tail), lambda bi, ti, _l=l, _nd=nd: (_l, bi) + (0,) * _nd)


def _call_mixer(g, l, x, W, states, prev_outs, branches, merge, br_in=None):
    n = g.B * g.T
    tails = _state_tails()
    consts, snames, br_names, outs, scratch = _mixer_plan(g, branches, merge)
    ins = [x] + [W[c] for c in consts] + [states[s] for s in snames] + [br_in[b[3:]] for b in br_names]
    in_specs = ([_rows(g, D_MODEL)] + [_const(W[c].shape) for c in consts]
                + [_layer_state(g, l, tails[s]) for s in snames] + [_rows(g, _BRANCH_WIDTH[b[3:]]) for b in br_names])
    n_main = 1 if merge else len(branches)
    aliases = {}
    if prev_outs is not None:
        for j, s in enumerate(snames):
            aliases[len(ins)] = n_main + j
            ins.append(prev_outs[s])
            in_specs.append(pl.BlockSpec(memory_space=pl.ANY))
    if merge:
        main_specs = [_rows(g, D_MODEL)]
        main_shapes = [jax.ShapeDtypeStruct((n, D_MODEL), F32)]
    else:
        main_specs = [_rows(g, _BRANCH_WIDTH[b]) for b in branches]
        main_shapes = [jax.ShapeDtypeStruct((n, _BRANCH_WIDTH[b]), BF16) for b in branches]
    out = pl.pallas_call(
        functools.partial(_mixer_kernel, g, branches, merge, len(aliases)),
        grid=(g.B // g.Bb, g.NT),
        in_specs=in_specs,
        out_specs=main_specs + [_layer_state(g, l, tails[s]) for s in snames],
        out_shape=main_shapes + [jax.ShapeDtypeStruct((DEPTH, g.B) + tails[s], F32) for s in snames],
        scratch_shapes=sum(scratch.values(), []),
        input_output_aliases=aliases,
        compiler_params=pltpu.CompilerParams(dimension_semantics=("arbitrary", "arbitrary"),
                                             vmem_limit_bytes=VMEM_LIMIT),
        name="mixer_" + "_".join(branches + (("merge",) if merge else ())),
    )(*ins)
    main = out[0] if merge else dict(zip(branches, out[:n_main]))
    return main, dict(zip(snames, out[n_main:]))


def _call_mlp(x, gm, wu, wd, gf, final):
    n = x.shape[0]
    consts = [gm, wu, wd, gf]
    tok = pl.BlockSpec((ROW_TILE, D_MODEL), lambda i: (i, 0))
    return pl.pallas_call(
        functools.partial(_mlp_kernel, final),
        grid=(n // ROW_TILE,),
        in_specs=[tok] + [pl.BlockSpec(c.shape, lambda i: (0, 0)) for c in consts],
        out_specs=tok,
        out_shape=jax.ShapeDtypeStruct((n, D_MODEL), F32),
        compiler_params=pltpu.CompilerParams(dimension_semantics=("arbitrary",), vmem_limit_bytes=VMEM_LIMIT),
        name="mlp",
    )(x, *consts)


def _cols(w, *names):
    return [w[:, _OFF[n][0]:_OFF[n][1]] for n in names]


def _pad_cols(w, width):
    return jnp.pad(w, [(0, 0)] * (w.ndim - 1) + [(0, width - w.shape[-1])])


def _block_diag(w):
    nb, d, e = w.shape
    return (jnp.eye(nb, dtype=w.dtype)[:, None, :, None] * w[:, :, None, :]).reshape(nb * d, nb * e)


def _row(v):
    return v.reshape(1, -1).astype(F32)


def _pad8(conv_state):
    return jnp.pad(conv_state, ((0, 0), (0, 0), (SUBLANES - (CONV_W - 1), 0), (0, 0)))


def _layer_weights(l, P, lbs):
    w = P["w_in"][l]
    W = {}
    W["gm"] = _row(P["norm_mix"][l])
    W["w_gla"] = jnp.concatenate(_cols(w, "g_q", "g_k", "g_v", "g_g") + [_pad_cols(_cols(w, "g_lr")[0], LANES)],
                                 axis=1).astype(BF16)
    W["wlr"] = jnp.pad(P["gla_w_lr"][l], ((0, LANES - GLA_RANK), (0, 0))).astype(BF16)
    W["blr"] = _row(P["gla_b_lr"][l])
    W["gla_norm"] = _row(P["gla_norm"][l])
    W["w_rg"] = jnp.concatenate(_cols(w, "r_x", "r_y"), axis=1).astype(BF16)
    W["rg_cw"] = P["rg_conv_w"][l].astype(F32)
    W["rg_cb"] = _row(P["rg_conv_b"][l])
    W["rg_wa"] = _block_diag(P["rg_wa"][l]).astype(BF16)
    W["rg_ba"] = _row(P["rg_ba"][l])
    W["rg_wx"] = _block_diag(P["rg_wx"][l]).astype(BF16)
    W["rg_bx"] = _row(P["rg_bx"][l])
    W["rg_lam"] = _row(P["rg_lam"][l])
    W["w_ml"] = jnp.concatenate(_cols(w, "m_q", "m_k", "m_v", "m_o")
                                + [_pad_cols(_cols(w, "m_i")[0], LANES), _pad_cols(_cols(w, "m_f")[0], LANES)],
                                axis=1).astype(BF16)
    W["ml_cw"] = P["ml_conv_w"][l].astype(F32)
    W["ml_cb"] = _row(P["ml_conv_b"][l])
    W["ml_bi"] = _pad_cols(_row(P["ml_b_i"][l]), LANES)
    W["ml_bf"] = _pad_cols(_row(P["ml_b_f"][l]), LANES)
    W["ml_norm"] = _row(P["ml_norm"][l])
    W["w_hg"] = jnp.concatenate(_cols(w, "h_q", "h_f", "h_i", "h_g"), axis=1).astype(BF16)
    W["hg_lb"] = _row(lbs[l])
    W["hg_norm"] = _row(P["hg_norm"][l])
    W["w_gates"] = _cols(w, "gates")[0].astype(BF16)
    W["merge_b"] = P["merge_b"][l].astype(F32)
    for j, nm in enumerate(("w_br_gla", "w_br_rg", "w_br_ml", "w_br_hg")):
        W["w_br%d" % j] = P[nm][l].astype(BF16)
    W["w_out"] = P["w_out"][l].astype(BF16)
    W["gf"] = _row(P["norm_ffn"][l])
    W["w_up"] = P["w_up"][l].astype(BF16)
    W["w_down"] = P["w_down"][l].astype(BF16)
    return W


def _ones_block_diag(dk, dv):
    hpg = MXU_DIM // dk
    return (np.arange(MXU_DIM)[:, None] // dk == np.arange(hpg * dv)[None, :] // dv).astype(np.float32)


def _geom_consts(g):
    t = np.arange(g.R)[:, None]
    s = np.arange(g.R)[None, :]
    keep = (s <= t) & (t // g.SEG == s // g.SEG)
    lvl = np.full((g.R, g.R), -1, np.int32)
    lvl[keep & (t // BAND == s // BAND)] = 0
    w, i = BAND, 1
    while 2 * w <= g.SEG:
        lvl[keep & (t // (2 * w) == s // (2 * w)) & (t // w != s // w)] = i
        w, i = 2 * w, i + 1
    return {"tri": jnp.asarray(keep, BF16), "lvl": jnp.asarray(lvl),
            "ones_gla": jnp.asarray(_ones_block_diag(GLA_DK, GLA_DV), BF16),
            "ones_hg": jnp.asarray(_ones_block_diag(HG_DK, HG_DV), BF16)}


def _trunk(x3, st, LW, norm_final):
    B, T, _ = x3.shape
    g = _geom(B, T)
    x = x3.reshape(B * T, D_MODEL)
    s_gla, h_rg, conv_rg, c_ml, n_ml, m_ml, conv_ml, s_hg = st
    states = {"s_gla": s_gla, "h_rg": h_rg[:, :, None, :], "conv_rg": _pad8(conv_rg), "c_ml": c_ml, "n_ml": n_ml,
              "m_ml": _pad_cols(m_ml, LANES)[:, :, None, :], "conv_ml": _pad8(conv_ml), "s_hg": s_hg}
    outs = None
    gc = _geom_consts(g)
    for l in range(DEPTH):
        W = {**LW[l], **gc}
        if g.NSEG == 1:
            x, outs = _call_mixer(g, l, x, W, states, outs, _BRANCHES, True)
        else:
            new_outs, br = {}, {}
            for b in _BRANCHES:
                res, st_b = _call_mixer(g, l, x, W, states, outs, (b,), False)
                br.update(res)
                new_outs.update(st_b)
            x, _ = _call_mixer(g, l, x, W, states, None, (), True, br)
            outs = new_outs
        x = _call_mlp(x, W["gf"], W["w_up"], W["w_down"], _row(norm_final), final=(l == DEPTH - 1))
    tail3 = SUBLANES - (CONV_W - 1)
    new = (outs["s_gla"], outs["h_rg"][:, :, 0, :], outs["conv_rg"][:, :, tail3:, :], outs["c_ml"], outs["n_ml"],
           outs["m_ml"][:, :, 0, :ML_HEADS], outs["conv_ml"][:, :, tail3:, :], outs["s_hg"])
    return x.reshape(B, T, D_MODEL), new


def _zero_states(B):
    return (jnp.zeros((DEPTH, B, GLA_HEADS, GLA_DK, GLA_DV), F32),
            jnp.zeros((DEPTH, B, RG_WIDTH), F32),
            jnp.zeros((DEPTH, B, CONV_W - 1, RG_WIDTH), F32),
            jnp.zeros((DEPTH, B, ML_HEADS, ML_DH, ML_DH), F32),
            jnp.zeros((DEPTH, B, ML_HEADS, ML_DH), F32),
            jnp.zeros((DEPTH, B, ML_HEADS), F32),
            jnp.zeros((DEPTH, B, CONV_W - 1, 2 * ML_W), F32),
            jnp.zeros((DEPTH, B, HG_HEADS, HG_DK, HG_DV), F32))


def kernel(x_prompt, x_sample, state_gla, state_rglru_h, state_rglru_conv, state_mlstm_C, state_mlstm_n, state_mlstm_m, state_mlstm_conv, state_hgrn, norm_mix, w_in, gla_w_lr, gla_b_lr, gla_norm, rg_conv_w, rg_conv_b, rg_wa, rg_ba, rg_wx, rg_bx, rg_lam, ml_conv_w, ml_conv_b, ml_b_i, ml_b_f, ml_norm, hg_gamma, hg_norm, merge_b, w_br_gla, w_br_rg, w_br_ml, w_br_hg, w_out, norm_ffn, w_up, w_down, norm_final):
    P = dict(norm_mix=norm_mix, w_in=w_in, gla_w_lr=gla_w_lr, gla_b_lr=gla_b_lr, gla_norm=gla_norm,
             rg_conv_w=rg_conv_w, rg_conv_b=rg_conv_b, rg_wa=rg_wa, rg_ba=rg_ba, rg_wx=rg_wx, rg_bx=rg_bx,
             rg_lam=rg_lam, ml_conv_w=ml_conv_w, ml_conv_b=ml_conv_b, ml_b_i=ml_b_i, ml_b_f=ml_b_f,
             ml_norm=ml_norm, hg_norm=hg_norm, merge_b=merge_b, w_br_gla=w_br_gla, w_br_rg=w_br_rg,
             w_br_ml=w_br_ml, w_br_hg=w_br_hg, w_out=w_out, norm_ffn=norm_ffn, w_up=w_up, w_down=w_down)
    sm = jax.nn.softmax(hg_gamma.astype(F32), axis=0)
    lbs = jnp.concatenate([jnp.zeros_like(sm[:1]), jnp.cumsum(sm, axis=0)[:-1]], axis=0)
    lbs = jnp.clip(lbs, 0.0, 1.0 - 1e-6)
    LW = [_layer_weights(l, P, lbs) for l in range(DEPTH)]

    y_prompt, p_st = _trunk(x_prompt, _zero_states(x_prompt.shape[0]), LW, norm_final)
    s_in = (state_gla, state_rglru_h, state_rglru_conv, state_mlstm_C, state_mlstm_n, state_mlstm_m,
            state_mlstm_conv, state_hgrn)
    y_sample, s_st = _trunk(x_sample, s_in, LW, norm_final)
    return (y_prompt, y_sample) + tuple(p_st) + tuple(s_st)
```

```python
import collections
import functools

import jax
import jax.numpy as jnp
import numpy as np
from jax import lax
from jax.experimental import pallas as pl
from jax.experimental.pallas import tpu as pltpu

F32 = jnp.float32
BF16 = jnp.bfloat16

D_MODEL = 1024
DEPTH = 2
GLA_HEADS, GLA_DK, GLA_DV, GLA_RANK = 4, 64, 128, 16
GLA_KW, GLA_W = GLA_HEADS * GLA_DK, GLA_HEADS * GLA_DV
GLA_GATE_TAU = 16.0
RG_WIDTH, RG_BLOCKS, RG_C = 512, 8, 8.0
CONV_W = 4
ML_HEADS, ML_DH = 4, 128
ML_W = ML_HEADS * ML_DH
HG_HEADS, HG_DK, HG_DV = 4, 128, 128
HG_W = HG_HEADS * HG_DV
N_BRANCH = 4
D_FF = 4 * D_MODEL
EPS = 1e-6
NEG = -1e30

LANES = 128
SUBLANES = 8
MXU_DIM = 256
ROW_TILE = 256
SUBTILES = 2
BAND = 4
LOG2E = 1.4426950408889634
SEG_UNROLL = 4
VMEM_LIMIT = 56 * 1024 * 1024

_OFF = {}
_o = 0
for _name, _size in (("g_q", GLA_KW), ("g_k", GLA_KW), ("g_v", GLA_W), ("g_lr", GLA_RANK), ("g_g", GLA_W),
                     ("r_x", RG_WIDTH), ("r_y", RG_WIDTH),
                     ("m_q", ML_W), ("m_k", ML_W), ("m_v", ML_W), ("m_o", ML_W), ("m_i", ML_HEADS), ("m_f", ML_HEADS),
                     ("h_q", HG_W), ("h_f", HG_W), ("h_i", HG_W), ("h_g", HG_W),
                     ("gates", N_BRANCH * D_MODEL)):
    _OFF[_name] = (_o, _o + _size)
    _o += _size

Geom = collections.namedtuple("Geom", "B T Bb Tt R SEG NSEG NT SUB")


def _geom(B, T):
    if T >= ROW_TILE:
        Bb, Tt = 1, ROW_TILE
        sub = SUBTILES if T % (ROW_TILE * SUBTILES) == 0 else 1
    else:
        Bb, Tt, sub = ROW_TILE // T, T, 1
    assert T % (Tt * sub) == 0 and B % Bb == 0 and Tt % SUBLANES == 0 and Tt >= SUBLANES
    return Geom(B, T, Bb, Tt, Bb * Tt, Tt, Bb, T // (Tt * sub), sub)


def _dot(a, b):
    return jnp.dot(a, b, preferred_element_type=F32)


def _dot_nt(a, b):
    return lax.dot_general(a, b, (((1,), (1,)), ((), ())), preferred_element_type=F32)


def _dot3(m01, x):
    hi = x.astype(BF16)
    r1 = x - hi.astype(F32)
    mid = r1.astype(BF16)
    lo = (r1 - mid.astype(F32)).astype(BF16)
    return _dot(m01, hi) + (_dot(m01, mid) + _dot(m01, lo))


def _iota(shape, dim):
    return lax.broadcasted_iota(jnp.int32, shape, dim)


def _div(x, n):
    assert n & (n - 1) == 0
    return x >> (n.bit_length() - 1)


def _log_sigmoid(x):
    return jnp.minimum(x, 0.0) - jnp.log(1.0 + jnp.exp(-jnp.abs(x)))


def _silu(x):
    h = 0.5 * x
    return h + h * jnp.tanh(h)


def _gelu_tanh(x):
    return 0.5 * x * (1.0 + jnp.tanh(0.7978845608028654 * (x + 0.044715 * (x * x * x))))


def _rmsnorm(x, g):
    return x * lax.rsqrt(jnp.mean(x * x, axis=-1, keepdims=True) + EPS) * g


def _head_rmsnorm(o, gn, H, d):
    outs = []
    for h in range(H):
        oh = o[:, h * d:(h + 1) * d]
        outs.append(oh * lax.rsqrt(jnp.mean(oh * oh, axis=-1, keepdims=True) + EPS) * gn[:, h * d:(h + 1) * d])
    return jnp.concatenate(outs, axis=1)


def _block_last(c, w):
    R, C = c.shape
    if w == R:
        return jnp.broadcast_to(c[R - 1:R, :], (R, C))
    c3 = c.reshape(R // w, w, C)
    return jnp.broadcast_to(c3[:, w - 1:w, :], (R // w, w, C)).reshape(R, C)


def _seg_rows(ref3, g):
    v = ref3[...]
    C = v.shape[-1]
    return jnp.broadcast_to(v, (g.NSEG, g.SEG, C)).reshape(g.R, C)


def _causal_conv(x, prev8, w_ref, b_ref, g):
    R, C = x.shape
    nb = R // SUBLANES
    sub = _iota((1, SUBLANES, 1), 1)
    x3 = x.reshape(nb, SUBLANES, C)
    p3 = prev8.reshape(nb, SUBLANES, C)
    y = b_ref[...] + x3 * w_ref[CONV_W - 1:CONV_W, :]
    for s in range(1, CONV_W):
        xs = pltpu.roll(jnp.where(sub >= SUBLANES - s, p3, x3), s, 1)
        y = y + xs * w_ref[CONV_W - 1 - s:CONV_W - s, :]
    return y.reshape(R, C)


def _conv_io(x, conv_in_ref, conv_out_ref, carry_ref, g):
    R = g.R
    if g.NSEG == 1:
        prev8 = jnp.concatenate([carry_ref[...], x[:R - SUBLANES, :]], axis=0)
        last8 = x[R - SUBLANES:, :]
        carry_ref[...] = last8
        conv_out_ref[0] = last8
    else:
        C = x.shape[-1]
        prev8 = conv_in_ref[...].reshape(R, C)
        conv_out_ref[...] = x.reshape(g.NSEG, g.SEG, C)
    return prev8


def _gl_core(q, k, v, la, s_ref, scr, g, H, dk, dv, tri, lvl, ones_bd):
    qd_s, kdT_s, v_s, o_s, decT_s = scr
    R, SEG, NSEG = g.R, g.SEG, g.NSEG
    HK = H * dk
    c = _dot3(tri, la)
    cl = _block_last(c, SEG)

    bw = BAND
    HV = H * dv
    c2 = c * LOG2E
    nb = R // SUBLANES
    sub = _iota((1, SUBLANES, 1), 1)
    sub_b = sub & (bw - 1)
    q3, k3, c3, v3 = (a.reshape(nb, SUBLANES, a.shape[1]) for a in (q, k, c2, v))
    o = None
    for d in range(bw):
        if d == 0:
            p3, vd = q3 * k3, v3
        else:
            diff = jnp.where(sub_b >= d, c3 - pltpu.roll(c3, d, 1), NEG)
            p3 = q3 * pltpu.roll(k3, d, 1) * jnp.exp2(diff)
            vd = pltpu.roll(v3, d, 1)
        pb = p3.reshape(R, HK).astype(BF16)
        parts = [_dot(pb[:, i * MXU_DIM:(i + 1) * MXU_DIM], ones_bd) for i in range(HK // MXU_DIM)]
        term = (parts[0] if len(parts) == 1 else jnp.concatenate(parts, axis=1)) * vd.reshape(R, HV)
        o = term if o is None else o + term

    vb = v.astype(BF16)
    w = bw
    a_off = [None] * H
    level = 0
    while 2 * w <= SEG:
        if w < SUBLANES:
            assert 2 * w == SUBLANES
            first = sub < w
            pmid = c3[:, w - 1:w, :]
            x = jnp.where(first, k3, q3) * jnp.exp2(jnp.where(first, pmid - c3, c3 - pmid))
            x = x.reshape(R, HK).astype(BF16)
        else:
            nb2 = R // (2 * w)
            c4 = c2.reshape(nb2, 2, w, HK)
            pmid = c4[:, 0:1, w - 1:w, :]
            xk = k.reshape(nb2, 2, w, HK)[:, 0:1] * jnp.exp2(pmid - c4[:, 0:1])
            xq = q.reshape(nb2, 2, w, HK)[:, 1:2] * jnp.exp2(c4[:, 1:2] - pmid)
            x = jnp.concatenate([xk, xq], axis=1).reshape(R, HK).astype(BF16)
        level += 1
        m = lvl == level
        for h in range(H):
            xh = x[:, h * dk:(h + 1) * dk]
            a = _dot_nt(xh, xh)
            a_off[h] = jnp.where(m, a, 0.0 if a_off[h] is None else a_off[h])
        w *= 2
    if a_off[0] is not None:
        o = o + jnp.concatenate([_dot(a_off[h].astype(BF16), vb[:, h * dv:(h + 1) * dv]) for h in range(H)], axis=1)

    qd_s[...] = q * jnp.exp(c)
    kdT_s[...] = (k * jnp.exp(cl - c)).T.astype(BF16)
    v_s[...] = vb
    decT_s[...] = jnp.exp(cl).T
    o_s[...] = o
    lane = _iota((1, R), 1)

    def seg_body(sg, carry):
        r0 = sg * SEG if isinstance(sg, int) else pl.multiple_of(sg * SEG, SEG)
        for h in range(H):
            S = s_ref[sg, h]
            qd = qd_s[pl.ds(r0, SEG), h * dk:(h + 1) * dk].astype(BF16)
            o_s[pl.ds(r0, SEG), h * dv:(h + 1) * dv] += _dot(qd, S.astype(BF16))
            kt = kdT_s[h * dk:(h + 1) * dk, :]
            if NSEG > 1:
                kt = jnp.where(_div(lane, SEG) == sg, kt, jnp.zeros_like(kt))
            U = _dot(kt, v_s[:, h * dv:(h + 1) * dv])
            dec = jnp.sum(jnp.where(lane == r0, decT_s[h * dk:(h + 1) * dk, :], 0.0), axis=1, keepdims=True)
            s_ref[sg, h] = dec * S + U
        return carry

    if NSEG == 1:
        seg_body(0, 0)
    else:
        lax.fori_loop(0, NSEG, seg_body, 0, unroll=SEG_UNROLL)
    return o_s[...]


def _gl_scratch(g, H, dk, dv):
    return [pltpu.VMEM((g.R, H * dk), F32), pltpu.VMEM((H * dk, g.R), BF16), pltpu.VMEM((g.R, H * dv), BF16),
            pltpu.VMEM((g.R, H * dv), F32), pltpu.VMEM((H * dk, g.R), F32)]


_PIECES = {"gla": (2 * GLA_KW, GLA_W, GLA_W, LANES), "rg": (RG_WIDTH, RG_WIDTH),
           "ml": (ML_W, ML_W, ML_W, ML_W, 2 * LANES), "hg": (HG_W, HG_W, HG_W, HG_W)}


def _gla_branch(g, z, r):
    qk, v, gate, lr = z
    q = qk[:, 0:GLA_KW] * (GLA_DK ** -0.5)
    k = qk[:, GLA_KW:]
    la = _log_sigmoid(_dot(lr.astype(BF16), r.wlr[...]) + r.blr[...]) * (1.0 / GLA_GATE_TAU)
    o = _gl_core(q, k, v, la, r.s_gla_o, r.gla_scr, g, GLA_HEADS, GLA_DK, GLA_DV, r.tri[...], r.lvl[...],
                 r.ones_gla[...])
    return _head_rmsnorm(o, r.gla_norm[...], GLA_HEADS, GLA_DV) * _silu(gate)


def _hgrn_branch(g, z, r):
    hq, fp, v, gate = z
    lb = r.hg_lb[...]
    ls = _log_sigmoid(fp)
    pos = lb > 0.0
    a_ = jnp.log(jnp.where(pos, lb, 1.0))
    b_ = jnp.log1p(-lb) + ls
    lae = jnp.maximum(a_, b_) + jnp.log(1.0 + jnp.exp(-jnp.abs(a_ - b_)))
    la = jnp.where(pos, lae, ls)
    k = (1.0 - lb) * jax.nn.sigmoid(-fp)
    q = _silu(hq)
    o = _gl_core(q, k, v, la, r.s_hg_o, r.hg_scr, g, HG_HEADS, HG_DK, HG_DV, r.tri[...], r.lvl[...],
                 r.ones_hg[...])
    return _head_rmsnorm(o, r.hg_norm[...], HG_HEADS, HG_DV) * _silu(gate)


def _rglru_branch(g, z, r):
    R, SEG = g.R, g.SEG
    rx, ry = z
    prev8 = _conv_io(rx, r.conv_rg, r.conv_rg_o, r.rg_carry, g)
    xc = _causal_conv(rx, prev8, r.rg_cw, r.rg_cb, g)
    xcb = xc.astype(BF16)
    rg = jax.nn.sigmoid(_dot(xcb, r.rg_wa[...]) + r.rg_ba[...])
    ig = jax.nn.sigmoid(_dot(xcb, r.rg_wx[...]) + r.rg_bx[...])
    log_a = RG_C * rg * _log_sigmoid(r.rg_lam[...])
    a = jnp.exp(log_a)
    u = jnp.sqrt(jnp.maximum(1.0 - a * a, 0.0)) * (ig * xc)
    nbs = SEG // SUBLANES
    sub = _iota((1, SUBLANES, 1), 1)
    a3 = a.reshape(R // SUBLANES, SUBLANES, RG_WIDTH)
    u3 = u.reshape(R // SUBLANES, SUBLANES, RG_WIDTH)
    d = 1
    while d < SUBLANES:
        ok = sub >= d
        a_s = jnp.where(ok, pltpu.roll(a3, d, 1), 1.0)
        u_s = jnp.where(ok, pltpu.roll(u3, d, 1), 0.0)
        u3 = a3 * u_s + u3
        a3 = a3 * a_s
        d *= 2
    a4 = a3.reshape(g.NSEG, nbs, SUBLANES, RG_WIDTH)
    u4 = u3.reshape(g.NSEG, nbs, SUBLANES, RG_WIDTH)
    hc = r.h_rg_o[...]
    blocks = []
    for j in range(nbs):
        hb = u4[:, j] + a4[:, j] * hc
        blocks.append(hb)
        hc = hb[:, SUBLANES - 1:SUBLANES, :]
    r.h_rg_o[...] = hc
    hcur = (blocks[0] if nbs == 1 else jnp.stack(blocks, axis=1)).reshape(R, RG_WIDTH)
    return hcur * _gelu_tanh(ry)


def _mlstm_branch(g, z, r):
    R, SEG, NSEG = g.R, g.SEG, g.NSEG
    H, dh = ML_HEADS, ML_DH
    q_s, wkT_s, wk_s, v_s, sc_s, qc_s, qn_s, hm_s = r.ml_scr
    cout_ref, nout_ref, mout_ref = r.c_ml_o, r.n_ml_o, r.m_ml_o
    zq, zk, mv, mo, zif = z
    qk_pre = jnp.concatenate([zq, zk], axis=1)
    ipre = zif[:, :LANES] + r.ml_bi[...]
    fpre = zif[:, LANES:] + r.ml_bf[...]
    prev8 = _conv_io(qk_pre, r.conv_ml, r.conv_ml_o, r.ml_carry, g)
    qk = _silu(_causal_conv(qk_pre, prev8, r.ml_cw, r.ml_cb, g))
    mq = qk[:, :ML_W]
    mk = qk[:, ML_W:] * (dh ** -0.5)

    keep = r.lvl[...] >= 0
    b = _dot3(r.tri[...], _log_sigmoid(fpre))
    m_rows = _seg_rows(mout_ref, g)
    prev = b + m_rows
    gT = (ipre - b).T
    bl = _block_last(b, SEG)
    wlog = bl - b + ipre
    wmax = jnp.broadcast_to(jnp.max(wlog.reshape(NSEG, SEG, LANES), axis=1, keepdims=True),
                            (NSEG, SEG, LANES)).reshape(R, LANES)
    m_new = jnp.maximum(bl + m_rows, wmax)
    wgt = jnp.exp(wlog - m_new)
    sc_s[...] = jnp.exp(bl + m_rows - m_new)
    mout_ref[...] = m_new.reshape(NSEG, SEG, LANES)[:, 0:1, :]

    q_s[...] = mq
    v_s[...] = mv.astype(BF16)
    wk = jnp.concatenate([wgt[:, h:h + 1] * mk[:, h * dh:(h + 1) * dh] for h in range(H)], axis=1)
    wk_s[...] = wk
    wkT_s[...] = wk.T.astype(BF16)
    lane = _iota((1, R), 1)

    for h in range(H):
        hs = slice(h * dh, (h + 1) * dh)
        ls = slice(h * LANES, (h + 1) * LANES)
        logd = jnp.where(keep, b[:, h:h + 1] + gT[h:h + 1, :], NEG)
        mt = jnp.maximum(prev[:, h:h + 1], jnp.max(logd, axis=1, keepdims=True))
        dm = jnp.where(keep, jnp.exp(logd - mt), 0.0)
        sp = jnp.exp(prev[:, h:h + 1] - mt)
        s_mat = _dot_nt(mq[:, hs].astype(BF16), mk[:, hs].astype(BF16)) * dm
        num = _dot(s_mat.astype(BF16), v_s[:, hs])
        den = jnp.sum(s_mat, axis=1, keepdims=True)

        def seg_body(sg, carry, h=h, hs=hs, ls=ls):
            r0 = sg * SEG if isinstance(sg, int) else pl.multiple_of(sg * SEG, SEG)
            C = cout_ref[sg, h]
            n = nout_ref[sg, pl.ds(h, 1), :]
            qb = q_s[pl.ds(r0, SEG), hs]
            qc_s[pl.ds(r0, SEG), hs] = _dot(qb.astype(BF16), C.astype(BF16))
            qn_s[pl.ds(r0, SEG), ls] = jnp.broadcast_to(jnp.sum(qb * n, axis=1, keepdims=True), (SEG, LANES))
            kt = wkT_s[hs, :]
            if NSEG > 1:
                kt = jnp.where(_div(lane, SEG) == sg, kt, jnp.zeros_like(kt))
            U = _dot(kt, v_s[:, hs])
            sc = sc_s[pl.ds(r0, 1), h:h + 1]
            cout_ref[sg, h] = sc * C + U
            nout_ref[sg, pl.ds(h, 1), :] = sc * n + jnp.sum(wk_s[pl.ds(r0, SEG), hs], axis=0, keepdims=True)
            return carry

        if NSEG == 1:
            seg_body(0, 0)
        else:
            lax.fori_loop(0, NSEG, seg_body, 0, unroll=SEG_UNROLL)
        num = num + sp * qc_s[:, hs]
        den = den + sp * qn_s[:, h * LANES:h * LANES + 1]
        hm_s[:, hs] = num / jnp.maximum(jnp.abs(den), jnp.exp(-mt))

    hm = jax.nn.sigmoid(mo) * hm_s[...]
    return _head_rmsnorm(hm, r.ml_norm[...], H, dh)


_BRANCHES = ("gla", "rg", "ml", "hg")
_BRANCH_FN = {"gla": _gla_branch, "rg": _rglru_branch, "ml": _mlstm_branch, "hg": _hgrn_branch}
_BRANCH_CONSTS = {
    "gla": ("w_gla", "wlr", "blr", "gla_norm", "ones_gla"),
    "rg": ("w_rg", "rg_cw", "rg_cb", "rg_wa", "rg_ba", "rg_wx", "rg_bx", "rg_lam"),
    "ml": ("w_ml", "ml_cw", "ml_cb", "ml_bi", "ml_bf", "ml_norm"),
    "hg": ("w_hg", "hg_lb", "hg_norm", "ones_hg"),
}
_MERGE_CONSTS = ("w_gates", "merge_b", "w_br0", "w_br1", "w_br2", "w_br3", "w_out")
_BRANCH_STATES = {"gla": ("s_gla",), "rg": ("h_rg", "conv_rg"), "ml": ("c_ml", "n_ml", "m_ml", "conv_ml"),
                  "hg": ("s_hg",)}
_CARRIED = ("s_gla", "h_rg", "c_ml", "n_ml", "m_ml", "s_hg")
_BRANCH_WIDTH = {"gla": GLA_W, "rg": RG_WIDTH, "ml": ML_W, "hg": HG_W}


def _state_tails():
    return {"s_gla": (GLA_HEADS, GLA_DK, GLA_DV), "h_rg": (1, RG_WIDTH), "conv_rg": (SUBLANES, RG_WIDTH),
            "c_ml": (ML_HEADS, ML_DH, ML_DH), "n_ml": (ML_HEADS, ML_DH), "m_ml": (1, LANES),
            "conv_ml": (SUBLANES, 2 * ML_W), "s_hg": (HG_HEADS, HG_DK, HG_DV)}


def _branch_scratch(g, b):
    R = g.R
    if b == "gla":
        return {"gla_scr": _gl_scratch(g, GLA_HEADS, GLA_DK, GLA_DV)}
    if b == "hg":
        return {"hg_scr": _gl_scratch(g, HG_HEADS, HG_DK, HG_DV)}
    if b == "rg":
        return {"rg_carry": [pltpu.VMEM((SUBLANES, RG_WIDTH), F32)]}
    return {"ml_carry": [pltpu.VMEM((SUBLANES, 2 * ML_W), F32)],
            "ml_scr": [pltpu.VMEM((R, ML_W), F32), pltpu.VMEM((ML_W, R), BF16), pltpu.VMEM((R, ML_W), F32),
                       pltpu.VMEM((R, ML_W), BF16), pltpu.VMEM((R, LANES), F32), pltpu.VMEM((R, ML_W), F32),
                       pltpu.VMEM((R, ML_HEADS * LANES), F32), pltpu.VMEM((R, ML_W), F32)]}


def _mixer_plan(g, branches, merge):
    consts = (("gm", "tri", "lvl") if branches else ("gm",)) + sum((_BRANCH_CONSTS[b] for b in branches), ())
    consts += _MERGE_CONSTS if merge else ()
    states = sum((_BRANCH_STATES[b] for b in branches), ())
    br_in = tuple("br_" + b for b in _BRANCHES if b not in branches) if merge else ()
    outs = (("y",) if merge else tuple("br_" + b + "_o" for b in branches)) + tuple(s + "_o" for s in states)
    scratch = {}
    for b in branches:
        scratch.update(_branch_scratch(g, b))
    return consts, states, br_in, outs, scratch


def _mixer_kernel(g, branches, merge, n_alias, *refs):
    consts, states, br_in, outs, scratch = _mixer_plan(g, branches, merge)
    names = ("x",) + consts + states + br_in
    r = dict(zip(names, refs[:len(names)]))
    pos = len(names) + n_alias
    r.update(zip(outs, refs[pos:pos + len(outs)]))
    pos += len(outs)
    for nm, shapes in scratch.items():
        grp = refs[pos:pos + len(shapes)]
        r[nm] = grp[0] if nm.endswith("carry") else grp
        pos += len(shapes)
    r = collections.namedtuple("Refs", r.keys())(**r)

    @pl.when(pl.program_id(1) == 0)
    def _():
        for nm in states:
            if nm in _CARRIED:
                getattr(r, nm + "_o")[...] = getattr(r, nm)[...]
        if g.NSEG == 1:
            for b, nm in (("rg", "conv_rg"), ("ml", "conv_ml")):
                if b in branches:
                    getattr(r, b + "_carry")[...] = getattr(r, nm)[0]

    for sub in range(g.SUB):
        _mixer_tile(g, branches, merge, r, slice(sub * g.R, (sub + 1) * g.R))


def _mixer_tile(g, branches, merge, r, rows):
    x = r.x[rows, :]
    xn = _rmsnorm(x, r.gm[...]).astype(BF16)

    def inproj_jobs(b):
        w_ref = getattr(r, _BRANCH_CONSTS[b][0])
        z[b] = [None] * len(_PIECES[b])
        jobs, lo = [], 0
        for i, wd in enumerate(_PIECES[b]):
            def job(i=i, lo=lo, hi=lo + wd):
                z[b][i] = _dot(xn, w_ref[:, lo:hi])
            jobs.append(job)
            lo += wd
        return jobs

    def gate_jobs(j):
        half = D_MODEL // 2
        gates[j] = [None, None]
        jobs = []
        for i in range(2):
            def job(i=i, lo=j * D_MODEL + i * half):
                pre = _dot(xn, r.w_gates[:, lo:lo + half]) + r.merge_b[j:j + 1, i * half:(i + 1) * half]
                gates[j][i] = jax.nn.sigmoid(pre).astype(BF16)
            jobs.append(job)
        return jobs

    def proj_job(j, b):
        def job():
            bv = br[b] if b in br else getattr(r, "br_" + b)[rows, :]
            gt = jnp.concatenate(gates[j], axis=1).astype(F32)
            merged.append(gt * _dot(bv, getattr(r, "w_br%d" % j)[...]))
        return job

    z, gates, br, merged = {}, {}, {}, []
    for b in branches:
        for jb in inproj_jobs(b):
            jb()
        br[b] = _BRANCH_FN[b](g, z[b], r).astype(BF16)
    if not merge:
        for b in branches:
            getattr(r, "br_" + b + "_o")[rows, :] = br[b]
        return
    for j, b in enumerate(_BRANCHES):
        for jb in gate_jobs(j):
            jb()
        proj_job(j, b)()
    total = merged[0]
    for term in merged[1:]:
        total = total + term
    r.y[rows, :] = x + _dot(total.astype(BF16), r.w_out[...])


def _mlp_kernel(final, x_ref, gm_ref, wu_ref, wd_ref, gf_ref, y_ref):
    x = x_ref[...]
    xn = _rmsnorm(x, gm_ref[...]).astype(BF16)
    hdn = jnp.square(jnp.maximum(_dot(xn, wu_ref[...]), 0.0))
    y = x + _dot(hdn.astype(BF16), wd_ref[...])
    if final:
        y = _rmsnorm(y, gf_ref[...])
    y_ref[...] = y


def _const(shape):
    nd = len(shape)
    return pl.BlockSpec(shape, lambda bi, ti, _nd=nd: (0,) * _nd)


def _rows(g, C):
    return pl.BlockSpec((g.R * g.SUB, C), lambda bi, ti, _nt=g.NT: (bi * _nt + ti, 0))


def _layer_state(g, l, tail):
    nd = len(tail)
    return pl.BlockSpec((None, g.Bb) + tuple(tail), lambda bi, ti, _l=l, _nd=nd: (_l, bi) + (0,) * _nd)


def _call_mixer(g, l, x, W, states, prev_outs, branches, merge, br_in=None):
    n = g.B * g.T
    tails = _state_tails()
    consts, snames, br_names, outs, scratch = _mixer_plan(g, branches, merge)
    ins = [x] + [W[c] for c in consts] + [states[s] for s in snames] + [br_in[b[3:]] for b in br_names]
    in_specs = ([_rows(g, D_MODEL)] + [_const(W[c].shape) for c in consts]
                + [_layer_state(g, l, tails[s]) for s in snames] + [_rows(g, _BRANCH_WIDTH[b[3:]]) for b in br_names])
    n_main = 1 if merge else len(branches)
    aliases = {}
    if prev_outs is not None:
        for j, s in enumerate(snames):
            aliases[len(ins)] = n_main + j
            ins.append(prev_outs[s])
            in_specs.append(pl.BlockSpec(memory_space=pl.ANY))
    if merge:
        main_specs = [_rows(g, D_MODEL)]
        main_shapes = [jax.ShapeDtypeStruct((n, D_MODEL), F32)]
    else:
        main_specs = [_rows(g, _BRANCH_WIDTH[b]) for b in branches]
        main_shapes = [jax.ShapeDtypeStruct((n, _BRANCH_WIDTH[b]), BF16) for b in branches]
    out = pl.pallas_call(
        functools.partial(_mixer_kernel, g, branches, merge, len(aliases)),
        grid=(g.B // g.Bb, g.NT),
        in_specs=in_specs,
        out_specs=main_specs + [_layer_state(g, l, tails[s]) for s in snames],
        out_shape=main_shapes + [jax.ShapeDtypeStruct((DEPTH, g.B) + tails[s], F32) for s in snames],
        scratch_shapes=sum(scratch.values(), []),
        input_output_aliases=aliases,
        compiler_params=pltpu.CompilerParams(dimension_semantics=("arbitrary", "arbitrary"),
                                             vmem_limit_bytes=VMEM_LIMIT),
        name="mixer_" + "_".join(branches + (("merge",) if merge else ())),
    )(*ins)
    main = out[0] if merge else dict(zip(branches, out[:n_main]))
    return main, dict(zip(snames, out[n_main:]))


def _call_mlp(x, gm, wu, wd, gf, final):
    n = x.shape[0]
    consts = [gm, wu, wd, gf]
    tok = pl.BlockSpec((ROW_TILE, D_MODEL), lambda i: (i, 0))
    return pl.pallas_call(
        functools.partial(_mlp_kernel, final),
        grid=(n // ROW_TILE,),
        in_specs=[tok] + [pl.BlockSpec(c.shape, lambda i: (0, 0)) for c in consts],
        out_specs=tok,
        out_shape=jax.ShapeDtypeStruct((n, D_MODEL), F32),
        compiler_params=pltpu.CompilerParams(dimension_semantics=("arbitrary",), vmem_limit_bytes=VMEM_LIMIT),
        name="mlp",
    )(x, *consts)


def _cols(w, *names):
    return [w[:, _OFF[n][0]:_OFF[n][1]] for n in names]


def _pad_cols(w, width):
    return jnp.pad(w, [(0, 0)] * (w.ndim - 1) + [(0, width - w.shape[-1])])


def _block_diag(w):
    nb, d, e = w.shape
    return (jnp.eye(nb, dtype=w.dtype)[:, None, :, None] * w[:, :, None, :]).reshape(nb * d, nb * e)


def _row(v):
    return v.reshape(1, -1).astype(F32)


def _pad8(conv_state):
    return jnp.pad(conv_state, ((0, 0), (0, 0), (SUBLANES - (CONV_W - 1), 0), (0, 0)))


def _layer_weights(l, P, lbs):
    w = P["w_in"][l]
    W = {}
    W["gm"] = _row(P["norm_mix"][l])
    W["w_gla"] = jnp.concatenate(_cols(w, "g_q", "g_k", "g_v", "g_g") + [_pad_cols(_cols(w, "g_lr")[0], LANES)],
                                 axis=1).astype(BF16)
    W["wlr"] = jnp.pad(P["gla_w_lr"][l], ((0, LANES - GLA_RANK), (0, 0))).astype(BF16)
    W["blr"] = _row(P["gla_b_lr"][l])
    W["gla_norm"] = _row(P["gla_norm"][l])
    W["w_rg"] = jnp.concatenate(_cols(w, "r_x", "r_y"), axis=1).astype(BF16)
    W["rg_cw"] = P["rg_conv_w"][l].astype(F32)
    W["rg_cb"] = _row(P["rg_conv_b"][l])
    W["rg_wa"] = _block_diag(P["rg_wa"][l]).astype(BF16)
    W["rg_ba"] = _row(P["rg_ba"][l])
    W["rg_wx"] = _block_diag(P["rg_wx"][l]).astype(BF16)
    W["rg_bx"] = _row(P["rg_bx"][l])
    W["rg_lam"] = _row(P["rg_lam"][l])
    W["w_ml"] = jnp.concatenate(_cols(w, "m_q", "m_k", "m_v", "m_o")
                                + [_pad_cols(_cols(w, "m_i")[0], LANES), _pad_cols(_cols(w, "m_f")[0], LANES)],
                                axis=1).astype(BF16)
    W["ml_cw"] = P["ml_conv_w"][l].astype(F32)
    W["ml_cb"] = _row(P["ml_conv_b"][l])
    W["ml_bi"] = _pad_cols(_row(P["ml_b_i"][l]), LANES)
    W["ml_bf"] = _pad_cols(_row(P["ml_b_f"][l]), LANES)
    W["ml_norm"] = _row(P["ml_norm"][l])
    W["w_hg"] = jnp.concatenate(_cols(w, "h_q", "h_f", "h_i", "h_g"), axis=1).astype(BF16)
    W["hg_lb"] = _row(lbs[l])
    W["hg_norm"] = _row(P["hg_norm"][l])
    W["w_gates"] = _cols(w, "gates")[0].astype(BF16)
    W["merge_b"] = P["merge_b"][l].astype(F32)
    for j, nm in enumerate(("w_br_gla", "w_br_rg", "w_br_ml", "w_br_hg")):
        W["w_br%d" % j] = P[nm][l].astype(BF16)
    W["w_out"] = P["w_out"][l].astype(BF16)
    W["gf"] = _row(P["norm_ffn"][l])
    W["w_up"] = P["w_up"][l].astype(BF16)
    W["w_down"] = P["w_down"][l].astype(BF16)
    return W


def _ones_block_diag(dk, dv):
    hpg = MXU_DIM // dk
    return (np.arange(MXU_DIM)[:, None] // dk == np.arange(hpg * dv)[None, :] // dv).astype(np.float32)


def _geom_consts(g):
    t = np.arange(g.R)[:, None]
    s = np.arange(g.R)[None, :]
    keep = (s <= t) & (t // g.SEG == s // g.SEG)
    lvl = np.full((g.R, g.R), -1, np.int32)
    lvl[keep & (t // BAND == s // BAND)] = 0
    w, i = BAND, 1
    while 2 * w <= g.SEG:
        lvl[keep & (t // (2 * w) == s // (2 * w)) & (t // w != s // w)] = i
        w, i = 2 * w, i + 1
    return {"tri": jnp.asarray(keep, BF16), "lvl": jnp.asarray(lvl),
            "ones_gla": jnp.asarray(_ones_block_diag(GLA_DK, GLA_DV), BF16),
            "ones_hg": jnp.asarray(_ones_block_diag(HG_DK, HG_DV), BF16)}


def _trunk(x3, st, LW, norm_final):
    B, T, _ = x3.shape
    g = _geom(B, T)
    x = x3.reshape(B * T, D_MODEL)
    s_gla, h_rg, conv_rg, c_ml, n_ml, m_ml, conv_ml, s_hg = st
    states = {"s_gla": s_gla, "h_rg": h_rg[:, :, None, :], "conv_rg": _pad8(conv_rg), "c_ml": c_ml, "n_ml": n_ml,
              "m_ml": _pad_cols(m_ml, LANES)[:, :, None, :], "conv_ml": _pad8(conv_ml), "s_hg": s_hg}
    outs = None
    gc = _geom_consts(g)
    for l in range(DEPTH):
        W = {**LW[l], **gc}
        if g.NSEG == 1:
            x, outs = _call_mixer(g, l, x, W, states, outs, _BRANCHES, True)
        else:
            new_outs, br = {}, {}
            for b in _BRANCHES:
                res, st_b = _call_mixer(g, l, x, W, states, outs, (b,), False)
                br.update(res)
                new_outs.update(st_b)
            x, _ = _call_mixer(g, l, x, W, states, None, (), True, br)
            outs = new_outs
        x = _call_mlp(x, W["gf"], W["w_up"], W["w_down"], _row(norm_final), final=(l == DEPTH - 1))
    tail3 = SUBLANES - (CONV_W - 1)
    new = (outs["s_gla"], outs["h_rg"][:, :, 0, :], outs["conv_rg"][:, :, tail3:, :], outs["c_ml"], outs["n_ml"],
           outs["m_ml"][:, :, 0, :ML_HEADS], outs["conv_ml"][:, :, tail3:, :], outs["s_hg"])
    return x.reshape(B, T, D_MODEL), new


def _zero_states(B):
    return (jnp.zeros((DEPTH, B, GLA_HEADS, GLA_DK, GLA_DV), F32),
            jnp.zeros((DEPTH, B, RG_WIDTH), F32),
            jnp.zeros((DEPTH, B, CONV_W - 1, RG_WIDTH), F32),
            jnp.zeros((DEPTH, B, ML_HEADS, ML_DH, ML_DH), F32),
            jnp.zeros((DEPTH, B, ML_HEADS, ML_DH), F32),
            jnp.zeros((DEPTH, B, ML_HEADS), F32),
            jnp.zeros((DEPTH, B, CONV_W - 1, 2 * ML_W), F32),
            jnp.zeros((DEPTH, B, HG_HEADS, HG_DK, HG_DV), F32))


def kernel(x_prompt, x_sample, state_gla, state_rglru_h, state_rglru_conv, state_mlstm_C, state_mlstm_n, state_mlstm_m, state_mlstm_conv, state_hgrn, norm_mix, w_in, gla_w_lr, gla_b_lr, gla_norm, rg_conv_w, rg_conv_b, rg_wa, rg_ba, rg_wx, rg_bx, rg_lam, ml_conv_w, ml_conv_b, ml_b_i, ml_b_f, ml_norm, hg_gamma, hg_norm, merge_b, w_br_gla, w_br_rg, w_br_ml, w_br_hg, w_out, norm_ffn, w_up, w_down, norm_final):
    P = dict(norm_mix=norm_mix, w_in=w_in, gla_w_lr=gla_w_lr, gla_b_lr=gla_b_lr, gla_norm=gla_norm,
             rg_conv_w=rg_conv_w, rg_conv_b=rg_conv_b, rg_wa=rg_wa, rg_ba=rg_ba, rg_wx=rg_wx, rg_bx=rg_bx,
             rg_lam=rg_lam, ml_conv_w=ml_conv_w, ml_conv_b=ml_conv_b, ml_b_i=ml_b_i, ml_b_f=ml_b_f,
             ml_norm=ml_norm, hg_norm=hg_norm, merge_b=merge_b, w_br_gla=w_br_gla, w_br_rg=w_br_rg,
             w_br_ml=w_br_ml, w_br_hg=w_br_hg, w_out=w_out, norm_ffn=norm_ffn, w_up=w_up, w_down=w_down)
    sm = jax.nn.softmax(hg_gamma.astype(F32), axis=0)
    lbs = jnp.concatenate([jnp.zeros_like(sm[:1]), jnp.cumsum(sm, axis=0)[:-1]], axis=0)
    lbs = jnp.clip(lbs, 0.0, 1.0 - 1e-6)
    LW = [_layer_weights(l, P, lbs) for l in range(DEPTH)]

    y_prompt, p_st = _trunk(x_prompt, _zero_states(x_prompt.shape[0]), LW, norm_final)
    s_in = (state_gla, state_rglru_h, state_rglru_conv, state_mlstm_C, state_mlstm_n, state_mlstm_m,
            state_mlstm_conv, state_hgrn)
    y_sample, s_st = _trunk(x_sample, s_in, LW, norm_final)
    return (y_prompt, y_sample) + tuple(p_st) + tuple(s_st)
```

```python
import collections
import functools

import jax
import jax.numpy as jnp
import numpy as np
from jax import lax
from jax.experimental import pallas as pl
from jax.experimental.pallas import tpu as pltpu

F32 = jnp.float32
BF16 = jnp.bfloat16

D_MODEL = 1024
DEPTH = 2
GLA_HEADS, GLA_DK, GLA_DV, GLA_RANK = 4, 64, 128, 16
GLA_KW, GLA_W = GLA_HEADS * GLA_DK, GLA_HEADS * GLA_DV
GLA_GATE_TAU = 16.0
RG_WIDTH, RG_BLOCKS, RG_C = 512, 8, 8.0
CONV_W = 4
ML_HEADS, ML_DH = 4, 128
ML_W = ML_HEADS * ML_DH
HG_HEADS, HG_DK, HG_DV = 4, 128, 128
HG_W = HG_HEADS * HG_DV
N_BRANCH = 4
D_FF = 4 * D_MODEL
EPS = 1e-6
NEG = -1e30

LANES = 128
SUBLANES = 8
MXU_DIM = 256
ROW_TILE = 256
SUBTILES = 2
BAND = 4
LOG2E = 1.4426950408889634
SEG_UNROLL = 16
VMEM_LIMIT = 56 * 1024 * 1024

_OFF = {}
_o = 0
for _name, _size in (("g_q", GLA_KW), ("g_k", GLA_KW), ("g_v", GLA_W), ("g_lr", GLA_RANK), ("g_g", GLA_W),
                     ("r_x", RG_WIDTH), ("r_y", RG_WIDTH),
                     ("m_q", ML_W), ("m_k", ML_W), ("m_v", ML_W), ("m_o", ML_W), ("m_i", ML_HEADS), ("m_f", ML_HEADS),
                     ("h_q", HG_W), ("h_f", HG_W), ("h_i", HG_W), ("h_g", HG_W),
                     ("gates", N_BRANCH * D_MODEL)):
    _OFF[_name] = (_o, _o + _size)
    _o += _size

Geom = collections.namedtuple("Geom", "B T Bb Tt R SEG NSEG NT SUB")


def _geom(B, T):
    if T >= ROW_TILE:
        Bb, Tt = 1, ROW_TILE
        sub = SUBTILES if T % (ROW_TILE * SUBTILES) == 0 else 1
    else:
        Bb, Tt, sub = ROW_TILE // T, T, 1
    assert T % (Tt * sub) == 0 and B % Bb == 0 and Tt % SUBLANES == 0 and Tt >= SUBLANES
    return Geom(B, T, Bb, Tt, Bb * Tt, Tt, Bb, T // (Tt * sub), sub)


def _dot(a, b):
    return jnp.dot(a, b, preferred_element_type=F32)


def _dot_nt(a, b):
    return lax.dot_general(a, b, (((1,), (1,)), ((), ())), preferred_element_type=F32)


def _dot3(m01, x):
    hi = x.astype(BF16)
    r1 = x - hi.astype(F32)
    mid = r1.astype(BF16)
    lo = (r1 - mid.astype(F32)).astype(BF16)
    return _dot(m01, hi) + (_dot(m01, mid) + _dot(m01, lo))


def _iota(shape, dim):
    return lax.broadcasted_iota(jnp.int32, shape, dim)


def _div(x, n):
    assert n & (n - 1) == 0
    return x >> (n.bit_length() - 1)


def _log_sigmoid(x):
    return jnp.minimum(x, 0.0) - jnp.log(1.0 + jnp.exp(-jnp.abs(x)))


def _silu(x):
    h = 0.5 * x
    return h + h * jnp.tanh(h)


def _gelu_tanh(x):
    return 0.5 * x * (1.0 + jnp.tanh(0.7978845608028654 * (x + 0.044715 * (x * x * x))))


def _rmsnorm(x, g):
    return x * lax.rsqrt(jnp.mean(x * x, axis=-1, keepdims=True) + EPS) * g


def _head_rmsnorm(o, gn, H, d):
    outs = []
    for h in range(H):
        oh = o[:, h * d:(h + 1) * d]
        outs.append(oh * lax.rsqrt(jnp.mean(oh * oh, axis=-1, keepdims=True) + EPS) * gn[:, h * d:(h + 1) * d])
    return jnp.concatenate(outs, axis=1)


def _block_last(c, w):
    R, C = c.shape
    if w == R:
        return jnp.broadcast_to(c[R - 1:R, :], (R, C))
    c3 = c.reshape(R // w, w, C)
    return jnp.broadcast_to(c3[:, w - 1:w, :], (R // w, w, C)).reshape(R, C)


def _seg_rows(ref3, g):
    v = ref3[...]
    C = v.shape[-1]
    return jnp.broadcast_to(v, (g.NSEG, g.SEG, C)).reshape(g.R, C)


def _causal_conv(x, prev8, w_ref, b_ref, g):
    R, C = x.shape
    nb = R // SUBLANES
    sub = _iota((1, SUBLANES, 1), 1)
    x3 = x.reshape(nb, SUBLANES, C)
    p3 = prev8.reshape(nb, SUBLANES, C)
    y = b_ref[...] + x3 * w_ref[CONV_W - 1:CONV_W, :]
    for s in range(1, CONV_W):
        xs = pltpu.roll(jnp.where(sub >= SUBLANES - s, p3, x3), s, 1)
        y = y + xs * w_ref[CONV_W - 1 - s:CONV_W - s, :]
    return y.reshape(R, C)


def _conv_io(x, conv_in_ref, conv_out_ref, carry_ref, g):
    R = g.R
    if g.NSEG == 1:
        prev8 = jnp.concatenate([carry_ref[...], x[:R - SUBLANES, :]], axis=0)
        last8 = x[R - SUBLANES:, :]
        carry_ref[...] = last8
        conv_out_ref[0] = last8
    else:
        C = x.shape[-1]
        prev8 = conv_in_ref[...].reshape(R, C)
        conv_out_ref[...] = x.reshape(g.NSEG, g.SEG, C)
    return prev8


def _gl_core(q, k, v, la, s_ref, scr, g, H, dk, dv, tri, lvl, ones_bd):
    qd_s, kdT_s, v_s, o_s, decT_s = scr
    R, SEG, NSEG = g.R, g.SEG, g.NSEG
    HK = H * dk
    c = _dot3(tri, la)
    cl = _block_last(c, SEG)

    bw = BAND
    HV = H * dv
    c2 = c * LOG2E
    nb = R // SUBLANES
    sub = _iota((1, SUBLANES, 1), 1)
    sub_b = sub & (bw - 1)
    q3, k3, c3, v3 = (a.reshape(nb, SUBLANES, a.shape[1]) for a in (q, k, c2, v))
    o = None
    for d in range(bw):
        if d == 0:
            p3, vd = q3 * k3, v3
        else:
            diff = jnp.where(sub_b >= d, c3 - pltpu.roll(c3, d, 1), NEG)
            p3 = q3 * pltpu.roll(k3, d, 1) * jnp.exp2(diff)
            vd = pltpu.roll(v3, d, 1)
        pb = p3.reshape(R, HK).astype(BF16)
        parts = [_dot(pb[:, i * MXU_DIM:(i + 1) * MXU_DIM], ones_bd) for i in range(HK // MXU_DIM)]
        term = (parts[0] if len(parts) == 1 else jnp.concatenate(parts, axis=1)) * vd.reshape(R, HV)
        o = term if o is None else o + term

    vb = v.astype(BF16)
    w = bw
    a_off = [None] * H
    level = 0
    while 2 * w <= SEG:
        if w < SUBLANES:
            assert 2 * w == SUBLANES
            first = sub < w
            pmid = c3[:, w - 1:w, :]
            x = jnp.where(first, k3, q3) * jnp.exp2(jnp.where(first, pmid - c3, c3 - pmid))
            x = x.reshape(R, HK).astype(BF16)
        else:
            nb2 = R // (2 * w)
            c4 = c2.reshape(nb2, 2, w, HK)
            pmid = c4[:, 0:1, w - 1:w, :]
            xk = k.reshape(nb2, 2, w, HK)[:, 0:1] * jnp.exp2(pmid - c4[:, 0:1])
            xq = q.reshape(nb2, 2, w, HK)[:, 1:2] * jnp.exp2(c4[:, 1:2] - pmid)
            x = jnp.concatenate([xk, xq], axis=1).reshape(R, HK).astype(BF16)
        level += 1
        m = lvl == level
        for h in range(H):
            xh = x[:, h * dk:(h + 1) * dk]
            a = _dot_nt(xh, xh)
            a_off[h] = jnp.where(m, a, 0.0 if a_off[h] is None else a_off[h])
        w *= 2
    if a_off[0] is not None:
        o = o + jnp.concatenate([_dot(a_off[h].astype(BF16), vb[:, h * dv:(h + 1) * dv]) for h in range(H)], axis=1)

    qd_s[...] = q * jnp.exp(c)
    kdT_s[...] = (k * jnp.exp(cl - c)).T.astype(BF16)
    v_s[...] = vb
    decT_s[...] = jnp.exp(cl).T
    o_s[...] = o
    lane = _iota((1, R), 1)

    def seg_body(sg, carry):
        r0 = sg * SEG if isinstance(sg, int) else pl.multiple_of(sg * SEG, SEG)
        for h in range(H):
            S = s_ref[sg, h]
            qd = qd_s[pl.ds(r0, SEG), h * dk:(h + 1) * dk].astype(BF16)
            o_s[pl.ds(r0, SEG), h * dv:(h + 1) * dv] += _dot(qd, S.astype(BF16))
            kt = kdT_s[h * dk:(h + 1) * dk, :]
            if NSEG > 1:
                kt = jnp.where(_div(lane, SEG) == sg, kt, jnp.zeros_like(kt))
            U = _dot(kt, v_s[:, h * dv:(h + 1) * dv])
            dec = jnp.sum(jnp.where(lane == r0, decT_s[h * dk:(h + 1) * dk, :], 0.0), axis=1, keepdims=True)
            s_ref[sg, h] = dec * S + U
        return carry

    if NSEG == 1:
        seg_body(0, 0)
    else:
        lax.fori_loop(0, NSEG, seg_body, 0, unroll=SEG_UNROLL)
    return o_s[...]


def _gl_scratch(g, H, dk, dv):
    return [pltpu.VMEM((g.R, H * dk), F32), pltpu.VMEM((H * dk, g.R), BF16), pltpu.VMEM((g.R, H * dv), BF16),
            pltpu.VMEM((g.R, H * dv), F32), pltpu.VMEM((H * dk, g.R), F32)]


_PIECES = {"gla": (2 * GLA_KW, GLA_W, GLA_W, LANES), "rg": (RG_WIDTH, RG_WIDTH),
           "ml": (ML_W, ML_W, ML_W, ML_W, 2 * LANES), "hg": (HG_W, HG_W, HG_W, HG_W)}


def _gla_branch(g, z, r):
    qk, v, gate, lr = z
    q = qk[:, 0:GLA_KW] * (GLA_DK ** -0.5)
    k = qk[:, GLA_KW:]
    la = _log_sigmoid(_dot(lr.astype(BF16), r.wlr[...]) + r.blr[...]) * (1.0 / GLA_GATE_TAU)
    o = _gl_core(q, k, v, la, r.s_gla_o, r.gla_scr, g, GLA_HEADS, GLA_DK, GLA_DV, r.tri[...], r.lvl[...],
                 r.ones_gla[...])
    return _head_rmsnorm(o, r.gla_norm[...], GLA_HEADS, GLA_DV) * _silu(gate)


def _hgrn_branch(g, z, r):
    hq, fp, v, gate = z
    lb = r.hg_lb[...]
    ls = _log_sigmoid(fp)
    pos = lb > 0.0
    a_ = jnp.log(jnp.where(pos, lb, 1.0))
    b_ = jnp.log1p(-lb) + ls
    lae = jnp.maximum(a_, b_) + jnp.log(1.0 + jnp.exp(-jnp.abs(a_ - b_)))
    la = jnp.where(pos, lae, ls)
    k = (1.0 - lb) * jax.nn.sigmoid(-fp)
    q = _silu(hq)
    o = _gl_core(q, k, v, la, r.s_hg_o, r.hg_scr, g, HG_HEADS, HG_DK, HG_DV, r.tri[...], r.lvl[...],
                 r.ones_hg[...])
    return _head_rmsnorm(o, r.hg_norm[...], HG_HEADS, HG_DV) * _silu(gate)


def _rglru_branch(g, z, r):
    R, SEG = g.R, g.SEG
    rx, ry = z
    prev8 = _conv_io(rx, r.conv_rg, r.conv_rg_o, r.rg_carry, g)
    xc = _causal_conv(rx, prev8, r.rg_cw, r.rg_cb, g)
    xcb = xc.astype(BF16)
    rg = jax.nn.sigmoid(_dot(xcb, r.rg_wa[...]) + r.rg_ba[...])
    ig = jax.nn.sigmoid(_dot(xcb, r.rg_wx[...]) + r.rg_bx[...])
    log_a = RG_C * rg * _log_sigmoid(r.rg_lam[...])
    a = jnp.exp(log_a)
    u = jnp.sqrt(jnp.maximum(1.0 - a * a, 0.0)) * (ig * xc)
    nbs = SEG // SUBLANES
    sub = _iota((1, SUBLANES, 1), 1)
    a3 = a.reshape(R // SUBLANES, SUBLANES, RG_WIDTH)
    u3 = u.reshape(R // SUBLANES, SUBLANES, RG_WIDTH)
    d = 1
    while d < SUBLANES:
        ok = sub >= d
        a_s = jnp.where(ok, pltpu.roll(a3, d, 1), 1.0)
        u_s = jnp.where(ok, pltpu.roll(u3, d, 1), 0.0)
        u3 = a3 * u_s + u3
        a3 = a3 * a_s
        d *= 2
    a4 = a3.reshape(g.NSEG, nbs, SUBLANES, RG_WIDTH)
    u4 = u3.reshape(g.NSEG, nbs, SUBLANES, RG_WIDTH)
    hc = r.h_rg_o[...]
    blocks = []
    for j in range(nbs):
        hb = u4[:, j] + a4[:, j] * hc
        blocks.append(hb)
        hc = hb[:, SUBLANES - 1:SUBLANES, :]
    r.h_rg_o[...] = hc
    hcur = (blocks[0] if nbs == 1 else jnp.stack(blocks, axis=1)).reshape(R, RG_WIDTH)
    return hcur * _gelu_tanh(ry)


def _mlstm_branch(g, z, r):
    R, SEG, NSEG = g.R, g.SEG, g.NSEG
    H, dh = ML_HEADS, ML_DH
    q_s, wkT_s, wk_s, v_s, sc_s, qc_s, qn_s, hm_s = r.ml_scr
    cout_ref, nout_ref, mout_ref = r.c_ml_o, r.n_ml_o, r.m_ml_o
    zq, zk, mv, mo, zif = z
    qk_pre = jnp.concatenate([zq, zk], axis=1)
    ipre = zif[:, :LANES] + r.ml_bi[...]
    fpre = zif[:, LANES:] + r.ml_bf[...]
    prev8 = _conv_io(qk_pre, r.conv_ml, r.conv_ml_o, r.ml_carry, g)
    qk = _silu(_causal_conv(qk_pre, prev8, r.ml_cw, r.ml_cb, g))
    mq = qk[:, :ML_W]
    mk = qk[:, ML_W:] * (dh ** -0.5)

    keep = r.lvl[...] >= 0
    b = _dot3(r.tri[...], _log_sigmoid(fpre))
    m_rows = _seg_rows(mout_ref, g)
    prev = b + m_rows
    gT = (ipre - b).T
    bl = _block_last(b, SEG)
    wlog = bl - b + ipre
    wmax = jnp.broadcast_to(jnp.max(wlog.reshape(NSEG, SEG, LANES), axis=1, keepdims=True),
                            (NSEG, SEG, LANES)).reshape(R, LANES)
    m_new = jnp.maximum(bl + m_rows, wmax)
    wgt = jnp.exp(wlog - m_new)
    sc_s[...] = jnp.exp(bl + m_rows - m_new)
    mout_ref[...] = m_new.reshape(NSEG, SEG, LANES)[:, 0:1, :]

    q_s[...] = mq
    v_s[...] = mv.astype(BF16)
    wk = jnp.concatenate([wgt[:, h:h + 1] * mk[:, h * dh:(h + 1) * dh] for h in range(H)], axis=1)
    wk_s[...] = wk
    wkT_s[...] = wk.T.astype(BF16)
    lane = _iota((1, R), 1)

    for h in range(H):
        hs = slice(h * dh, (h + 1) * dh)
        ls = slice(h * LANES, (h + 1) * LANES)
        logd = jnp.where(keep, b[:, h:h + 1] + gT[h:h + 1, :], NEG)
        mt = jnp.maximum(prev[:, h:h + 1], jnp.max(logd, axis=1, keepdims=True))
        dm = jnp.where(keep, jnp.exp(logd - mt), 0.0)
        sp = jnp.exp(prev[:, h:h + 1] - mt)
        s_mat = _dot_nt(mq[:, hs].astype(BF16), mk[:, hs].astype(BF16)) * dm
        num = _dot(s_mat.astype(BF16), v_s[:, hs])
        den = jnp.sum(s_mat, axis=1, keepdims=True)

        def seg_body(sg, carry, h=h, hs=hs, ls=ls):
            r0 = sg * SEG if isinstance(sg, int) else pl.multiple_of(sg * SEG, SEG)
            C = cout_ref[sg, h]
            n = nout_ref[sg, pl.ds(h, 1), :]
            qb = q_s[pl.ds(r0, SEG), hs]
            qc_s[pl.ds(r0, SEG), hs] = _dot(qb.astype(BF16), C.astype(BF16))
            qn_s[pl.ds(r0, SEG), ls] = jnp.broadcast_to(jnp.sum(qb * n, axis=1, keepdims=True), (SEG, LANES))
            kt = wkT_s[hs, :]
            if NSEG > 1:
                kt = jnp.where(_div(lane, SEG) == sg, kt, jnp.zeros_like(kt))
            U = _dot(kt, v_s[:, hs])
            sc = sc_s[pl.ds(r0, 1), h:h + 1]
            cout_ref[sg, h] = sc * C + U
            nout_ref[sg, pl.ds(h, 1), :] = sc * n + jnp.sum(wk_s[pl.ds(r0, SEG), hs], axis=0, keepdims=True)
            return carry

        if NSEG == 1:
            seg_body(0, 0)
        else:
            lax.fori_loop(0, NSEG, seg_body, 0, unroll=SEG_UNROLL)
        num = num + sp * qc_s[:, hs]
        den = den + sp * qn_s[:, h * LANES:h * LANES + 1]
        hm_s[:, hs] = num / jnp.maximum(jnp.abs(den), jnp.exp(-mt))

    hm = jax.nn.sigmoid(mo) * hm_s[...]
    return _head_rmsnorm(hm, r.ml_norm[...], H, dh)


_BRANCHES = ("gla", "rg", "ml", "hg")
_BRANCH_FN = {"gla": _gla_branch, "rg": _rglru_branch, "ml": _mlstm_branch, "hg": _hgrn_branch}
_BRANCH_CONSTS = {
    "gla": ("w_gla", "wlr", "blr", "gla_norm", "ones_gla"),
    "rg": ("w_rg", "rg_cw", "rg_cb", "rg_wa", "rg_ba", "rg_wx", "rg_bx", "rg_lam"),
    "ml": ("w_ml", "ml_cw", "ml_cb", "ml_bi", "ml_bf", "ml_norm"),
    "hg": ("w_hg", "hg_lb", "hg_norm", "ones_hg"),
}
_MERGE_CONSTS = ("w_gates", "merge_b", "w_br0", "w_br1", "w_br2", "w_br3", "w_out")
_BRANCH_STATES = {"gla": ("s_gla",), "rg": ("h_rg", "conv_rg"), "ml": ("c_ml", "n_ml", "m_ml", "conv_ml"),
                  "hg": ("s_hg",)}
_CARRIED = ("s_gla", "h_rg", "c_ml", "n_ml", "m_ml", "s_hg")
_BRANCH_WIDTH = {"gla": GLA_W, "rg": RG_WIDTH, "ml": ML_W, "hg": HG_W}


def _state_tails():
    return {"s_gla": (GLA_HEADS, GLA_DK, GLA_DV), "h_rg": (1, RG_WIDTH), "conv_rg": (SUBLANES, RG_WIDTH),
            "c_ml": (ML_HEADS, ML_DH, ML_DH), "n_ml": (ML_HEADS, ML_DH), "m_ml": (1, LANES),
            "conv_ml": (SUBLANES, 2 * ML_W), "s_hg": (HG_HEADS, HG_DK, HG_DV)}


def _branch_scratch(g, b):
    R = g.R
    if b == "gla":
        return {"gla_scr": _gl_scratch(g, GLA_HEADS, GLA_DK, GLA_DV)}
    if b == "hg":
        return {"hg_scr": _gl_scratch(g, HG_HEADS, HG_DK, HG_DV)}
    if b == "rg":
        return {"rg_carry": [pltpu.VMEM((SUBLANES, RG_WIDTH), F32)]}
    return {"ml_carry": [pltpu.VMEM((SUBLANES, 2 * ML_W), F32)],
            "ml_scr": [pltpu.VMEM((R, ML_W), F32), pltpu.VMEM((ML_W, R), BF16), pltpu.VMEM((R, ML_W), F32),
                       pltpu.VMEM((R, ML_W), BF16), pltpu.VMEM((R, LANES), F32), pltpu.VMEM((R, ML_W), F32),
                       pltpu.VMEM((R, ML_HEADS * LANES), F32), pltpu.VMEM((R, ML_W), F32)]}


def _mixer_plan(g, branches, merge):
    consts = (("gm", "tri", "lvl") if branches else ("gm",)) + sum((_BRANCH_CONSTS[b] for b in branches), ())
    consts += _MERGE_CONSTS if merge else ()
    states = sum((_BRANCH_STATES[b] for b in branches), ())
    br_in = tuple("br_" + b for b in _BRANCHES if b not in branches) if merge else ()
    outs = (("y",) if merge else tuple("br_" + b + "_o" for b in branches)) + tuple(s + "_o" for s in states)
    scratch = {}
    for b in branches:
        scratch.update(_branch_scratch(g, b))
    return consts, states, br_in, outs, scratch


def _mixer_kernel(g, branches, merge, n_alias, *refs):
    consts, states, br_in, outs, scratch = _mixer_plan(g, branches, merge)
    names = ("x",) + consts + states + br_in
    r = dict(zip(names, refs[:len(names)]))
    pos = len(names) + n_alias
    r.update(zip(outs, refs[pos:pos + len(outs)]))
    pos += len(outs)
    for nm, shapes in scratch.items():
        grp = refs[pos:pos + len(shapes)]
        r[nm] = grp[0] if nm.endswith("carry") else grp
        pos += len(shapes)
    r = collections.namedtuple("Refs", r.keys())(**r)

    @pl.when(pl.program_id(1) == 0)
    def _():
        for nm in states:
            if nm in _CARRIED:
                getattr(r, nm + "_o")[...] = getattr(r, nm)[...]
        if g.NSEG == 1:
            for b, nm in (("rg", "conv_rg"), ("ml", "conv_ml")):
                if b in branches:
                    getattr(r, b + "_carry")[...] = getattr(r, nm)[0]

    for sub in range(g.SUB):
        _mixer_tile(g, branches, merge, r, slice(sub * g.R, (sub + 1) * g.R))


def _mixer_tile(g, branches, merge, r, rows):
    x = r.x[rows, :]
    xn = _rmsnorm(x, r.gm[...]).astype(BF16)

    def inproj_jobs(b):
        w_ref = getattr(r, _BRANCH_CONSTS[b][0])
        z[b] = [None] * len(_PIECES[b])
        jobs, lo = [], 0
        for i, wd in enumerate(_PIECES[b]):
            def job(i=i, lo=lo, hi=lo + wd):
                z[b][i] = _dot(xn, w_ref[:, lo:hi])
            jobs.append(job)
            lo += wd
        return jobs

    def gate_jobs(j):
        half = D_MODEL // 2
        gates[j] = [None, None]
        jobs = []
        for i in range(2):
            def job(i=i, lo=j * D_MODEL + i * half):
                pre = _dot(xn, r.w_gates[:, lo:lo + half]) + r.merge_b[j:j + 1, i * half:(i + 1) * half]
                gates[j][i] = jax.nn.sigmoid(pre).astype(BF16)
            jobs.append(job)
        return jobs

    def proj_job(j, b):
        def job():
            bv = br[b] if b in br else getattr(r, "br_" + b)[rows, :]
            gt = jnp.concatenate(gates[j], axis=1).astype(F32)
            merged.append(gt * _dot(bv, getattr(r, "w_br%d" % j)[...]))
        return job

    z, gates, br, merged = {}, {}, {}, []
    for b in branches:
        for jb in inproj_jobs(b):
            jb()
        br[b] = _BRANCH_FN[b](g, z[b], r).astype(BF16)
    if not merge:
        for b in branches:
            getattr(r, "br_" + b + "_o")[rows, :] = br[b]
        return
    for j, b in enumerate(_BRANCHES):
        for jb in gate_jobs(j):
            jb()
        proj_job(j, b)()
    total = merged[0]
    for term in merged[1:]:
        total = total + term
    r.y[rows, :] = x + _dot(total.astype(BF16), r.w_out[...])


def _mlp_kernel(final, x_ref, gm_ref, wu_ref, wd_ref, gf_ref, y_ref):
    x = x_ref[...]
    xn = _rmsnorm(x, gm_ref[...]).astype(BF16)
    hdn = jnp.square(jnp.maximum(_dot(xn, wu_ref[...]), 0.0))
    y = x + _dot(hdn.astype(BF16), wd_ref[...])
    if final:
        y = _rmsnorm(y, gf_ref[...])
    y_ref[...] = y


def _const(shape):
    nd = len(shape)
    return pl.BlockSpec(shape, lambda bi, ti, _nd=nd: (0,) * _nd)


def _rows(g, C):
    return pl.BlockSpec((g.R * g.SUB, C), lambda bi, ti, _nt=g.NT: (bi * _nt + ti, 0))


def _layer_state(g, l, tail):
    nd = len(tail)
    return pl.BlockSpec((None, g.Bb) + tuple(tail), lambda bi, ti, _l=l, _nd=nd: (_l, bi) + (0,) * _nd)


def _call_mixer(g, l, x, W, states, prev_outs, branches, merge, br_in=None):
    n = g.B * g.T
    tails = _state_tails()
    consts, snames, br_names, outs, scratch = _mixer_plan(g, branches, merge)
    ins = [x] + [W[c] for c in consts] + [states[s] for s in snames] + [br_in[b[3:]] for b in br_names]
    in_specs = ([_rows(g, D_MODEL)] + [_const(W[c].shape) for c in consts]
                + [_layer_state(g, l, tails[s]) for s in snames] + [_rows(g, _BRANCH_WIDTH[b[3:]]) for b in br_names])
    n_main = 1 if merge else len(branches)
    aliases = {}
    if prev_outs is not None:
        for j, s in enumerate(snames):
            aliases[len(ins)] = n_main + j
            ins.append(prev_outs[s])
            in_specs.append(pl.BlockSpec(memory_space=pl.ANY))
    if merge:
        main_specs = [_rows(g, D_MODEL)]
        main_shapes = [jax.ShapeDtypeStruct((n, D_MODEL), F32)]
    else:
        main_specs = [_rows(g, _BRANCH_WIDTH[b]) for b in branches]
        main_shapes = [jax.ShapeDtypeStruct((n, _BRANCH_WIDTH[b]), BF16) for b in branches]
    out = pl.pallas_call(
        functools.partial(_mixer_kernel, g, branches, merge, len(aliases)),
        grid=(g.B // g.Bb, g.NT),
        in_specs=in_specs,
        out_specs=main_specs + [_layer_state(g, l, tails[s]) for s in snames],
        out_shape=main_shapes + [jax.ShapeDtypeStruct((DEPTH, g.B) + tails[s], F32) for s in snames],
        scratch_shapes=sum(scratch.values(), []),
        input_output_aliases=aliases,
        compiler_params=pltpu.CompilerParams(dimension_semantics=("arbitrary", "arbitrary"),
                                             vmem_limit_bytes=VMEM_LIMIT),
        name="mixer_" + "_".join(branches + (("merge",) if merge else ())),
    )(*ins)
    main = out[0] if merge else dict(zip(branches, out[:n_main]))
    return main, dict(zip(snames, out[n_main:]))


def _call_mlp(x, gm, wu, wd, gf, final):
    n = x.shape[0]
    consts = [gm, wu, wd, gf]
    tok = pl.BlockSpec((ROW_TILE, D_MODEL), lambda i: (i, 0))
    return pl.pallas_call(
        functools.partial(_mlp_kernel, final),
        grid=(n // ROW_TILE,),
        in_specs=[tok] + [pl.BlockSpec(c.shape, lambda i: (0, 0)) for c in consts],
        out_specs=tok,
        out_shape=jax.ShapeDtypeStruct((n, D_MODEL), F32),
        compiler_params=pltpu.CompilerParams(dimension_semantics=("arbitrary",), vmem_limit_bytes=VMEM_LIMIT),
        name="mlp",
    )(x, *consts)


def _cols(w, *names):
    return [w[:, _OFF[n][0]:_OFF[n][1]] for n in names]


def _pad_cols(w, width):
    return jnp.pad(w, [(0, 0)] * (w.ndim - 1) + [(0, width - w.shape[-1])])


def _block_diag(w):
    nb, d, e = w.shape
    return (jnp.eye(nb, dtype=w.dtype)[:, None, :, None] * w[:, :, None, :]).reshape(nb * d, nb * e)


def _row(v):
    return v.reshape(1, -1).astype(F32)


def _pad8(conv_state):
    return jnp.pad(conv_state, ((0, 0), (0, 0), (SUBLANES - (CONV_W - 1), 0), (0, 0)))


def _layer_weights(l, P, lbs):
    w = P["w_in"][l]
    W = {}
    W["gm"] = _row(P["norm_mix"][l])
    W["w_gla"] = jnp.concatenate(_cols(w, "g_q", "g_k", "g_v", "g_g") + [_pad_cols(_cols(w, "g_lr")[0], LANES)],
                                 axis=1).astype(BF16)
    W["wlr"] = jnp.pad(P["gla_w_lr"][l], ((0, LANES - GLA_RANK), (0, 0))).astype(BF16)
    W["blr"] = _row(P["gla_b_lr"][l])
    W["gla_norm"] = _row(P["gla_norm"][l])
    W["w_rg"] = jnp.concatenate(_cols(w, "r_x", "r_y"), axis=1).astype(BF16)
    W["rg_cw"] = P["rg_conv_w"][l].astype(F32)
    W["rg_cb"] = _row(P["rg_conv_b"][l])
    W["rg_wa"] = _block_diag(P["rg_wa"][l]).astype(BF16)
    W["rg_ba"] = _row(P["rg_ba"][l])
    W["rg_wx"] = _block_diag(P["rg_wx"][l]).astype(BF16)
    W["rg_bx"] = _row(P["rg_bx"][l])
    W["rg_lam"] = _row(P["rg_lam"][l])
    W["w_ml"] = jnp.concatenate(_cols(w, "m_q", "m_k", "m_v", "m_o")
                                + [_pad_cols(_cols(w, "m_i")[0], LANES), _pad_cols(_cols(w, "m_f")[0], LANES)],
                                axis=1).astype(BF16)
    W["ml_cw"] = P["ml_conv_w"][l].astype(F32)
    W["ml_cb"] = _row(P["ml_conv_b"][l])
    W["ml_bi"] = _pad_cols(_row(P["ml_b_i"][l]), LANES)
    W["ml_bf"] = _pad_cols(_row(P["ml_b_f"][l]), LANES)
    W["ml_norm"] = _row(P["ml_norm"][l])
    W["w_hg"] = jnp.concatenate(_cols(w, "h_q", "h_f", "h_i", "h_g"), axis=1).astype(BF16)
    W["hg_lb"] = _row(lbs[l])
    W["hg_norm"] = _row(P["hg_norm"][l])
    W["w_gates"] = _cols(w, "gates")[0].astype(BF16)
    W["merge_b"] = P["merge_b"][l].astype(F32)
    for j, nm in enumerate(("w_br_gla", "w_br_rg", "w_br_ml", "w_br_hg")):
        W["w_br%d" % j] = P[nm][l].astype(BF16)
    W["w_out"] = P["w_out"][l].astype(BF16)
    W["gf"] = _row(P["norm_ffn"][l])
    W["w_up"] = P["w_up"][l].astype(BF16)
    W["w_down"] = P["w_down"][l].astype(BF16)
    return W


def _ones_block_diag(dk, dv):
    hpg = MXU_DIM // dk
    return (np.arange(MXU_DIM)[:, None] // dk == np.arange(hpg * dv)[None, :] // dv).astype(np.float32)


def _geom_consts(g):
    t = np.arange(g.R)[:, None]
    s = np.arange(g.R)[None, :]
    keep = (s <= t) & (t // g.SEG == s // g.SEG)
    lvl = np.full((g.R, g.R), -1, np.int32)
    lvl[keep & (t // BAND == s // BAND)] = 0
    w, i = BAND, 1
    while 2 * w <= g.SEG:
        lvl[keep & (t // (2 * w) == s // (2 * w)) & (t // w != s // w)] = i
        w, i = 2 * w, i + 1
    return {"tri": jnp.asarray(keep, BF16), "lvl": jnp.asarray(lvl),
            "ones_gla": jnp.asarray(_ones_block_diag(GLA_DK, GLA_DV), BF16),
            "ones_hg": jnp.asarray(_ones_block_diag(HG_DK, HG_DV), BF16)}


def _trunk(x3, st, LW, norm_final):
    B, T, _ = x3.shape
    g = _geom(B, T)
    x = x3.reshape(B * T, D_MODEL)
    s_gla, h_rg, conv_rg, c_ml, n_ml, m_ml, conv_ml, s_hg = st
    states = {"s_gla": s_gla, "h_rg": h_rg[:, :, None, :], "conv_rg": _pad8(conv_rg), "c_ml": c_ml, "n_ml": n_ml,
              "m_ml": _pad_cols(m_ml, LANES)[:, :, None, :], "conv_ml": _pad8(conv_ml), "s_hg": s_hg}
    outs = None
    gc = _geom_consts(g)
    for l in range(DEPTH):
        W = {**LW[l], **gc}
        if g.NSEG == 1:
            x, outs = _call_mixer(g, l, x, W, states, outs, _BRANCHES, True)
        else:
            new_outs, br = {}, {}
            for b in _BRANCHES:
                res, st_b = _call_mixer(g, l, x, W, states, outs, (b,), False)
                br.update(res)
                new_outs.update(st_b)
            x, _ = _call_mixer(g, l, x, W, states, None, (), True, br)
            outs = new_outs
        x = _call_mlp(x, W["gf"], W["w_up"], W["w_down"], _row(norm_final), final=(l == DEPTH - 1))
    tail3 = SUBLANES - (CONV_W - 1)
    new = (outs["s_gla"], outs["h_rg"][:, :, 0, :], outs["conv_rg"][:, :, tail3:, :], outs["c_ml"], outs["n_ml"],
           outs["m_ml"][:, :, 0, :ML_HEADS], outs["conv_ml"][:, :, tail3:, :], outs["s_hg"])
    return x.reshape(B, T, D_MODEL), new


def _zero_states(B):
    return (jnp.zeros((DEPTH, B, GLA_HEADS, GLA_DK, GLA_DV), F32),
            jnp.zeros((DEPTH, B, RG_WIDTH), F32),
            jnp.zeros((DEPTH, B, CONV_W - 1, RG_WIDTH), F32),
            jnp.zeros((DEPTH, B, ML_HEADS, ML_DH, ML_DH), F32),
            jnp.zeros((DEPTH, B, ML_HEADS, ML_DH), F32),
            jnp.zeros((DEPTH, B, ML_HEADS), F32),
            jnp.zeros((DEPTH, B, CONV_W - 1, 2 * ML_W), F32),
            jnp.zeros((DEPTH, B, HG_HEADS, HG_DK, HG_DV), F32))


def kernel(x_prompt, x_sample, state_gla, state_rglru_h, state_rglru_conv, state_mlstm_C, state_mlstm_n, state_mlstm_m, state_mlstm_conv, state_hgrn, norm_mix, w_in, gla_w_lr, gla_b_lr, gla_norm, rg_conv_w, rg_conv_b, rg_wa, rg_ba, rg_wx, rg_bx, rg_lam, ml_conv_w, ml_conv_b, ml_b_i, ml_b_f, ml_norm, hg_gamma, hg_norm, merge_b, w_br_gla, w_br_rg, w_br_ml, w_br_hg, w_out, norm_ffn, w_up, w_down, norm_final):
    P = dict(norm_mix=norm_mix, w_in=w_in, gla_w_lr=gla_w_lr, gla_b_lr=gla_b_lr, gla_norm=gla_norm,
             rg_conv_w=rg_conv_w, rg_conv_b=rg_conv_b, rg_wa=rg_wa, rg_ba=rg_ba, rg_wx=rg_wx, rg_bx=rg_bx,
             rg_lam=rg_lam, ml_conv_w=ml_conv_w, ml_conv_b=ml_conv_b, ml_b_i=ml_b_i, ml_b_f=ml_b_f,
             ml_norm=ml_norm, hg_norm=hg_norm, merge_b=merge_b, w_br_gla=w_br_gla, w_br_rg=w_br_rg,
             w_br_ml=w_br_ml, w_br_hg=w_br_hg, w_out=w_out, norm_ffn=norm_ffn, w_up=w_up, w_down=w_down)
    sm = jax.nn.softmax(hg_gamma.astype(F32), axis=0)
    lbs = jnp.concatenate([jnp.zeros_like(sm[:1]), jnp.cumsum(sm, axis=0)[:-1]], axis=0)
    lbs = jnp.clip(lbs, 0.0, 1.0 - 1e-6)
    LW = [_layer_weights(l, P, lbs) for l in range(DEPTH)]

    y_prompt, p_st = _trunk(x_prompt, _zero_states(x_prompt.shape[0]), LW, norm_final)
    s_in = (state_gla, state_rglru_h, state_rglru_conv, state_mlstm_C, state_mlstm_n, state_mlstm_m,
            state_mlstm_conv, state_hgrn)
    y_sample, s_st = _trunk(x_sample, s_in, LW, norm_final)
    return (y_prompt, y_sample) + tuple(p_st) + tuple(s_st)
```

```python
import collections
import functools

import jax
import jax.numpy as jnp
import numpy as np
from jax import lax
from jax.experimental import pallas as pl
from jax.experimental.pallas import tpu as pltpu

F32 = jnp.float32
BF16 = jnp.bfloat16

D_MODEL = 1024
DEPTH = 2
GLA_HEADS, GLA_DK, GLA_DV, GLA_RANK = 4, 64, 128, 16
GLA_KW, GLA_W = GLA_HEADS * GLA_DK, GLA_HEADS * GLA_DV
GLA_GATE_TAU = 16.0
RG_WIDTH, RG_BLOCKS, RG_C = 512, 8, 8.0
CONV_W = 4
ML_HEADS, ML_DH = 4, 128
ML_W = ML_HEADS * ML_DH
HG_HEADS, HG_DK, HG_DV = 4, 128, 128
HG_W = HG_HEADS * HG_DV
N_BRANCH = 4
D_FF = 4 * D_MODEL
EPS = 1e-6
NEG = -1e30

LANES = 128
SUBLANES = 8
MXU_DIM = 256
ROW_TILE = 256
MLP_TILE = 512
SUBTILES = 2
BAND = 4
LOG2E = 1.4426950408889634
SEG_UNROLL = 16
VMEM_LIMIT = 56 * 1024 * 1024

_OFF = {}
_o = 0
for _name, _size in (("g_q", GLA_KW), ("g_k", GLA_KW), ("g_v", GLA_W), ("g_lr", GLA_RANK), ("g_g", GLA_W),
                     ("r_x", RG_WIDTH), ("r_y", RG_WIDTH),
                     ("m_q", ML_W), ("m_k", ML_W), ("m_v", ML_W), ("m_o", ML_W), ("m_i", ML_HEADS), ("m_f", ML_HEADS),
                     ("h_q", HG_W), ("h_f", HG_W), ("h_i", HG_W), ("h_g", HG_W),
                     ("gates", N_BRANCH * D_MODEL)):
    _OFF[_name] = (_o, _o + _size)
    _o += _size

Geom = collections.namedtuple("Geom", "B T Bb Tt R SEG NSEG NT SUB")


def _geom(B, T):
    if T >= ROW_TILE:
        Bb, Tt = 1, ROW_TILE
        sub = SUBTILES if T % (ROW_TILE * SUBTILES) == 0 else 1
    else:
        Bb, Tt, sub = ROW_TILE // T, T, 1
    assert T % (Tt * sub) == 0 and B % Bb == 0 and Tt % SUBLANES == 0 and Tt >= SUBLANES
    return Geom(B, T, Bb, Tt, Bb * Tt, Tt, Bb, T // (Tt * sub), sub)


def _dot(a, b):
    return jnp.dot(a, b, preferred_element_type=F32)


def _dot_nt(a, b):
    return lax.dot_general(a, b, (((1,), (1,)), ((), ())), preferred_element_type=F32)


def _dot3(m01, x):
    hi = x.astype(BF16)
    r1 = x - hi.astype(F32)
    mid = r1.astype(BF16)
    lo = (r1 - mid.astype(F32)).astype(BF16)
    return _dot(m01, hi) + (_dot(m01, mid) + _dot(m01, lo))


def _iota(shape, dim):
    return lax.broadcasted_iota(jnp.int32, shape, dim)


def _div(x, n):
    assert n & (n - 1) == 0
    return x >> (n.bit_length() - 1)


def _log_sigmoid(x):
    return jnp.minimum(x, 0.0) - jnp.log(1.0 + jnp.exp(-jnp.abs(x)))


def _silu(x):
    h = 0.5 * x
    return h + h * jnp.tanh(h)


def _gelu_tanh(x):
    return 0.5 * x * (1.0 + jnp.tanh(0.7978845608028654 * (x + 0.044715 * (x * x * x))))


def _rmsnorm(x, g):
    return x * lax.rsqrt(jnp.mean(x * x, axis=-1, keepdims=True) + EPS) * g


def _head_rmsnorm(o, gn, H, d):
    outs = []
    for h in range(H):
        oh = o[:, h * d:(h + 1) * d]
        outs.append(oh * lax.rsqrt(jnp.mean(oh * oh, axis=-1, keepdims=True) + EPS) * gn[:, h * d:(h + 1) * d])
    return jnp.concatenate(outs, axis=1)


def _block_last(c, w):
    R, C = c.shape
    if w == R:
        return jnp.broadcast_to(c[R - 1:R, :], (R, C))
    c3 = c.reshape(R // w, w, C)
    return jnp.broadcast_to(c3[:, w - 1:w, :], (R // w, w, C)).reshape(R, C)


def _seg_rows(ref3, g):
    v = ref3[...]
    C = v.shape[-1]
    return jnp.broadcast_to(v, (g.NSEG, g.SEG, C)).reshape(g.R, C)


def _causal_conv(x, prev8, w_ref, b_ref, g):
    R, C = x.shape
    nb = R // SUBLANES
    sub = _iota((1, SUBLANES, 1), 1)
    x3 = x.reshape(nb, SUBLANES, C)
    p3 = prev8.reshape(nb, SUBLANES, C)
    y = b_ref[...] + x3 * w_ref[CONV_W - 1:CONV_W, :]
    for s in range(1, CONV_W):
        xs = pltpu.roll(jnp.where(sub >= SUBLANES - s, p3, x3), s, 1)
        y = y + xs * w_ref[CONV_W - 1 - s:CONV_W - s, :]
    return y.reshape(R, C)


def _conv_io(x, conv_in_ref, conv_out_ref, carry_ref, g):
    R = g.R
    if g.NSEG == 1:
        prev8 = jnp.concatenate([carry_ref[...], x[:R - SUBLANES, :]], axis=0)
        last8 = x[R - SUBLANES:, :]
        carry_ref[...] = last8
        conv_out_ref[0] = last8
    else:
        C = x.shape[-1]
        prev8 = conv_in_ref[...].reshape(R, C)
        conv_out_ref[...] = x.reshape(g.NSEG, g.SEG, C)
    return prev8


def _gl_core(q, k, v, la, s_ref, scr, g, H, dk, dv, tri, lvl, ones_bd):
    qd_s, kdT_s, v_s, o_s, decT_s = scr
    R, SEG, NSEG = g.R, g.SEG, g.NSEG
    HK = H * dk
    c = _dot3(tri, la)
    cl = _block_last(c, SEG)

    bw = BAND
    HV = H * dv
    c2 = c * LOG2E
    nb = R // SUBLANES
    sub = _iota((1, SUBLANES, 1), 1)
    sub_b = sub & (bw - 1)
    q3, k3, c3, v3 = (a.reshape(nb, SUBLANES, a.shape[1]) for a in (q, k, c2, v))
    o = None
    for d in range(bw):
        if d == 0:
            p3, vd = q3 * k3, v3
        else:
            diff = jnp.where(sub_b >= d, c3 - pltpu.roll(c3, d, 1), NEG)
            p3 = q3 * pltpu.roll(k3, d, 1) * jnp.exp2(diff)
            vd = pltpu.roll(v3, d, 1)
        pb = p3.reshape(R, HK).astype(BF16)
        parts = [_dot(pb[:, i * MXU_DIM:(i + 1) * MXU_DIM], ones_bd) for i in range(HK // MXU_DIM)]
        term = (parts[0] if len(parts) == 1 else jnp.concatenate(parts, axis=1)) * vd.reshape(R, HV)
        o = term if o is None else o + term

    vb = v.astype(BF16)
    w = bw
    a_off = [None] * H
    level = 0
    while 2 * w <= SEG:
        if w < SUBLANES:
            assert 2 * w == SUBLANES
            first = sub < w
            pmid = c3[:, w - 1:w, :]
            x = jnp.where(first, k3, q3) * jnp.exp2(jnp.where(first, pmid - c3, c3 - pmid))
            x = x.reshape(R, HK).astype(BF16)
        else:
            nb2 = R // (2 * w)
            c4 = c2.reshape(nb2, 2, w, HK)
            pmid = c4[:, 0:1, w - 1:w, :]
            xk = k.reshape(nb2, 2, w, HK)[:, 0:1] * jnp.exp2(pmid - c4[:, 0:1])
            xq = q.reshape(nb2, 2, w, HK)[:, 1:2] * jnp.exp2(c4[:, 1:2] - pmid)
            x = jnp.concatenate([xk, xq], axis=1).reshape(R, HK).astype(BF16)
        level += 1
        m = lvl == level
        for h in range(H):
            xh = x[:, h * dk:(h + 1) * dk]
            a = _dot_nt(xh, xh)
            a_off[h] = jnp.where(m, a, 0.0 if a_off[h] is None else a_off[h])
        w *= 2
    if a_off[0] is not None:
        o = o + jnp.concatenate([_dot(a_off[h].astype(BF16), vb[:, h * dv:(h + 1) * dv]) for h in range(H)], axis=1)

    qd_s[...] = q * jnp.exp(c)
    kdT_s[...] = (k * jnp.exp(cl - c)).T.astype(BF16)
    v_s[...] = vb
    decT_s[...] = jnp.exp(cl).T
    o_s[...] = o
    lane = _iota((1, R), 1)

    def seg_body(sg, carry):
        r0 = sg * SEG if isinstance(sg, int) else pl.multiple_of(sg * SEG, SEG)
        for h in range(H):
            S = s_ref[sg, h]
            qd = qd_s[pl.ds(r0, SEG), h * dk:(h + 1) * dk].astype(BF16)
            o_s[pl.ds(r0, SEG), h * dv:(h + 1) * dv] += _dot(qd, S.astype(BF16))
            kt = kdT_s[h * dk:(h + 1) * dk, :]
            if NSEG > 1:
                kt = jnp.where(_div(lane, SEG) == sg, kt, jnp.zeros_like(kt))
            U = _dot(kt, v_s[:, h * dv:(h + 1) * dv])
            dec = jnp.sum(jnp.where(lane == r0, decT_s[h * dk:(h + 1) * dk, :], 0.0), axis=1, keepdims=True)
            s_ref[sg, h] = dec * S + U
        return carry

    if NSEG == 1:
        seg_body(0, 0)
    else:
        lax.fori_loop(0, NSEG, seg_body, 0, unroll=SEG_UNROLL)
    return o_s[...]


def _gl_scratch(g, H, dk, dv):
    return [pltpu.VMEM((g.R, H * dk), F32), pltpu.VMEM((H * dk, g.R), BF16), pltpu.VMEM((g.R, H * dv), BF16),
            pltpu.VMEM((g.R, H * dv), F32), pltpu.VMEM((H * dk, g.R), F32)]


_PIECES = {"gla": (2 * GLA_KW, GLA_W, GLA_W, LANES), "rg": (RG_WIDTH, RG_WIDTH),
           "ml": (ML_W, ML_W, ML_W, ML_W, 2 * LANES), "hg": (HG_W, HG_W, HG_W, HG_W)}


def _gla_branch(g, z, r):
    qk, v, gate, lr = z
    q = qk[:, 0:GLA_KW] * (GLA_DK ** -0.5)
    k = qk[:, GLA_KW:]
    la = _log_sigmoid(_dot(lr.astype(BF16), r.wlr[...]) + r.blr[...]) * (1.0 / GLA_GATE_TAU)
    o = _gl_core(q, k, v, la, r.s_gla_o, r.gla_scr, g, GLA_HEADS, GLA_DK, GLA_DV, r.tri[...], r.lvl[...],
                 r.ones_gla[...])
    return _head_rmsnorm(o, r.gla_norm[...], GLA_HEADS, GLA_DV) * _silu(gate)


def _hgrn_branch(g, z, r):
    hq, fp, v, gate = z
    lb = r.hg_lb[...]
    ls = _log_sigmoid(fp)
    pos = lb > 0.0
    a_ = jnp.log(jnp.where(pos, lb, 1.0))
    b_ = jnp.log1p(-lb) + ls
    lae = jnp.maximum(a_, b_) + jnp.log(1.0 + jnp.exp(-jnp.abs(a_ - b_)))
    la = jnp.where(pos, lae, ls)
    k = (1.0 - lb) * jax.nn.sigmoid(-fp)
    q = _silu(hq)
    o = _gl_core(q, k, v, la, r.s_hg_o, r.hg_scr, g, HG_HEADS, HG_DK, HG_DV, r.tri[...], r.lvl[...],
                 r.ones_hg[...])
    return _head_rmsnorm(o, r.hg_norm[...], HG_HEADS, HG_DV) * _silu(gate)


def _rglru_branch(g, z, r):
    R, SEG = g.R, g.SEG
    rx, ry = z
    prev8 = _conv_io(rx, r.conv_rg, r.conv_rg_o, r.rg_carry, g)
    xc = _causal_conv(rx, prev8, r.rg_cw, r.rg_cb, g)
    xcb = xc.astype(BF16)
    rg = jax.nn.sigmoid(_dot(xcb, r.rg_wa[...]) + r.rg_ba[...])
    ig = jax.nn.sigmoid(_dot(xcb, r.rg_wx[...]) + r.rg_bx[...])
    log_a = RG_C * rg * _log_sigmoid(r.rg_lam[...])
    a = jnp.exp(log_a)
    u = jnp.sqrt(jnp.maximum(1.0 - a * a, 0.0)) * (ig * xc)
    nbs = SEG // SUBLANES
    sub = _iota((1, SUBLANES, 1), 1)
    a3 = a.reshape(R // SUBLANES, SUBLANES, RG_WIDTH)
    u3 = u.reshape(R // SUBLANES, SUBLANES, RG_WIDTH)
    d = 1
    while d < SUBLANES:
        ok = sub >= d
        a_s = jnp.where(ok, pltpu.roll(a3, d, 1), 1.0)
        u_s = jnp.where(ok, pltpu.roll(u3, d, 1), 0.0)
        u3 = a3 * u_s + u3
        a3 = a3 * a_s
        d *= 2
    a4 = a3.reshape(g.NSEG, nbs, SUBLANES, RG_WIDTH)
    u4 = u3.reshape(g.NSEG, nbs, SUBLANES, RG_WIDTH)
    hc = r.h_rg_o[...]
    blocks = []
    for j in range(nbs):
        hb = u4[:, j] + a4[:, j] * hc
        blocks.append(hb)
        hc = hb[:, SUBLANES - 1:SUBLANES, :]
    r.h_rg_o[...] = hc
    hcur = (blocks[0] if nbs == 1 else jnp.stack(blocks, axis=1)).reshape(R, RG_WIDTH)
    return hcur * _gelu_tanh(ry)


def _mlstm_branch(g, z, r):
    R, SEG, NSEG = g.R, g.SEG, g.NSEG
    H, dh = ML_HEADS, ML_DH
    q_s, wkT_s, wk_s, v_s, sc_s, qc_s, qn_s, hm_s = r.ml_scr
    cout_ref, nout_ref, mout_ref = r.c_ml_o, r.n_ml_o, r.m_ml_o
    zq, zk, mv, mo, zif = z
    qk_pre = jnp.concatenate([zq, zk], axis=1)
    ipre = zif[:, :LANES] + r.ml_bi[...]
    fpre = zif[:, LANES:] + r.ml_bf[...]
    prev8 = _conv_io(qk_pre, r.conv_ml, r.conv_ml_o, r.ml_carry, g)
    qk = _silu(_causal_conv(qk_pre, prev8, r.ml_cw, r.ml_cb, g))
    mq = qk[:, :ML_W]
    mk = qk[:, ML_W:] * (dh ** -0.5)

    keep = r.lvl[...] >= 0
    b = _dot3(r.tri[...], _log_sigmoid(fpre))
    m_rows = _seg_rows(mout_ref, g)
    prev = b + m_rows
    gT = (ipre - b).T
    bl = _block_last(b, SEG)
    wlog = bl - b + ipre
    wmax = jnp.broadcast_to(jnp.max(wlog.reshape(NSEG, SEG, LANES), axis=1, keepdims=True),
                            (NSEG, SEG, LANES)).reshape(R, LANES)
    m_new = jnp.maximum(bl + m_rows, wmax)
    wgt = jnp.exp(wlog - m_new)
    sc_s[...] = jnp.exp(bl + m_rows - m_new)
    mout_ref[...] = m_new.reshape(NSEG, SEG, LANES)[:, 0:1, :]

    q_s[...] = mq
    v_s[...] = mv.astype(BF16)
    wk = jnp.concatenate([wgt[:, h:h + 1] * mk[:, h * dh:(h + 1) * dh] for h in range(H)], axis=1)
    wk_s[...] = wk
    wkT_s[...] = wk.T.astype(BF16)
    lane = _iota((1, R), 1)

    for h in range(H):
        hs = slice(h * dh, (h + 1) * dh)
        ls = slice(h * LANES, (h + 1) * LANES)
        logd = jnp.where(keep, b[:, h:h + 1] + gT[h:h + 1, :], NEG)
        mt = jnp.maximum(prev[:, h:h + 1], jnp.max(logd, axis=1, keepdims=True))
        dm = jnp.where(keep, jnp.exp(logd - mt), 0.0)
        sp = jnp.exp(prev[:, h:h + 1] - mt)
        s_mat = _dot_nt(mq[:, hs].astype(BF16), mk[:, hs].astype(BF16)) * dm
        num = _dot(s_mat.astype(BF16), v_s[:, hs])
        den = jnp.sum(s_mat, axis=1, keepdims=True)

        def seg_body(sg, carry, h=h, hs=hs, ls=ls):
            r0 = sg * SEG if isinstance(sg, int) else pl.multiple_of(sg * SEG, SEG)
            C = cout_ref[sg, h]
            n = nout_ref[sg, pl.ds(h, 1), :]
            qb = q_s[pl.ds(r0, SEG), hs]
            qc_s[pl.ds(r0, SEG), hs] = _dot(qb.astype(BF16), C.astype(BF16))
            qn_s[pl.ds(r0, SEG), ls] = jnp.broadcast_to(jnp.sum(qb * n, axis=1, keepdims=True), (SEG, LANES))
            kt = wkT_s[hs, :]
            if NSEG > 1:
                kt = jnp.where(_div(lane, SEG) == sg, kt, jnp.zeros_like(kt))
            U = _dot(kt, v_s[:, hs])
            sc = sc_s[pl.ds(r0, 1), h:h + 1]
            cout_ref[sg, h] = sc * C + U
            nout_ref[sg, pl.ds(h, 1), :] = sc * n + jnp.sum(wk_s[pl.ds(r0, SEG), hs], axis=0, keepdims=True)
            return carry

        if NSEG == 1:
            seg_body(0, 0)
        else:
            lax.fori_loop(0, NSEG, seg_body, 0, unroll=SEG_UNROLL)
        num = num + sp * qc_s[:, hs]
        den = den + sp * qn_s[:, h * LANES:h * LANES + 1]
        hm_s[:, hs] = num / jnp.maximum(jnp.abs(den), jnp.exp(-mt))

    hm = jax.nn.sigmoid(mo) * hm_s[...]
    return _head_rmsnorm(hm, r.ml_norm[...], H, dh)


_BRANCHES = ("gla", "rg", "ml", "hg")
_BRANCH_FN = {"gla": _gla_branch, "rg": _rglru_branch, "ml": _mlstm_branch, "hg": _hgrn_branch}
_BRANCH_CONSTS = {
    "gla": ("w_gla", "wlr", "blr", "gla_norm", "ones_gla"),
    "rg": ("w_rg", "rg_cw", "rg_cb", "rg_wa", "rg_ba", "rg_wx", "rg_bx", "rg_lam"),
    "ml": ("w_ml", "ml_cw", "ml_cb", "ml_bi", "ml_bf", "ml_norm"),
    "hg": ("w_hg", "hg_lb", "hg_norm", "ones_hg"),
}
_MERGE_CONSTS = ("w_gates", "merge_b", "w_br0", "w_br1", "w_br2", "w_br3", "w_out")
_BRANCH_STATES = {"gla": ("s_gla",), "rg": ("h_rg", "conv_rg"), "ml": ("c_ml", "n_ml", "m_ml", "conv_ml"),
                  "hg": ("s_hg",)}
_CARRIED = ("s_gla", "h_rg", "c_ml", "n_ml", "m_ml", "s_hg")
_BRANCH_WIDTH = {"gla": GLA_W, "rg": RG_WIDTH, "ml": ML_W, "hg": HG_W}


def _state_tails():
    return {"s_gla": (GLA_HEADS, GLA_DK, GLA_DV), "h_rg": (1, RG_WIDTH), "conv_rg": (SUBLANES, RG_WIDTH),
            "c_ml": (ML_HEADS, ML_DH, ML_DH), "n_ml": (ML_HEADS, ML_DH), "m_ml": (1, LANES),
            "conv_ml": (SUBLANES, 2 * ML_W), "s_hg": (HG_HEADS, HG_DK, HG_DV)}


def _branch_scratch(g, b):
    R = g.R
    if b == "gla":
        return {"gla_scr": _gl_scratch(g, GLA_HEADS, GLA_DK, GLA_DV)}
    if b == "hg":
        return {"hg_scr": _gl_scratch(g, HG_HEADS, HG_DK, HG_DV)}
    if b == "rg":
        return {"rg_carry": [pltpu.VMEM((SUBLANES, RG_WIDTH), F32)]}
    return {"ml_carry": [pltpu.VMEM((SUBLANES, 2 * ML_W), F32)],
            "ml_scr": [pltpu.VMEM((R, ML_W), F32), pltpu.VMEM((ML_W, R), BF16), pltpu.VMEM((R, ML_W), F32),
                       pltpu.VMEM((R, ML_W), BF16), pltpu.VMEM((R, LANES), F32), pltpu.VMEM((R, ML_W), F32),
                       pltpu.VMEM((R, ML_HEADS * LANES), F32), pltpu.VMEM((R, ML_W), F32)]}


def _mixer_plan(g, branches, merge):
    consts = (("gm", "tri", "lvl") if branches else ("gm",)) + sum((_BRANCH_CONSTS[b] for b in branches), ())
    consts += _MERGE_CONSTS if merge else ()
    states = sum((_BRANCH_STATES[b] for b in branches), ())
    br_in = tuple("br_" + b for b in _BRANCHES if b not in branches) if merge else ()
    outs = (("y",) if merge else tuple("br_" + b + "_o" for b in branches)) + tuple(s + "_o" for s in states)
    scratch = {}
    for b in branches:
        scratch.update(_branch_scratch(g, b))
    return consts, states, br_in, outs, scratch


def _mixer_kernel(g, branches, merge, n_alias, *refs):
    consts, states, br_in, outs, scratch = _mixer_plan(g, branches, merge)
    names = ("x",) + consts + states + br_in
    r = dict(zip(names, refs[:len(names)]))
    pos = len(names) + n_alias
    r.update(zip(outs, refs[pos:pos + len(outs)]))
    pos += len(outs)
    for nm, shapes in scratch.items():
        grp = refs[pos:pos + len(shapes)]
        r[nm] = grp[0] if nm.endswith("carry") else grp
        pos += len(shapes)
    r = collections.namedtuple("Refs", r.keys())(**r)

    @pl.when(pl.program_id(1) == 0)
    def _():
        for nm in states:
            if nm in _CARRIED:
                getattr(r, nm + "_o")[...] = getattr(r, nm)[...]
        if g.NSEG == 1:
            for b, nm in (("rg", "conv_rg"), ("ml", "conv_ml")):
                if b in branches:
                    getattr(r, b + "_carry")[...] = getattr(r, nm)[0]

    for sub in range(g.SUB):
        _mixer_tile(g, branches, merge, r, slice(sub * g.R, (sub + 1) * g.R))


def _mixer_tile(g, branches, merge, r, rows):
    x = r.x[rows, :]
    xn = _rmsnorm(x, r.gm[...]).astype(BF16)

    def inproj_jobs(b):
        w_ref = getattr(r, _BRANCH_CONSTS[b][0])
        z[b] = [None] * len(_PIECES[b])
        jobs, lo = [], 0
        for i, wd in enumerate(_PIECES[b]):
            def job(i=i, lo=lo, hi=lo + wd):
                z[b][i] = _dot(xn, w_ref[:, lo:hi])
            jobs.append(job)
            lo += wd
        return jobs

    def gate_jobs(j):
        half = D_MODEL // 2
        gates[j] = [None, None]
        jobs = []
        for i in range(2):
            def job(i=i, lo=j * D_MODEL + i * half):
                pre = _dot(xn, r.w_gates[:, lo:lo + half]) + r.merge_b[j:j + 1, i * half:(i + 1) * half]
                gates[j][i] = jax.nn.sigmoid(pre).astype(BF16)
            jobs.append(job)
        return jobs

    def proj_job(j, b):
        def job():
            bv = br[b] if b in br else getattr(r, "br_" + b)[rows, :]
            gt = jnp.concatenate(gates[j], axis=1).astype(F32)
            merged.append(gt * _dot(bv, getattr(r, "w_br%d" % j)[...]))
        return job

    z, gates, br, merged = {}, {}, {}, []
    for b in branches:
        for jb in inproj_jobs(b):
            jb()
        br[b] = _BRANCH_FN[b](g, z[b], r).astype(BF16)
    if not merge:
        for b in branches:
            getattr(r, "br_" + b + "_o")[rows, :] = br[b]
        return
    for j, b in enumerate(_BRANCHES):
        for jb in gate_jobs(j):
            jb()
        proj_job(j, b)()
    total = merged[0]
    for term in merged[1:]:
        total = total + term
    r.y[rows, :] = x + _dot(total.astype(BF16), r.w_out[...])


def _mlp_kernel(final, x_ref, gm_ref, wu_ref, wd_ref, gf_ref, y_ref):
    x = x_ref[...]
    xn = _rmsnorm(x, gm_ref[...]).astype(BF16)
    hdn = jnp.square(jnp.maximum(_dot(xn, wu_ref[...]), 0.0))
    y = x + _dot(hdn.astype(BF16), wd_ref[...])
    if final:
        y = _rmsnorm(y, gf_ref[...])
    y_ref[...] = y


def _const(shape):
    nd = len(shape)
    return pl.BlockSpec(shape, lambda bi, ti, _nd=nd: (0,) * _nd)


def _rows(g, C):
    return pl.BlockSpec((g.R * g.SUB, C), lambda bi, ti, _nt=g.NT: (bi * _nt + ti, 0))


def _layer_state(g, l, tail):
    nd = len(tail)
    return pl.BlockSpec((None, g.Bb) + tuple(tail), lambda bi, ti, _l=l, _nd=nd: (_l, bi) + (0,) * _nd)


def _call_mixer(g, l, x, W, states, prev_outs, branches, merge, br_in=None):
    n = g.B * g.T
    tails = _state_tails()
    consts, snames, br_names, outs, scratch = _mixer_plan(g, branches, merge)
    ins = [x] + [W[c] for c in consts] + [states[s] for s in snames] + [br_in[b[3:]] for b in br_names]
    in_specs = ([_rows(g, D_MODEL)] + [_const(W[c].shape) for c in consts]
                + [_layer_state(g, l, tails[s]) for s in snames] + [_rows(g, _BRANCH_WIDTH[b[3:]]) for b in br_names])
    n_main = 1 if merge else len(branches)
    aliases = {}
    if prev_outs is not None:
        for j, s in enumerate(snames):
            aliases[len(ins)] = n_main + j
            ins.append(prev_outs[s])
            in_specs.append(pl.BlockSpec(memory_space=pl.ANY))
    if merge:
        main_specs = [_rows(g, D_MODEL)]
        main_shapes = [jax.ShapeDtypeStruct((n, D_MODEL), F32)]
    else:
        main_specs = [_rows(g, _BRANCH_WIDTH[b]) for b in branches]
        main_shapes = [jax.ShapeDtypeStruct((n, _BRANCH_WIDTH[b]), BF16) for b in branches]
    out = pl.pallas_call(
        functools.partial(_mixer_kernel, g, branches, merge, len(aliases)),
        grid=(g.B // g.Bb, g.NT),
        in_specs=in_specs,
        out_specs=main_specs + [_layer_state(g, l, tails[s]) for s in snames],
        out_shape=main_shapes + [jax.ShapeDtypeStruct((DEPTH, g.B) + tails[s], F32) for s in snames],
        scratch_shapes=sum(scratch.values(), []),
        input_output_aliases=aliases,
        compiler_params=pltpu.CompilerParams(dimension_semantics=("arbitrary", "arbitrary"),
                                             vmem_limit_bytes=VMEM_LIMIT),
        name="mixer_" + "_".join(branches + (("merge",) if merge else ())),
    )(*ins)
    main = out[0] if merge else dict(zip(branches, out[:n_main]))
    return main, dict(zip(snames, out[n_main:]))


def _call_mlp(x, gm, wu, wd, gf, final):
    n = x.shape[0]
    consts = [gm, wu, wd, gf]
    tok = pl.BlockSpec((MLP_TILE, D_MODEL), lambda i: (i, 0))
    return pl.pallas_call(
        functools.partial(_mlp_kernel, final),
        grid=(n // MLP_TILE,),
        in_specs=[tok] + [pl.BlockSpec(c.shape, lambda i: (0, 0)) for c in consts],
        out_specs=tok,
        out_shape=jax.ShapeDtypeStruct((n, D_MODEL), F32),
        compiler_params=pltpu.CompilerParams(dimension_semantics=("arbitrary",), vmem_limit_bytes=VMEM_LIMIT),
        name="mlp",
    )(x, *consts)


def _cols(w, *names):
    return [w[:, _OFF[n][0]:_OFF[n][1]] for n in names]


def _pad_cols(w, width):
    return jnp.pad(w, [(0, 0)] * (w.ndim - 1) + [(0, width - w.shape[-1])])


def _block_diag(w):
    nb, d, e = w.shape
    return (jnp.eye(nb, dtype=w.dtype)[:, None, :, None] * w[:, :, None, :]).reshape(nb * d, nb * e)


def _row(v):
    return v.reshape(1, -1).astype(F32)


def _pad8(conv_state):
    return jnp.pad(conv_state, ((0, 0), (0, 0), (SUBLANES - (CONV_W - 1), 0), (0, 0)))


def _layer_weights(l, P, lbs):
    w = P["w_in"][l]
    W = {}
    W["gm"] = _row(P["norm_mix"][l])
    W["w_gla"] = jnp.concatenate(_cols(w, "g_q", "g_k", "g_v", "g_g") + [_pad_cols(_cols(w, "g_lr")[0], LANES)],
                                 axis=1).astype(BF16)
    W["wlr"] = jnp.pad(P["gla_w_lr"][l], ((0, LANES - GLA_RANK), (0, 0))).astype(BF16)
    W["blr"] = _row(P["gla_b_lr"][l])
    W["gla_norm"] = _row(P["gla_norm"][l])
    W["w_rg"] = jnp.concatenate(_cols(w, "r_x", "r_y"), axis=1).astype(BF16)
    W["rg_cw"] = P["rg_conv_w"][l].astype(F32)
    W["rg_cb"] = _row(P["rg_conv_b"][l])
    W["rg_wa"] = _block_diag(P["rg_wa"][l]).astype(BF16)
    W["rg_ba"] = _row(P["rg_ba"][l])
    W["rg_wx"] = _block_diag(P["rg_wx"][l]).astype(BF16)
    W["rg_bx"] = _row(P["rg_bx"][l])
    W["rg_lam"] = _row(P["rg_lam"][l])
    W["w_ml"] = jnp.concatenate(_cols(w, "m_q", "m_k", "m_v", "m_o")
                                + [_pad_cols(_cols(w, "m_i")[0], LANES), _pad_cols(_cols(w, "m_f")[0], LANES)],
                                axis=1).astype(BF16)
    W["ml_cw"] = P["ml_conv_w"][l].astype(F32)
    W["ml_cb"] = _row(P["ml_conv_b"][l])
    W["ml_bi"] = _pad_cols(_row(P["ml_b_i"][l]), LANES)
    W["ml_bf"] = _pad_cols(_row(P["ml_b_f"][l]), LANES)
    W["ml_norm"] = _row(P["ml_norm"][l])
    W["w_hg"] = jnp.concatenate(_cols(w, "h_q", "h_f", "h_i", "h_g"), axis=1).astype(BF16)
    W["hg_lb"] = _row(lbs[l])
    W["hg_norm"] = _row(P["hg_norm"][l])
    W["w_gates"] = _cols(w, "gates")[0].astype(BF16)
    W["merge_b"] = P["merge_b"][l].astype(F32)
    for j, nm in enumerate(("w_br_gla", "w_br_rg", "w_br_ml", "w_br_hg")):
        W["w_br%d" % j] = P[nm][l].astype(BF16)
    W["w_out"] = P["w_out"][l].astype(BF16)
    W["gf"] = _row(P["norm_ffn"][l])
    W["w_up"] = P["w_up"][l].astype(BF16)
    W["w_down"] = P["w_down"][l].astype(BF16)
    return W


def _ones_block_diag(dk, dv):
    hpg = MXU_DIM // dk
    return (np.arange(MXU_DIM)[:, None] // dk == np.arange(hpg * dv)[None, :] // dv).astype(np.float32)


def _geom_consts(g):
    t = np.arange(g.R)[:, None]
    s = np.arange(g.R)[None, :]
    keep = (s <= t) & (t // g.SEG == s // g.SEG)
    lvl = np.full((g.R, g.R), -1, np.int32)
    lvl[keep & (t // BAND == s // BAND)] = 0
    w, i = BAND, 1
    while 2 * w <= g.SEG:
        lvl[keep & (t // (2 * w) == s // (2 * w)) & (t // w != s // w)] = i
        w, i = 2 * w, i + 1
    return {"tri": jnp.asarray(keep, BF16), "lvl": jnp.asarray(lvl),
            "ones_gla": jnp.asarray(_ones_block_diag(GLA_DK, GLA_DV), BF16),
            "ones_hg": jnp.asarray(_ones_block_diag(HG_DK, HG_DV), BF16)}


def _trunk(x3, st, LW, norm_final):
    B, T, _ = x3.shape
    g = _geom(B, T)
    x = x3.reshape(B * T, D_MODEL)
    s_gla, h_rg, conv_rg, c_ml, n_ml, m_ml, conv_ml, s_hg = st
    states = {"s_gla": s_gla, "h_rg": h_rg[:, :, None, :], "conv_rg": _pad8(conv_rg), "c_ml": c_ml, "n_ml": n_ml,
              "m_ml": _pad_cols(m_ml, LANES)[:, :, None, :], "conv_ml": _pad8(conv_ml), "s_hg": s_hg}
    outs = None
    gc = _geom_consts(g)
    for l in range(DEPTH):
        W = {**LW[l], **gc}
        if g.NSEG == 1:
            x, outs = _call_mixer(g, l, x, W, states, outs, _BRANCHES, True)
        else:
            new_outs, br = {}, {}
            for b in _BRANCHES:
                res, st_b = _call_mixer(g, l, x, W, states, outs, (b,), False)
                br.update(res)
                new_outs.update(st_b)
            x, _ = _call_mixer(g, l, x, W, states, None, (), True, br)
            outs = new_outs
        x = _call_mlp(x, W["gf"], W["w_up"], W["w_down"], _row(norm_final), final=(l == DEPTH - 1))
    tail3 = SUBLANES - (CONV_W - 1)
    new = (outs["s_gla"], outs["h_rg"][:, :, 0, :], outs["conv_rg"][:, :, tail3:, :], outs["c_ml"], outs["n_ml"],
           outs["m_ml"][:, :, 0, :ML_HEADS], outs["conv_ml"][:, :, tail3:, :], outs["s_hg"])
    return x.reshape(B, T, D_MODEL), new


def _zero_states(B):
    return (jnp.zeros((DEPTH, B, GLA_HEADS, GLA_DK, GLA_DV), F32),
            jnp.zeros((DEPTH, B, RG_WIDTH), F32),
            jnp.zeros((DEPTH, B, CONV_W - 1, RG_WIDTH), F32),
            jnp.zeros((DEPTH, B, ML_HEADS, ML_DH, ML_DH), F32),
            jnp.zeros((DEPTH, B, ML_HEADS, ML_DH), F32),
            jnp.zeros((DEPTH, B, ML_HEADS), F32),
            jnp.zeros((DEPTH, B, CONV_W - 1, 2 * ML_W), F32),
            jnp.zeros((DEPTH, B, HG_HEADS, HG_DK, HG_DV), F32))


def kernel(x_prompt, x_sample, state_gla, state_rglru_h, state_rglru_conv, state_mlstm_C, state_mlstm_n, state_mlstm_m, state_mlstm_conv, state_hgrn, norm_mix, w_in, gla_w_lr, gla_b_lr, gla_norm, rg_conv_w, rg_conv_b, rg_wa, rg_ba, rg_wx, rg_bx, rg_lam, ml_conv_w, ml_conv_b, ml_b_i, ml_b_f, ml_norm, hg_gamma, hg_norm, merge_b, w_br_gla, w_br_rg, w_br_ml, w_br_hg, w_out, norm_ffn, w_up, w_down, norm_final):
    P = dict(norm_mix=norm_mix, w_in=w_in, gla_w_lr=gla_w_lr, gla_b_lr=gla_b_lr, gla_norm=gla_norm,
             rg_conv_w=rg_conv_w, rg_conv_b=rg_conv_b, rg_wa=rg_wa, rg_ba=rg_ba, rg_wx=rg_wx, rg_bx=rg_bx,
             rg_lam=rg_lam, ml_conv_w=ml_conv_w, ml_conv_b=ml_conv_b, ml_b_i=ml_b_i, ml_b_f=ml_b_f,
             ml_norm=ml_norm, hg_norm=hg_norm, merge_b=merge_b, w_br_gla=w_br_gla, w_br_rg=w_br_rg,
             w_br_ml=w_br_ml, w_br_hg=w_br_hg, w_out=w_out, norm_ffn=norm_ffn, w_up=w_up, w_down=w_down)
    sm = jax.nn.softmax(hg_gamma.astype(F32), axis=0)
    lbs = jnp.concatenate([jnp.zeros_like(sm[:1]), jnp.cumsum(sm, axis=0)[:-1]], axis=0)
    lbs = jnp.clip(lbs, 0.0, 1.0 - 1e-6)
    LW = [_layer_weights(l, P, lbs) for l in range(DEPTH)]

    y_prompt, p_st = _trunk(x_prompt, _zero_states(x_prompt.shape[0]), LW, norm_final)
    s_in = (state_gla, state_rglru_h, state_rglru_conv, state_mlstm_C, state_mlstm_n, state_mlstm_m,
            state_mlstm_conv, state_hgrn)
    y_sample, s_st = _trunk(x_sample, s_in, LW, norm_final)
    return (y_prompt, y_sample) + tuple(p_st) + tuple(s_st)
```

```python
import collections
import functools

import jax
import jax.numpy as jnp
import numpy as np
from jax import lax
from jax.experimental import pallas as pl
from jax.experimental.pallas import tpu as pltpu

F32 = jnp.float32
BF16 = jnp.bfloat16

D_MODEL = 1024
DEPTH = 2
GLA_HEADS, GLA_DK, GLA_DV, GLA_RANK = 4, 64, 128, 16
GLA_KW, GLA_W = GLA_HEADS * GLA_DK, GLA_HEADS * GLA_DV
GLA_GATE_TAU = 16.0
RG_WIDTH, RG_BLOCKS, RG_C = 512, 8, 8.0
CONV_W = 4
ML_HEADS, ML_DH = 4, 128
ML_W = ML_HEADS * ML_DH
HG_HEADS, HG_DK, HG_DV = 4, 128, 128
HG_W = HG_HEADS * HG_DV
N_BRANCH = 4
D_FF = 4 * D_MODEL
EPS = 1e-6
NEG = -1e30

LANES = 128
SUBLANES = 8
MXU_DIM = 256
ROW_TILE = 256
MLP_TILE = 1024
SUBTILES = 2
BAND = 4
LOG2E = 1.4426950408889634
SEG_UNROLL = 16
VMEM_LIMIT = 56 * 1024 * 1024

_OFF = {}
_o = 0
for _name, _size in (("g_q", GLA_KW), ("g_k", GLA_KW), ("g_v", GLA_W), ("g_lr", GLA_RANK), ("g_g", GLA_W),
                     ("r_x", RG_WIDTH), ("r_y", RG_WIDTH),
                     ("m_q", ML_W), ("m_k", ML_W), ("m_v", ML_W), ("m_o", ML_W), ("m_i", ML_HEADS), ("m_f", ML_HEADS),
                     ("h_q", HG_W), ("h_f", HG_W), ("h_i", HG_W), ("h_g", HG_W),
                     ("gates", N_BRANCH * D_MODEL)):
    _OFF[_name] = (_o, _o + _size)
    _o += _size

Geom = collections.namedtuple("Geom", "B T Bb Tt R SEG NSEG NT SUB")


def _geom(B, T):
    if T >= ROW_TILE:
        Bb, Tt = 1, ROW_TILE
        sub = SUBTILES if T % (ROW_TILE * SUBTILES) == 0 else 1
    else:
        Bb, Tt, sub = ROW_TILE // T, T, 1
    assert T % (Tt * sub) == 0 and B % Bb == 0 and Tt % SUBLANES == 0 and Tt >= SUBLANES
    return Geom(B, T, Bb, Tt, Bb * Tt, Tt, Bb, T // (Tt * sub), sub)


def _dot(a, b):
    return jnp.dot(a, b, preferred_element_type=F32)


def _dot_nt(a, b):
    return lax.dot_general(a, b, (((1,), (1,)), ((), ())), preferred_element_type=F32)


def _dot3(m01, x):
    hi = x.astype(BF16)
    r1 = x - hi.astype(F32)
    mid = r1.astype(BF16)
    lo = (r1 - mid.astype(F32)).astype(BF16)
    return _dot(m01, hi) + (_dot(m01, mid) + _dot(m01, lo))


def _iota(shape, dim):
    return lax.broadcasted_iota(jnp.int32, shape, dim)


def _div(x, n):
    assert n & (n - 1) == 0
    return x >> (n.bit_length() - 1)


def _log_sigmoid(x):
    return jnp.minimum(x, 0.0) - jnp.log(1.0 + jnp.exp(-jnp.abs(x)))


def _silu(x):
    h = 0.5 * x
    return h + h * jnp.tanh(h)


def _gelu_tanh(x):
    return 0.5 * x * (1.0 + jnp.tanh(0.7978845608028654 * (x + 0.044715 * (x * x * x))))


def _rmsnorm(x, g):
    return x * lax.rsqrt(jnp.mean(x * x, axis=-1, keepdims=True) + EPS) * g


def _head_rmsnorm(o, gn, H, d):
    outs = []
    for h in range(H):
        oh = o[:, h * d:(h + 1) * d]
        outs.append(oh * lax.rsqrt(jnp.mean(oh * oh, axis=-1, keepdims=True) + EPS) * gn[:, h * d:(h + 1) * d])
    return jnp.concatenate(outs, axis=1)


def _block_last(c, w):
    R, C = c.shape
    if w == R:
        return jnp.broadcast_to(c[R - 1:R, :], (R, C))
    c3 = c.reshape(R // w, w, C)
    return jnp.broadcast_to(c3[:, w - 1:w, :], (R // w, w, C)).reshape(R, C)


def _seg_rows(ref3, g):
    v = ref3[...]
    C = v.shape[-1]
    return jnp.broadcast_to(v, (g.NSEG, g.SEG, C)).reshape(g.R, C)


def _causal_conv(x, prev8, w_ref, b_ref, g):
    R, C = x.shape
    nb = R // SUBLANES
    sub = _iota((1, SUBLANES, 1), 1)
    x3 = x.reshape(nb, SUBLANES, C)
    p3 = prev8.reshape(nb, SUBLANES, C)
    y = b_ref[...] + x3 * w_ref[CONV_W - 1:CONV_W, :]
    for s in range(1, CONV_W):
        xs = pltpu.roll(jnp.where(sub >= SUBLANES - s, p3, x3), s, 1)
        y = y + xs * w_ref[CONV_W - 1 - s:CONV_W - s, :]
    return y.reshape(R, C)


def _conv_io(x, conv_in_ref, conv_out_ref, carry_ref, g):
    R = g.R
    if g.NSEG == 1:
        prev8 = jnp.concatenate([carry_ref[...], x[:R - SUBLANES, :]], axis=0)
        last8 = x[R - SUBLANES:, :]
        carry_ref[...] = last8
        conv_out_ref[0] = last8
    else:
        C = x.shape[-1]
        prev8 = conv_in_ref[...].reshape(R, C)
        conv_out_ref[...] = x.reshape(g.NSEG, g.SEG, C)
    return prev8


def _gl_core(q, k, v, la, s_ref, scr, g, H, dk, dv, tri, lvl, ones_bd):
    qd_s, kdT_s, v_s, o_s, decT_s = scr
    R, SEG, NSEG = g.R, g.SEG, g.NSEG
    HK = H * dk
    c = _dot3(tri, la)
    cl = _block_last(c, SEG)

    bw = BAND
    HV = H * dv
    c2 = c * LOG2E
    nb = R // SUBLANES
    sub = _iota((1, SUBLANES, 1), 1)
    sub_b = sub & (bw - 1)
    q3, k3, c3, v3 = (a.reshape(nb, SUBLANES, a.shape[1]) for a in (q, k, c2, v))
    o = None
    for d in range(bw):
        if d == 0:
            p3, vd = q3 * k3, v3
        else:
            diff = jnp.where(sub_b >= d, c3 - pltpu.roll(c3, d, 1), NEG)
            p3 = q3 * pltpu.roll(k3, d, 1) * jnp.exp2(diff)
            vd = pltpu.roll(v3, d, 1)
        pb = p3.reshape(R, HK).astype(BF16)
        parts = [_dot(pb[:, i * MXU_DIM:(i + 1) * MXU_DIM], ones_bd) for i in range(HK // MXU_DIM)]
        term = (parts[0] if len(parts) == 1 else jnp.concatenate(parts, axis=1)) * vd.reshape(R, HV)
        o = term if o is None else o + term

    vb = v.astype(BF16)
    w = bw
    a_off = [None] * H
    level = 0
    while 2 * w <= SEG:
        if w < SUBLANES:
            assert 2 * w == SUBLANES
            first = sub < w
            pmid = c3[:, w - 1:w, :]
            x = jnp.where(first, k3, q3) * jnp.exp2(jnp.where(first, pmid - c3, c3 - pmid))
            x = x.reshape(R, HK).astype(BF16)
        else:
            nb2 = R // (2 * w)
            c4 = c2.reshape(nb2, 2, w, HK)
            pmid = c4[:, 0:1, w - 1:w, :]
            xk = k.reshape(nb2, 2, w, HK)[:, 0:1] * jnp.exp2(pmid - c4[:, 0:1])
            xq = q.reshape(nb2, 2, w, HK)[:, 1:2] * jnp.exp2(c4[:, 1:2] - pmid)
            x = jnp.concatenate([xk, xq], axis=1).reshape(R, HK).astype(BF16)
        level += 1
        m = lvl == level
        for h in range(H):
            xh = x[:, h * dk:(h + 1) * dk]
            a = _dot_nt(xh, xh)
            a_off[h] = jnp.where(m, a, 0.0 if a_off[h] is None else a_off[h])
        w *= 2
    if a_off[0] is not None:
        o = o + jnp.concatenate([_dot(a_off[h].astype(BF16), vb[:, h * dv:(h + 1) * dv]) for h in range(H)], axis=1)

    qd_s[...] = q * jnp.exp(c)
    kdT_s[...] = (k * jnp.exp(cl - c)).T.astype(BF16)
    v_s[...] = vb
    decT_s[...] = jnp.exp(cl).T
    o_s[...] = o
    lane = _iota((1, R), 1)

    def seg_body(sg, carry):
        r0 = sg * SEG if isinstance(sg, int) else pl.multiple_of(sg * SEG, SEG)
        for h in range(H):
            S = s_ref[sg, h]
            qd = qd_s[pl.ds(r0, SEG), h * dk:(h + 1) * dk].astype(BF16)
            o_s[pl.ds(r0, SEG), h * dv:(h + 1) * dv] += _dot(qd, S.astype(BF16))
            kt = kdT_s[h * dk:(h + 1) * dk, :]
            if NSEG > 1:
                kt = jnp.where(_div(lane, SEG) == sg, kt, jnp.zeros_like(kt))
            U = _dot(kt, v_s[:, h * dv:(h + 1) * dv])
            dec = jnp.sum(jnp.where(lane == r0, decT_s[h * dk:(h + 1) * dk, :], 0.0), axis=1, keepdims=True)
            s_ref[sg, h] = dec * S + U
        return carry

    if NSEG == 1:
        seg_body(0, 0)
    else:
        lax.fori_loop(0, NSEG, seg_body, 0, unroll=SEG_UNROLL)
    return o_s[...]


def _gl_scratch(g, H, dk, dv):
    return [pltpu.VMEM((g.R, H * dk), F32), pltpu.VMEM((H * dk, g.R), BF16), pltpu.VMEM((g.R, H * dv), BF16),
            pltpu.VMEM((g.R, H * dv), F32), pltpu.VMEM((H * dk, g.R), F32)]


_PIECES = {"gla": (2 * GLA_KW, GLA_W, GLA_W, LANES), "rg": (RG_WIDTH, RG_WIDTH),
           "ml": (ML_W, ML_W, ML_W, ML_W, 2 * LANES), "hg": (HG_W, HG_W, HG_W, HG_W)}


def _gla_branch(g, z, r):
    qk, v, gate, lr = z
    q = qk[:, 0:GLA_KW] * (GLA_DK ** -0.5)
    k = qk[:, GLA_KW:]
    la = _log_sigmoid(_dot(lr.astype(BF16), r.wlr[...]) + r.blr[...]) * (1.0 / GLA_GATE_TAU)
    o = _gl_core(q, k, v, la, r.s_gla_o, r.gla_scr, g, GLA_HEADS, GLA_DK, GLA_DV, r.tri[...], r.lvl[...],
                 r.ones_gla[...])
    return _head_rmsnorm(o, r.gla_norm[...], GLA_HEADS, GLA_DV) * _silu(gate)


def _hgrn_branch(g, z, r):
    hq, fp, v, gate = z
    lb = r.hg_lb[...]
    ls = _log_sigmoid(fp)
    pos = lb > 0.0
    a_ = jnp.log(jnp.where(pos, lb, 1.0))
    b_ = jnp.log1p(-lb) + ls
    lae = jnp.maximum(a_, b_) + jnp.log(1.0 + jnp.exp(-jnp.abs(a_ - b_)))
    la = jnp.where(pos, lae, ls)
    k = (1.0 - lb) * jax.nn.sigmoid(-fp)
    q = _silu(hq)
    o = _gl_core(q, k, v, la, r.s_hg_o, r.hg_scr, g, HG_HEADS, HG_DK, HG_DV, r.tri[...], r.lvl[...],
                 r.ones_hg[...])
    return _head_rmsnorm(o, r.hg_norm[...], HG_HEADS, HG_DV) * _silu(gate)


def _rglru_branch(g, z, r):
    R, SEG = g.R, g.SEG
    rx, ry = z
    prev8 = _conv_io(rx, r.conv_rg, r.conv_rg_o, r.rg_carry, g)
    xc = _causal_conv(rx, prev8, r.rg_cw, r.rg_cb, g)
    xcb = xc.astype(BF16)
    rg = jax.nn.sigmoid(_dot(xcb, r.rg_wa[...]) + r.rg_ba[...])
    ig = jax.nn.sigmoid(_dot(xcb, r.rg_wx[...]) + r.rg_bx[...])
    log_a = RG_C * rg * _log_sigmoid(r.rg_lam[...])
    a = jnp.exp(log_a)
    u = jnp.sqrt(jnp.maximum(1.0 - a * a, 0.0)) * (ig * xc)
    nbs = SEG // SUBLANES
    sub = _iota((1, SUBLANES, 1), 1)
    a3 = a.reshape(R // SUBLANES, SUBLANES, RG_WIDTH)
    u3 = u.reshape(R // SUBLANES, SUBLANES, RG_WIDTH)
    d = 1
    while d < SUBLANES:
        ok = sub >= d
        a_s = jnp.where(ok, pltpu.roll(a3, d, 1), 1.0)
        u_s = jnp.where(ok, pltpu.roll(u3, d, 1), 0.0)
        u3 = a3 * u_s + u3
        a3 = a3 * a_s
        d *= 2
    a4 = a3.reshape(g.NSEG, nbs, SUBLANES, RG_WIDTH)
    u4 = u3.reshape(g.NSEG, nbs, SUBLANES, RG_WIDTH)
    hc = r.h_rg_o[...]
    blocks = []
    for j in range(nbs):
        hb = u4[:, j] + a4[:, j] * hc
        blocks.append(hb)
        hc = hb[:, SUBLANES - 1:SUBLANES, :]
    r.h_rg_o[...] = hc
    hcur = (blocks[0] if nbs == 1 else jnp.stack(blocks, axis=1)).reshape(R, RG_WIDTH)
    return hcur * _gelu_tanh(ry)


def _mlstm_branch(g, z, r):
    R, SEG, NSEG = g.R, g.SEG, g.NSEG
    H, dh = ML_HEADS, ML_DH
    q_s, wkT_s, wk_s, v_s, sc_s, qc_s, qn_s, hm_s = r.ml_scr
    cout_ref, nout_ref, mout_ref = r.c_ml_o, r.n_ml_o, r.m_ml_o
    zq, zk, mv, mo, zif = z
    qk_pre = jnp.concatenate([zq, zk], axis=1)
    ipre = zif[:, :LANES] + r.ml_bi[...]
    fpre = zif[:, LANES:] + r.ml_bf[...]
    prev8 = _conv_io(qk_pre, r.conv_ml, r.conv_ml_o, r.ml_carry, g)
    qk = _silu(_causal_conv(qk_pre, prev8, r.ml_cw, r.ml_cb, g))
    mq = qk[:, :ML_W]
    mk = qk[:, ML_W:] * (dh ** -0.5)

    keep = r.lvl[...] >= 0
    b = _dot3(r.tri[...], _log_sigmoid(fpre))
    m_rows = _seg_rows(mout_ref, g)
    prev = b + m_rows
    gT = (ipre - b).T
    bl = _block_last(b, SEG)
    wlog = bl - b + ipre
    wmax = jnp.broadcast_to(jnp.max(wlog.reshape(NSEG, SEG, LANES), axis=1, keepdims=True),
                            (NSEG, SEG, LANES)).reshape(R, LANES)
    m_new = jnp.maximum(bl + m_rows, wmax)
    wgt = jnp.exp(wlog - m_new)
    sc_s[...] = jnp.exp(bl + m_rows - m_new)
    mout_ref[...] = m_new.reshape(NSEG, SEG, LANES)[:, 0:1, :]

    q_s[...] = mq
    v_s[...] = mv.astype(BF16)
    wk = jnp.concatenate([wgt[:, h:h + 1] * mk[:, h * dh:(h + 1) * dh] for h in range(H)], axis=1)
    wk_s[...] = wk
    wkT_s[...] = wk.T.astype(BF16)
    lane = _iota((1, R), 1)

    for h in range(H):
        hs = slice(h * dh, (h + 1) * dh)
        ls = slice(h * LANES, (h + 1) * LANES)
        logd = jnp.where(keep, b[:, h:h + 1] + gT[h:h + 1, :], NEG)
        mt = jnp.maximum(prev[:, h:h + 1], jnp.max(logd, axis=1, keepdims=True))
        dm = jnp.where(keep, jnp.exp(logd - mt), 0.0)
        sp = jnp.exp(prev[:, h:h + 1] - mt)
        s_mat = _dot_nt(mq[:, hs].astype(BF16), mk[:, hs].astype(BF16)) * dm
        num = _dot(s_mat.astype(BF16), v_s[:, hs])
        den = jnp.sum(s_mat, axis=1, keepdims=True)

        def seg_body(sg, carry, h=h, hs=hs, ls=ls):
            r0 = sg * SEG if isinstance(sg, int) else pl.multiple_of(sg * SEG, SEG)
            C = cout_ref[sg, h]
            n = nout_ref[sg, pl.ds(h, 1), :]
            qb = q_s[pl.ds(r0, SEG), hs]
            qc_s[pl.ds(r0, SEG), hs] = _dot(qb.astype(BF16), C.astype(BF16))
            qn_s[pl.ds(r0, SEG), ls] = jnp.broadcast_to(jnp.sum(qb * n, axis=1, keepdims=True), (SEG, LANES))
            kt = wkT_s[hs, :]
            if NSEG > 1:
                kt = jnp.where(_div(lane, SEG) == sg, kt, jnp.zeros_like(kt))
            U = _dot(kt, v_s[:, hs])
            sc = sc_s[pl.ds(r0, 1), h:h + 1]
            cout_ref[sg, h] = sc * C + U
            nout_ref[sg, pl.ds(h, 1), :] = sc * n + jnp.sum(wk_s[pl.ds(r0, SEG), hs], axis=0, keepdims=True)
            return carry

        if NSEG == 1:
            seg_body(0, 0)
        else:
            lax.fori_loop(0, NSEG, seg_body, 0, unroll=SEG_UNROLL)
        num = num + sp * qc_s[:, hs]
        den = den + sp * qn_s[:, h * LANES:h * LANES + 1]
        hm_s[:, hs] = num / jnp.maximum(jnp.abs(den), jnp.exp(-mt))

    hm = jax.nn.sigmoid(mo) * hm_s[...]
    return _head_rmsnorm(hm, r.ml_norm[...], H, dh)


_BRANCHES = ("gla", "rg", "ml", "hg")
_BRANCH_FN = {"gla": _gla_branch, "rg": _rglru_branch, "ml": _mlstm_branch, "hg": _hgrn_branch}
_BRANCH_CONSTS = {
    "gla": ("w_gla", "wlr", "blr", "gla_norm", "ones_gla"),
    "rg": ("w_rg", "rg_cw", "rg_cb", "rg_wa", "rg_ba", "rg_wx", "rg_bx", "rg_lam"),
    "ml": ("w_ml", "ml_cw", "ml_cb", "ml_bi", "ml_bf", "ml_norm"),
    "hg": ("w_hg", "hg_lb", "hg_norm", "ones_hg"),
}
_MERGE_CONSTS = ("w_gates", "merge_b", "w_br0", "w_br1", "w_br2", "w_br3", "w_out")
_BRANCH_STATES = {"gla": ("s_gla",), "rg": ("h_rg", "conv_rg"), "ml": ("c_ml", "n_ml", "m_ml", "conv_ml"),
                  "hg": ("s_hg",)}
_CARRIED = ("s_gla", "h_rg", "c_ml", "n_ml", "m_ml", "s_hg")
_BRANCH_WIDTH = {"gla": GLA_W, "rg": RG_WIDTH, "ml": ML_W, "hg": HG_W}


def _state_tails():
    return {"s_gla": (GLA_HEADS, GLA_DK, GLA_DV), "h_rg": (1, RG_WIDTH), "conv_rg": (SUBLANES, RG_WIDTH),
            "c_ml": (ML_HEADS, ML_DH, ML_DH), "n_ml": (ML_HEADS, ML_DH), "m_ml": (1, LANES),
            "conv_ml": (SUBLANES, 2 * ML_W), "s_hg": (HG_HEADS, HG_DK, HG_DV)}


def _branch_scratch(g, b):
    R = g.R
    if b == "gla":
        return {"gla_scr": _gl_scratch(g, GLA_HEADS, GLA_DK, GLA_DV)}
    if b == "hg":
        return {"hg_scr": _gl_scratch(g, HG_HEADS, HG_DK, HG_DV)}
    if b == "rg":
        return {"rg_carry": [pltpu.VMEM((SUBLANES, RG_WIDTH), F32)]}
    return {"ml_carry": [pltpu.VMEM((SUBLANES, 2 * ML_W), F32)],
            "ml_scr": [pltpu.VMEM((R, ML_W), F32), pltpu.VMEM((ML_W, R), BF16), pltpu.VMEM((R, ML_W), F32),
                       pltpu.VMEM((R, ML_W), BF16), pltpu.VMEM((R, LANES), F32), pltpu.VMEM((R, ML_W), F32),
                       pltpu.VMEM((R, ML_HEADS * LANES), F32), pltpu.VMEM((R, ML_W), F32)]}


def _mixer_plan(g, branches, merge):
    consts = (("gm", "tri", "lvl") if branches else ("gm",)) + sum((_BRANCH_CONSTS[b] for b in branches), ())
    consts += _MERGE_CONSTS if merge else ()
    states = sum((_BRANCH_STATES[b] for b in branches), ())
    br_in = tuple("br_" + b for b in _BRANCHES if b not in branches) if merge else ()
    outs = (("y",) if merge else tuple("br_" + b + "_o" for b in branches)) + tuple(s + "_o" for s in states)
    scratch = {}
    for b in branches:
        scratch.update(_branch_scratch(g, b))
    return consts, states, br_in, outs, scratch


def _mixer_kernel(g, branches, merge, n_alias, *refs):
    consts, states, br_in, outs, scratch = _mixer_plan(g, branches, merge)
    names = ("x",) + consts + states + br_in
    r = dict(zip(names, refs[:len(names)]))
    pos = len(names) + n_alias
    r.update(zip(outs, refs[pos:pos + len(outs)]))
    pos += len(outs)
    for nm, shapes in scratch.items():
        grp = refs[pos:pos + len(shapes)]
        r[nm] = grp[0] if nm.endswith("carry") else grp
        pos += len(shapes)
    r = collections.namedtuple("Refs", r.keys())(**r)

    @pl.when(pl.program_id(1) == 0)
    def _():
        for nm in states:
            if nm in _CARRIED:
                getattr(r, nm + "_o")[...] = getattr(r, nm)[...]
        if g.NSEG == 1:
            for b, nm in (("rg", "conv_rg"), ("ml", "conv_ml")):
                if b in branches:
                    getattr(r, b + "_carry")[...] = getattr(r, nm)[0]

    for sub in range(g.SUB):
        _mixer_tile(g, branches, merge, r, slice(sub * g.R, (sub + 1) * g.R))


def _mixer_tile(g, branches, merge, r, rows):
    x = r.x[rows, :]
    xn = _rmsnorm(x, r.gm[...]).astype(BF16)

    def inproj_jobs(b):
        w_ref = getattr(r, _BRANCH_CONSTS[b][0])
        z[b] = [None] * len(_PIECES[b])
        jobs, lo = [], 0
        for i, wd in enumerate(_PIECES[b]):
            def job(i=i, lo=lo, hi=lo + wd):
                z[b][i] = _dot(xn, w_ref[:, lo:hi])
            jobs.append(job)
            lo += wd
        return jobs

    def gate_jobs(j):
        half = D_MODEL // 2
        gates[j] = [None, None]
        jobs = []
        for i in range(2):
            def job(i=i, lo=j * D_MODEL + i * half):
                pre = _dot(xn, r.w_gates[:, lo:lo + half]) + r.merge_b[j:j + 1, i * half:(i + 1) * half]
                gates[j][i] = jax.nn.sigmoid(pre).astype(BF16)
            jobs.append(job)
        return jobs

    def proj_job(j, b):
        def job():
            bv = br[b] if b in br else getattr(r, "br_" + b)[rows, :]
            gt = jnp.concatenate(gates[j], axis=1).astype(F32)
            merged.append(gt * _dot(bv, getattr(r, "w_br%d" % j)[...]))
        return job

    z, gates, br, merged = {}, {}, {}, []
    for b in branches:
        for jb in inproj_jobs(b):
            jb()
        br[b] = _BRANCH_FN[b](g, z[b], r).astype(BF16)
    if not merge:
        for b in branches:
            getattr(r, "br_" + b + "_o")[rows, :] = br[b]
        return
    for j, b in enumerate(_BRANCHES):
        for jb in gate_jobs(j):
            jb()
        proj_job(j, b)()
    total = merged[0]
    for term in merged[1:]:
        total = total + term
    r.y[rows, :] = x + _dot(total.astype(BF16), r.w_out[...])


def _mlp_kernel(final, x_ref, gm_ref, wu_ref, wd_ref, gf_ref, y_ref):
    x = x_ref[...]
    xn = _rmsnorm(x, gm_ref[...]).astype(BF16)
    hdn = jnp.square(jnp.maximum(_dot(xn, wu_ref[...]), 0.0))
    y = x + _dot(hdn.astype(BF16), wd_ref[...])
    if final:
        y = _rmsnorm(y, gf_ref[...])
    y_ref[...] = y


def _const(shape):
    nd = len(shape)
    return pl.BlockSpec(shape, lambda bi, ti, _nd=nd: (0,) * _nd)


def _rows(g, C):
    return pl.BlockSpec((g.R * g.SUB, C), lambda bi, ti, _nt=g.NT: (bi * _nt + ti, 0))


def _layer_state(g, l, tail):
    nd = len(tail)
    return pl.BlockSpec((None, g.Bb) + tuple(tail), lambda bi, ti, _l=l, _nd=nd: (_l, bi) + (0,) * _nd)


def _call_mixer(g, l, x, W, states, prev_outs, branches, merge, br_in=None):
    n = g.B * g.T
    tails = _state_tails()
    consts, snames, br_names, outs, scratch = _mixer_plan(g, branches, merge)
    ins = [x] + [W[c] for c in consts] + [states[s] for s in snames] + [br_in[b[3:]] for b in br_names]
    in_specs = ([_rows(g, D_MODEL)] + [_const(W[c].shape) for c in consts]
                + [_layer_state(g, l, tails[s]) for s in snames] + [_rows(g, _BRANCH_WIDTH[b[3:]]) for b in br_names])
    n_main = 1 if merge else len(branches)
    aliases = {}
    if prev_outs is not None:
        for j, s in enumerate(snames):
            aliases[len(ins)] = n_main + j
            ins.append(prev_outs[s])
            in_specs.append(pl.BlockSpec(memory_space=pl.ANY))
    if merge:
        main_specs = [_rows(g, D_MODEL)]
        main_shapes = [jax.ShapeDtypeStruct((n, D_MODEL), F32)]
    else:
        main_specs = [_rows(g, _BRANCH_WIDTH[b]) for b in branches]
        main_shapes = [jax.ShapeDtypeStruct((n, _BRANCH_WIDTH[b]), BF16) for b in branches]
    out = pl.pallas_call(
        functools.partial(_mixer_kernel, g, branches, merge, len(aliases)),
        grid=(g.B // g.Bb, g.NT),
        in_specs=in_specs,
        out_specs=main_specs + [_layer_state(g, l, tails[s]) for s in snames],
        out_shape=main_shapes + [jax.ShapeDtypeStruct((DEPTH, g.B) + tails[s], F32) for s in snames],
        scratch_shapes=sum(scratch.values(), []),
        input_output_aliases=aliases,
        compiler_params=pltpu.CompilerParams(dimension_semantics=("arbitrary", "arbitrary"),
                                             vmem_limit_bytes=VMEM_LIMIT),
        name="mixer_" + "_".join(branches + (("merge",) if merge else ())),
    )(*ins)
    main = out[0] if merge else dict(zip(branches, out[:n_main]))
    return main, dict(zip(snames, out[n_main:]))


def _call_mlp(x, gm, wu, wd, gf, final):
    n = x.shape[0]
    consts = [gm, wu, wd, gf]
    tok = pl.BlockSpec((MLP_TILE, D_MODEL), lambda i: (i, 0))
    return pl.pallas_call(
        functools.partial(_mlp_kernel, final),
        grid=(n // MLP_TILE,),
        in_specs=[tok] + [pl.BlockSpec(c.shape, lambda i: (0, 0)) for c in consts],
        out_specs=tok,
        out_shape=jax.ShapeDtypeStruct((n, D_MODEL), F32),
        compiler_params=pltpu.CompilerParams(dimension_semantics=("arbitrary",), vmem_limit_bytes=VMEM_LIMIT),
        name="mlp",
    )(x, *consts)


def _cols(w, *names):
    return [w[:, _OFF[n][0]:_OFF[n][1]] for n in names]


def _pad_cols(w, width):
    return jnp.pad(w, [(0, 0)] * (w.ndim - 1) + [(0, width - w.shape[-1])])


def _block_diag(w):
    nb, d, e = w.shape
    return (jnp.eye(nb, dtype=w.dtype)[:, None, :, None] * w[:, :, None, :]).reshape(nb * d, nb * e)


def _row(v):
    return v.reshape(1, -1).astype(F32)


def _pad8(conv_state):
    return jnp.pad(conv_state, ((0, 0), (0, 0), (SUBLANES - (CONV_W - 1), 0), (0, 0)))


def _layer_weights(l, P, lbs):
    w = P["w_in"][l]
    W = {}
    W["gm"] = _row(P["norm_mix"][l])
    W["w_gla"] = jnp.concatenate(_cols(w, "g_q", "g_k", "g_v", "g_g") + [_pad_cols(_cols(w, "g_lr")[0], LANES)],
                                 axis=1).astype(BF16)
    W["wlr"] = jnp.pad(P["gla_w_lr"][l], ((0, LANES - GLA_RANK), (0, 0))).astype(BF16)
    W["blr"] = _row(P["gla_b_lr"][l])
    W["gla_norm"] = _row(P["gla_norm"][l])
    W["w_rg"] = jnp.concatenate(_cols(w, "r_x", "r_y"), axis=1).astype(BF16)
    W["rg_cw"] = P["rg_conv_w"][l].astype(F32)
    W["rg_cb"] = _row(P["rg_conv_b"][l])
    W["rg_wa"] = _block_diag(P["rg_wa"][l]).astype(BF16)
    W["rg_ba"] = _row(P["rg_ba"][l])
    W["rg_wx"] = _block_diag(P["rg_wx"][l]).astype(BF16)
    W["rg_bx"] = _row(P["rg_bx"][l])
    W["rg_lam"] = _row(P["rg_lam"][l])
    W["w_ml"] = jnp.concatenate(_cols(w, "m_q", "m_k", "m_v", "m_o")
                                + [_pad_cols(_cols(w, "m_i")[0], LANES), _pad_cols(_cols(w, "m_f")[0], LANES)],
                                axis=1).astype(BF16)
    W["ml_cw"] = P["ml_conv_w"][l].astype(F32)
    W["ml_cb"] = _row(P["ml_conv_b"][l])
    W["ml_bi"] = _pad_cols(_row(P["ml_b_i"][l]), LANES)
    W["ml_bf"] = _pad_cols(_row(P["ml_b_f"][l]), LANES)
    W["ml_norm"] = _row(P["ml_norm"][l])
    W["w_hg"] = jnp.concatenate(_cols(w, "h_q", "h_f", "h_i", "h_g"), axis=1).astype(BF16)
    W["hg_lb"] = _row(lbs[l])
    W["hg_norm"] = _row(P["hg_norm"][l])
    W["w_gates"] = _cols(w, "gates")[0].astype(BF16)
    W["merge_b"] = P["merge_b"][l].astype(F32)
    for j, nm in enumerate(("w_br_gla", "w_br_rg", "w_br_ml", "w_br_hg")):
        W["w_br%d" % j] = P[nm][l].astype(BF16)
    W["w_out"] = P["w_out"][l].astype(BF16)
    W["gf"] = _row(P["norm_ffn"][l])
    W["w_up"] = P["w_up"][l].astype(BF16)
    W["w_down"] = P["w_down"][l].astype(BF16)
    return W


def _ones_block_diag(dk, dv):
    hpg = MXU_DIM // dk
    return (np.arange(MXU_DIM)[:, None] // dk == np.arange(hpg * dv)[None, :] // dv).astype(np.float32)


def _geom_consts(g):
    t = np.arange(g.R)[:, None]
    s = np.arange(g.R)[None, :]
    keep = (s <= t) & (t // g.SEG == s // g.SEG)
    lvl = np.full((g.R, g.R), -1, np.int32)
    lvl[keep & (t // BAND == s // BAND)] = 0
    w, i = BAND, 1
    while 2 * w <= g.SEG:
        lvl[keep & (t // (2 * w) == s // (2 * w)) & (t // w != s // w)] = i
        w, i = 2 * w, i + 1
    return {"tri": jnp.asarray(keep, BF16), "lvl": jnp.asarray(lvl),
            "ones_gla": jnp.asarray(_ones_block_diag(GLA_DK, GLA_DV), BF16),
            "ones_hg": jnp.asarray(_ones_block_diag(HG_DK, HG_DV), BF16)}


def _trunk(x3, st, LW, norm_final):
    B, T, _ = x3.shape
    g = _geom(B, T)
    x = x3.reshape(B * T, D_MODEL)
    s_gla, h_rg, conv_rg, c_ml, n_ml, m_ml, conv_ml, s_hg = st
    states = {"s_gla": s_gla, "h_rg": h_rg[:, :, None, :], "conv_rg": _pad8(conv_rg), "c_ml": c_ml, "n_ml": n_ml,
              "m_ml": _pad_cols(m_ml, LANES)[:, :, None, :], "conv_ml": _pad8(conv_ml), "s_hg": s_hg}
    outs = None
    gc = _geom_consts(g)
    for l in range(DEPTH):
        W = {**LW[l], **gc}
        if g.NSEG == 1:
            x, outs = _call_mixer(g, l, x, W, states, outs, _BRANCHES, True)
        else:
            new_outs, br = {}, {}
            for b in _BRANCHES:
                res, st_b = _call_mixer(g, l, x, W, states, outs, (b,), False)
                br.update(res)
                new_outs.update(st_b)
            x, _ = _call_mixer(g, l, x, W, states, None, (), True, br)
            outs = new_outs
        x = _call_mlp(x, W["gf"], W["w_up"], W["w_down"], _row(norm_final), final=(l == DEPTH - 1))
    tail3 = SUBLANES - (CONV_W - 1)
    new = (outs["s_gla"], outs["h_rg"][:, :, 0, :], outs["conv_rg"][:, :, tail3:, :], outs["c_ml"], outs["n_ml"],
           outs["m_ml"][:, :, 0, :ML_HEADS], outs["conv_ml"][:, :, tail3:, :], outs["s_hg"])
    return x.reshape(B, T, D_MODEL), new


def _zero_states(B):
    return (jnp.zeros((DEPTH, B, GLA_HEADS, GLA_DK, GLA_DV), F32),
            jnp.zeros((DEPTH, B, RG_WIDTH), F32),
            jnp.zeros((DEPTH, B, CONV_W - 1, RG_WIDTH), F32),
            jnp.zeros((DEPTH, B, ML_HEADS, ML_DH, ML_DH), F32),
            jnp.zeros((DEPTH, B, ML_HEADS, ML_DH), F32),
            jnp.zeros((DEPTH, B, ML_HEADS), F32),
            jnp.zeros((DEPTH, B, CONV_W - 1, 2 * ML_W), F32),
            jnp.zeros((DEPTH, B, HG_HEADS, HG_DK, HG_DV), F32))


def kernel(x_prompt, x_sample, state_gla, state_rglru_h, state_rglru_conv, state_mlstm_C, state_mlstm_n, state_mlstm_m, state_mlstm_conv, state_hgrn, norm_mix, w_in, gla_w_lr, gla_b_lr, gla_norm, rg_conv_w, rg_conv_b, rg_wa, rg_ba, rg_wx, rg_bx, rg_lam, ml_conv_w, ml_conv_b, ml_b_i, ml_b_f, ml_norm, hg_gamma, hg_norm, merge_b, w_br_gla, w_br_rg, w_br_ml, w_br_hg, w_out, norm_ffn, w_up, w_down, norm_final):
    P = dict(norm_mix=norm_mix, w_in=w_in, gla_w_lr=gla_w_lr, gla_b_lr=gla_b_lr, gla_norm=gla_norm,
             rg_conv_w=rg_conv_w, rg_conv_b=rg_conv_b, rg_wa=rg_wa, rg_ba=rg_ba, rg_wx=rg_wx, rg_bx=rg_bx,
             rg_lam=rg_lam, ml_conv_w=ml_conv_w, ml_conv_b=ml_conv_b, ml_b_i=ml_b_i, ml_b_f=ml_b_f,
             ml_norm=ml_norm, hg_norm=hg_norm, merge_b=merge_b, w_br_gla=w_br_gla, w_br_rg=w_br_rg,
             w_br_ml=w_br_ml, w_br_hg=w_br_hg, w_out=w_out, norm_ffn=norm_ffn, w_up=w_up, w_down=w_down)
    sm = jax.nn.softmax(hg_gamma.astype(F32), axis=0)
    lbs = jnp.concatenate([jnp.zeros_like(sm[:1]), jnp.cumsum(sm, axis=0)[:-1]], axis=0)
    lbs = jnp.clip(lbs, 0.0, 1.0 - 1e-6)
    LW = [_layer_weights(l, P, lbs) for l in range(DEPTH)]

    y_prompt, p_st = _trunk(x_prompt, _zero_states(x_prompt.shape[0]), LW, norm_final)
    s_in = (state_gla, state_rglru_h, state_rglru_conv, state_mlstm_C, state_mlstm_n, state_mlstm_m,
            state_mlstm_conv, state_hgrn)
    y_sample, s_st = _trunk(x_sample, s_in, LW, norm_final)
    return (y_prompt, y_sample) + tuple(p_st) + tuple(s_st)
```

```python
import collections
import functools

import jax
import jax.numpy as jnp
import numpy as np
from jax import lax
from jax.experimental import pallas as pl
from jax.experimental.pallas import tpu as pltpu

F32 = jnp.float32
BF16 = jnp.bfloat16

D_MODEL = 1024
DEPTH = 2
GLA_HEADS, GLA_DK, GLA_DV, GLA_RANK = 4, 64, 128, 16
GLA_KW, GLA_W = GLA_HEADS * GLA_DK, GLA_HEADS * GLA_DV
GLA_GATE_TAU = 16.0
RG_WIDTH, RG_BLOCKS, RG_C = 512, 8, 8.0
CONV_W = 4
ML_HEADS, ML_DH = 4, 128
ML_W = ML_HEADS * ML_DH
HG_HEADS, HG_DK, HG_DV = 4, 128, 128
HG_W = HG_HEADS * HG_DV
N_BRANCH = 4
D_FF = 4 * D_MODEL
EPS = 1e-6
NEG = -1e30

LANES = 128
SUBLANES = 8
MXU_DIM = 256
ROW_TILE = 256
MLP_TILE = 512
SUBTILES = 1
BAND = 4
LOG2E = 1.4426950408889634
SEG_UNROLL = 16
VMEM_LIMIT = 56 * 1024 * 1024

_OFF = {}
_o = 0
for _name, _size in (("g_q", GLA_KW), ("g_k", GLA_KW), ("g_v", GLA_W), ("g_lr", GLA_RANK), ("g_g", GLA_W),
                     ("r_x", RG_WIDTH), ("r_y", RG_WIDTH),
                     ("m_q", ML_W), ("m_k", ML_W), ("m_v", ML_W), ("m_o", ML_W), ("m_i", ML_HEADS), ("m_f", ML_HEADS),
                     ("h_q", HG_W), ("h_f", HG_W), ("h_i", HG_W), ("h_g", HG_W),
                     ("gates", N_BRANCH * D_MODEL)):
    _OFF[_name] = (_o, _o + _size)
    _o += _size

Geom = collections.namedtuple("Geom", "B T Bb Tt R SEG NSEG NT SUB")


def _geom(B, T):
    if T >= ROW_TILE:
        Bb, Tt = 1, ROW_TILE
        sub = SUBTILES if T % (ROW_TILE * SUBTILES) == 0 else 1
    else:
        Bb, Tt, sub = ROW_TILE // T, T, 1
    assert T % (Tt * sub) == 0 and B % Bb == 0 and Tt % SUBLANES == 0 and Tt >= SUBLANES
    return Geom(B, T, Bb, Tt, Bb * Tt, Tt, Bb, T // (Tt * sub), sub)


def _dot(a, b):
    return jnp.dot(a, b, preferred_element_type=F32)


def _dot_nt(a, b):
    return lax.dot_general(a, b, (((1,), (1,)), ((), ())), preferred_element_type=F32)


def _dot3(m01, x):
    hi = x.astype(BF16)
    r1 = x - hi.astype(F32)
    mid = r1.astype(BF16)
    lo = (r1 - mid.astype(F32)).astype(BF16)
    return _dot(m01, hi) + (_dot(m01, mid) + _dot(m01, lo))


def _iota(shape, dim):
    return lax.broadcasted_iota(jnp.int32, shape, dim)


def _div(x, n):
    assert n & (n - 1) == 0
    return x >> (n.bit_length() - 1)


def _log_sigmoid(x):
    return jnp.minimum(x, 0.0) - jnp.log(1.0 + jnp.exp(-jnp.abs(x)))


def _silu(x):
    h = 0.5 * x
    return h + h * jnp.tanh(h)


def _gelu_tanh(x):
    return 0.5 * x * (1.0 + jnp.tanh(0.7978845608028654 * (x + 0.044715 * (x * x * x))))


def _rmsnorm(x, g):
    return x * lax.rsqrt(jnp.mean(x * x, axis=-1, keepdims=True) + EPS) * g


def _head_rmsnorm(o, gn, H, d):
    outs = []
    for h in range(H):
        oh = o[:, h * d:(h + 1) * d]
        outs.append(oh * lax.rsqrt(jnp.mean(oh * oh, axis=-1, keepdims=True) + EPS) * gn[:, h * d:(h + 1) * d])
    return jnp.concatenate(outs, axis=1)


def _block_last(c, w):
    R, C = c.shape
    if w == R:
        return jnp.broadcast_to(c[R - 1:R, :], (R, C))
    c3 = c.reshape(R // w, w, C)
    return jnp.broadcast_to(c3[:, w - 1:w, :], (R // w, w, C)).reshape(R, C)


def _seg_rows(ref3, g):
    v = ref3[...]
    C = v.shape[-1]
    return jnp.broadcast_to(v, (g.NSEG, g.SEG, C)).reshape(g.R, C)


def _causal_conv(x, prev8, w_ref, b_ref, g):
    R, C = x.shape
    nb = R // SUBLANES
    sub = _iota((1, SUBLANES, 1), 1)
    x3 = x.reshape(nb, SUBLANES, C)
    p3 = prev8.reshape(nb, SUBLANES, C)
    y = b_ref[...] + x3 * w_ref[CONV_W - 1:CONV_W, :]
    for s in range(1, CONV_W):
        xs = pltpu.roll(jnp.where(sub >= SUBLANES - s, p3, x3), s, 1)
        y = y + xs * w_ref[CONV_W - 1 - s:CONV_W - s, :]
    return y.reshape(R, C)


def _conv_io(x, conv_in_ref, conv_out_ref, carry_ref, g):
    R = g.R
    if g.NSEG == 1:
        prev8 = jnp.concatenate([carry_ref[...], x[:R - SUBLANES, :]], axis=0)
        last8 = x[R - SUBLANES:, :]
        carry_ref[...] = last8
        conv_out_ref[0] = last8
    else:
        C = x.shape[-1]
        prev8 = conv_in_ref[...].reshape(R, C)
        conv_out_ref[...] = x.reshape(g.NSEG, g.SEG, C)
    return prev8


def _gl_core(q, k, v, la, s_ref, scr, g, H, dk, dv, tri, lvl, ones_bd):
    qd_s, kdT_s, v_s, o_s, decT_s = scr
    R, SEG, NSEG = g.R, g.SEG, g.NSEG
    HK = H * dk
    c = _dot3(tri, la)
    cl = _block_last(c, SEG)

    bw = BAND
    HV = H * dv
    c2 = c * LOG2E
    nb = R // SUBLANES
    sub = _iota((1, SUBLANES, 1), 1)
    sub_b = sub & (bw - 1)
    q3, k3, c3, v3 = (a.reshape(nb, SUBLANES, a.shape[1]) for a in (q, k, c2, v))
    o = None
    for d in range(bw):
        if d == 0:
            p3, vd = q3 * k3, v3
        else:
            diff = jnp.where(sub_b >= d, c3 - pltpu.roll(c3, d, 1), NEG)
            p3 = q3 * pltpu.roll(k3, d, 1) * jnp.exp2(diff)
            vd = pltpu.roll(v3, d, 1)
        pb = p3.reshape(R, HK).astype(BF16)
        parts = [_dot(pb[:, i * MXU_DIM:(i + 1) * MXU_DIM], ones_bd) for i in range(HK // MXU_DIM)]
        term = (parts[0] if len(parts) == 1 else jnp.concatenate(parts, axis=1)) * vd.reshape(R, HV)
        o = term if o is None else o + term

    vb = v.astype(BF16)
    w = bw
    a_off = [None] * H
    level = 0
    while 2 * w <= SEG:
        if w < SUBLANES:
            assert 2 * w == SUBLANES
            first = sub < w
            pmid = c3[:, w - 1:w, :]
            x = jnp.where(first, k3, q3) * jnp.exp2(jnp.where(first, pmid - c3, c3 - pmid))
            x = x.reshape(R, HK).astype(BF16)
        else:
            nb2 = R // (2 * w)
            c4 = c2.reshape(nb2, 2, w, HK)
            pmid = c4[:, 0:1, w - 1:w, :]
            xk = k.reshape(nb2, 2, w, HK)[:, 0:1] * jnp.exp2(pmid - c4[:, 0:1])
            xq = q.reshape(nb2, 2, w, HK)[:, 1:2] * jnp.exp2(c4[:, 1:2] - pmid)
            x = jnp.concatenate([xk, xq], axis=1).reshape(R, HK).astype(BF16)
        level += 1
        m = lvl == level
        for h in range(H):
            xh = x[:, h * dk:(h + 1) * dk]
            a = _dot_nt(xh, xh)
            a_off[h] = jnp.where(m, a, 0.0 if a_off[h] is None else a_off[h])
        w *= 2
    if a_off[0] is not None:
        o = o + jnp.concatenate([_dot(a_off[h].astype(BF16), vb[:, h * dv:(h + 1) * dv]) for h in range(H)], axis=1)

    qd_s[...] = q * jnp.exp(c)
    kdT_s[...] = (k * jnp.exp(cl - c)).T.astype(BF16)
    v_s[...] = vb
    decT_s[...] = jnp.exp(cl).T
    o_s[...] = o
    lane = _iota((1, R), 1)

    def seg_body(sg, carry):
        r0 = sg * SEG if isinstance(sg, int) else pl.multiple_of(sg * SEG, SEG)
        for h in range(H):
            S = s_ref[sg, h]
            qd = qd_s[pl.ds(r0, SEG), h * dk:(h + 1) * dk].astype(BF16)
            o_s[pl.ds(r0, SEG), h * dv:(h + 1) * dv] += _dot(qd, S.astype(BF16))
            kt = kdT_s[h * dk:(h + 1) * dk, :]
            if NSEG > 1:
                kt = jnp.where(_div(lane, SEG) == sg, kt, jnp.zeros_like(kt))
            U = _dot(kt, v_s[:, h * dv:(h + 1) * dv])
            dec = jnp.sum(jnp.where(lane == r0, decT_s[h * dk:(h + 1) * dk, :], 0.0), axis=1, keepdims=True)
            s_ref[sg, h] = dec * S + U
        return carry

    if NSEG == 1:
        seg_body(0, 0)
    else:
        lax.fori_loop(0, NSEG, seg_body, 0, unroll=SEG_UNROLL)
    return o_s[...]


def _gl_scratch(g, H, dk, dv):
    return [pltpu.VMEM((g.R, H * dk), F32), pltpu.VMEM((H * dk, g.R), BF16), pltpu.VMEM((g.R, H * dv), BF16),
            pltpu.VMEM((g.R, H * dv), F32), pltpu.VMEM((H * dk, g.R), F32)]


_PIECES = {"gla": (2 * GLA_KW, GLA_W, GLA_W, LANES), "rg": (RG_WIDTH, RG_WIDTH),
           "ml": (ML_W, ML_W, ML_W, ML_W, 2 * LANES), "hg": (HG_W, HG_W, HG_W, HG_W)}


def _gla_branch(g, z, r):
    qk, v, gate, lr = z
    q = qk[:, 0:GLA_KW] * (GLA_DK ** -0.5)
    k = qk[:, GLA_KW:]
    la = _log_sigmoid(_dot(lr.astype(BF16), r.wlr[...]) + r.blr[...]) * (1.0 / GLA_GATE_TAU)
    o = _gl_core(q, k, v, la, r.s_gla_o, r.gla_scr, g, GLA_HEADS, GLA_DK, GLA_DV, r.tri[...], r.lvl[...],
                 r.ones_gla[...])
    return _head_rmsnorm(o, r.gla_norm[...], GLA_HEADS, GLA_DV) * _silu(gate)


def _hgrn_branch(g, z, r):
    hq, fp, v, gate = z
    lb = r.hg_lb[...]
    ls = _log_sigmoid(fp)
    pos = lb > 0.0
    a_ = jnp.log(jnp.where(pos, lb, 1.0))
    b_ = jnp.log1p(-lb) + ls
    lae = jnp.maximum(a_, b_) + jnp.log(1.0 + jnp.exp(-jnp.abs(a_ - b_)))
    la = jnp.where(pos, lae, ls)
    k = (1.0 - lb) * jax.nn.sigmoid(-fp)
    q = _silu(hq)
    o = _gl_core(q, k, v, la, r.s_hg_o, r.hg_scr, g, HG_HEADS, HG_DK, HG_DV, r.tri[...], r.lvl[...],
                 r.ones_hg[...])
    return _head_rmsnorm(o, r.hg_norm[...], HG_HEADS, HG_DV) * _silu(gate)


def _rglru_branch(g, z, r):
    R, SEG = g.R, g.SEG
    rx, ry = z
    prev8 = _conv_io(rx, r.conv_rg, r.conv_rg_o, r.rg_carry, g)
    xc = _causal_conv(rx, prev8, r.rg_cw, r.rg_cb, g)
    xcb = xc.astype(BF16)
    rg = jax.nn.sigmoid(_dot(xcb, r.rg_wa[...]) + r.rg_ba[...])
    ig = jax.nn.sigmoid(_dot(xcb, r.rg_wx[...]) + r.rg_bx[...])
    log_a = RG_C * rg * _log_sigmoid(r.rg_lam[...])
    a = jnp.exp(log_a)
    u = jnp.sqrt(jnp.maximum(1.0 - a * a, 0.0)) * (ig * xc)
    nbs = SEG // SUBLANES
    sub = _iota((1, SUBLANES, 1), 1)
    a3 = a.reshape(R // SUBLANES, SUBLANES, RG_WIDTH)
    u3 = u.reshape(R // SUBLANES, SUBLANES, RG_WIDTH)
    d = 1
    while d < SUBLANES:
        ok = sub >= d
        a_s = jnp.where(ok, pltpu.roll(a3, d, 1), 1.0)
        u_s = jnp.where(ok, pltpu.roll(u3, d, 1), 0.0)
        u3 = a3 * u_s + u3
        a3 = a3 * a_s
        d *= 2
    a4 = a3.reshape(g.NSEG, nbs, SUBLANES, RG_WIDTH)
    u4 = u3.reshape(g.NSEG, nbs, SUBLANES, RG_WIDTH)
    hc = r.h_rg_o[...]
    blocks = []
    for j in range(nbs):
        hb = u4[:, j] + a4[:, j] * hc
        blocks.append(hb)
        hc = hb[:, SUBLANES - 1:SUBLANES, :]
    r.h_rg_o[...] = hc
    hcur = (blocks[0] if nbs == 1 else jnp.stack(blocks, axis=1)).reshape(R, RG_WIDTH)
    return hcur * _gelu_tanh(ry)


def _mlstm_branch(g, z, r):
    R, SEG, NSEG = g.R, g.SEG, g.NSEG
    H, dh = ML_HEADS, ML_DH
    q_s, wkT_s, wk_s, v_s, sc_s, qc_s, qn_s, hm_s = r.ml_scr
    cout_ref, nout_ref, mout_ref = r.c_ml_o, r.n_ml_o, r.m_ml_o
    zq, zk, mv, mo, zif = z
    qk_pre = jnp.concatenate([zq, zk], axis=1)
    ipre = zif[:, :LANES] + r.ml_bi[...]
    fpre = zif[:, LANES:] + r.ml_bf[...]
    prev8 = _conv_io(qk_pre, r.conv_ml, r.conv_ml_o, r.ml_carry, g)
    qk = _silu(_causal_conv(qk_pre, prev8, r.ml_cw, r.ml_cb, g))
    mq = qk[:, :ML_W]
    mk = qk[:, ML_W:] * (dh ** -0.5)

    keep = r.lvl[...] >= 0
    b = _dot3(r.tri[...], _log_sigmoid(fpre))
    m_rows = _seg_rows(mout_ref, g)
    prev = b + m_rows
    gT = (ipre - b).T
    bl = _block_last(b, SEG)
    wlog = bl - b + ipre
    wmax = jnp.broadcast_to(jnp.max(wlog.reshape(NSEG, SEG, LANES), axis=1, keepdims=True),
                            (NSEG, SEG, LANES)).reshape(R, LANES)
    m_new = jnp.maximum(bl + m_rows, wmax)
    wgt = jnp.exp(wlog - m_new)
    sc_s[...] = jnp.exp(bl + m_rows - m_new)
    mout_ref[...] = m_new.reshape(NSEG, SEG, LANES)[:, 0:1, :]

    q_s[...] = mq
    v_s[...] = mv.astype(BF16)
    wk = jnp.concatenate([wgt[:, h:h + 1] * mk[:, h * dh:(h + 1) * dh] for h in range(H)], axis=1)
    wk_s[...] = wk
    wkT_s[...] = wk.T.astype(BF16)
    lane = _iota((1, R), 1)

    for h in range(H):
        hs = slice(h * dh, (h + 1) * dh)
        ls = slice(h * LANES, (h + 1) * LANES)
        logd = jnp.where(keep, b[:, h:h + 1] + gT[h:h + 1, :], NEG)
        mt = jnp.maximum(prev[:, h:h + 1], jnp.max(logd, axis=1, keepdims=True))
        dm = jnp.where(keep, jnp.exp(logd - mt), 0.0)
        sp = jnp.exp(prev[:, h:h + 1] - mt)
        s_mat = _dot_nt(mq[:, hs].astype(BF16), mk[:, hs].astype(BF16)) * dm
        num = _dot(s_mat.astype(BF16), v_s[:, hs])
        den = jnp.sum(s_mat, axis=1, keepdims=True)

        def seg_body(sg, carry, h=h, hs=hs, ls=ls):
            r0 = sg * SEG if isinstance(sg, int) else pl.multiple_of(sg * SEG, SEG)
            C = cout_ref[sg, h]
            n = nout_ref[sg, pl.ds(h, 1), :]
            qb = q_s[pl.ds(r0, SEG), hs]
            qc_s[pl.ds(r0, SEG), hs] = _dot(qb.astype(BF16), C.astype(BF16))
            qn_s[pl.ds(r0, SEG), ls] = jnp.broadcast_to(jnp.sum(qb * n, axis=1, keepdims=True), (SEG, LANES))
            kt = wkT_s[hs, :]
            if NSEG > 1:
                kt = jnp.where(_div(lane, SEG) == sg, kt, jnp.zeros_like(kt))
            U = _dot(kt, v_s[:, hs])
            sc = sc_s[pl.ds(r0, 1), h:h + 1]
            cout_ref[sg, h] = sc * C + U
            nout_ref[sg, pl.ds(h, 1), :] = sc * n + jnp.sum(wk_s[pl.ds(r0, SEG), hs], axis=0, keepdims=True)
            return carry

        if NSEG == 1:
            seg_body(0, 0)
        else:
            lax.fori_loop(0, NSEG, seg_body, 0, unroll=SEG_UNROLL)
        num = num + sp * qc_s[:, hs]
        den = den + sp * qn_s[:, h * LANES:h * LANES + 1]
        hm_s[:, hs] = num / jnp.maximum(jnp.abs(den), jnp.exp(-mt))

    hm = jax.nn.sigmoid(mo) * hm_s[...]
    return _head_rmsnorm(hm, r.ml_norm[...], H, dh)


_BRANCHES = ("gla", "rg", "ml", "hg")
_BRANCH_FN = {"gla": _gla_branch, "rg": _rglru_branch, "ml": _mlstm_branch, "hg": _hgrn_branch}
_BRANCH_CONSTS = {
    "gla": ("w_gla", "wlr", "blr", "gla_norm", "ones_gla"),
    "rg": ("w_rg", "rg_cw", "rg_cb", "rg_wa", "rg_ba", "rg_wx", "rg_bx", "rg_lam"),
    "ml": ("w_ml", "ml_cw", "ml_cb", "ml_bi", "ml_bf", "ml_norm"),
    "hg": ("w_hg", "hg_lb", "hg_norm", "ones_hg"),
}
_MERGE_CONSTS = ("w_gates", "merge_b", "w_br0", "w_br1", "w_br2", "w_br3", "w_out")
_BRANCH_STATES = {"gla": ("s_gla",), "rg": ("h_rg", "conv_rg"), "ml": ("c_ml", "n_ml", "m_ml", "conv_ml"),
                  "hg": ("s_hg",)}
_CARRIED = ("s_gla", "h_rg", "c_ml", "n_ml", "m_ml", "s_hg")
_BRANCH_WIDTH = {"gla": GLA_W, "rg": RG_WIDTH, "ml": ML_W, "hg": HG_W}


def _state_tails():
    return {"s_gla": (GLA_HEADS, GLA_DK, GLA_DV), "h_rg": (1, RG_WIDTH), "conv_rg": (SUBLANES, RG_WIDTH),
            "c_ml": (ML_HEADS, ML_DH, ML_DH), "n_ml": (ML_HEADS, ML_DH), "m_ml": (1, LANES),
            "conv_ml": (SUBLANES, 2 * ML_W), "s_hg": (HG_HEADS, HG_DK, HG_DV)}


def _branch_scratch(g, b):
    R = g.R
    if b == "gla":
        return {"gla_scr": _gl_scratch(g, GLA_HEADS, GLA_DK, GLA_DV)}
    if b == "hg":
        return {"hg_scr": _gl_scratch(g, HG_HEADS, HG_DK, HG_DV)}
    if b == "rg":
        return {"rg_carry": [pltpu.VMEM((SUBLANES, RG_WIDTH), F32)]}
    return {"ml_carry": [pltpu.VMEM((SUBLANES, 2 * ML_W), F32)],
            "ml_scr": [pltpu.VMEM((R, ML_W), F32), pltpu.VMEM((ML_W, R), BF16), pltpu.VMEM((R, ML_W), F32),
                       pltpu.VMEM((R, ML_W), BF16), pltpu.VMEM((R, LANES), F32), pltpu.VMEM((R, ML_W), F32),
                       pltpu.VMEM((R, ML_HEADS * LANES), F32), pltpu.VMEM((R, ML_W), F32)]}


def _mixer_plan(g, branches, merge):
    consts = (("gm", "tri", "lvl") if branches else ("gm",)) + sum((_BRANCH_CONSTS[b] for b in branches), ())
    consts += _MERGE_CONSTS if merge else ()
    states = sum((_BRANCH_STATES[b] for b in branches), ())
    br_in = tuple("br_" + b for b in _BRANCHES if b not in branches) if merge else ()
    outs = (("y",) if merge else tuple("br_" + b + "_o" for b in branches)) + tuple(s + "_o" for s in states)
    scratch = {}
    for b in branches:
        scratch.update(_branch_scratch(g, b))
    return consts, states, br_in, outs, scratch


def _mixer_kernel(g, branches, merge, n_alias, *refs):
    consts, states, br_in, outs, scratch = _mixer_plan(g, branches, merge)
    names = ("x",) + consts + states + br_in
    r = dict(zip(names, refs[:len(names)]))
    pos = len(names) + n_alias
    r.update(zip(outs, refs[pos:pos + len(outs)]))
    pos += len(outs)
    for nm, shapes in scratch.items():
        grp = refs[pos:pos + len(shapes)]
        r[nm] = grp[0] if nm.endswith("carry") else grp
        pos += len(shapes)
    r = collections.namedtuple("Refs", r.keys())(**r)

    @pl.when(pl.program_id(1) == 0)
    def _():
        for nm in states:
            if nm in _CARRIED:
                getattr(r, nm + "_o")[...] = getattr(r, nm)[...]
        if g.NSEG == 1:
            for b, nm in (("rg", "conv_rg"), ("ml", "conv_ml")):
                if b in branches:
                    getattr(r, b + "_carry")[...] = getattr(r, nm)[0]

    for sub in range(g.SUB):
        _mixer_tile(g, branches, merge, r, slice(sub * g.R, (sub + 1) * g.R))


def _mixer_tile(g, branches, merge, r, rows):
    x = r.x[rows, :]
    xn = _rmsnorm(x, r.gm[...]).astype(BF16)

    def inproj_jobs(b):
        w_ref = getattr(r, _BRANCH_CONSTS[b][0])
        z[b] = [None] * len(_PIECES[b])
        jobs, lo = [], 0
        for i, wd in enumerate(_PIECES[b]):
            def job(i=i, lo=lo, hi=lo + wd):
                z[b][i] = _dot(xn, w_ref[:, lo:hi])
            jobs.append(job)
            lo += wd
        return jobs

    def gate_jobs(j):
        half = D_MODEL // 2
        gates[j] = [None, None]
        jobs = []
        for i in range(2):
            def job(i=i, lo=j * D_MODEL + i * half):
                pre = _dot(xn, r.w_gates[:, lo:lo + half]) + r.merge_b[j:j + 1, i * half:(i + 1) * half]
                gates[j][i] = jax.nn.sigmoid(pre).astype(BF16)
            jobs.append(job)
        return jobs

    def proj_job(j, b):
        def job():
            bv = br[b] if b in br else getattr(r, "br_" + b)[rows, :]
            gt = jnp.concatenate(gates[j], axis=1).astype(F32)
            merged.append(gt * _dot(bv, getattr(r, "w_br%d" % j)[...]))
        return job

    z, gates, br, merged = {}, {}, {}, []
    for b in branches:
        for jb in inproj_jobs(b):
            jb()
        br[b] = _BRANCH_FN[b](g, z[b], r).astype(BF16)
    if not merge:
        for b in branches:
            getattr(r, "br_" + b + "_o")[rows, :] = br[b]
        return
    for j, b in enumerate(_BRANCHES):
        for jb in gate_jobs(j):
            jb()
        proj_job(j, b)()
    total = merged[0]
    for term in merged[1:]:
        total = total + term
    r.y[rows, :] = x + _dot(total.astype(BF16), r.w_out[...])


def _mlp_kernel(final, x_ref, gm_ref, wu_ref, wd_ref, gf_ref, y_ref):
    x = x_ref[...]
    xn = _rmsnorm(x, gm_ref[...]).astype(BF16)
    hdn = jnp.square(jnp.maximum(_dot(xn, wu_ref[...]), 0.0))
    y = x + _dot(hdn.astype(BF16), wd_ref[...])
    if final:
        y = _rmsnorm(y, gf_ref[...])
    y_ref[...] = y


def _const(shape):
    nd = len(shape)
    return pl.BlockSpec(shape, lambda bi, ti, _nd=nd: (0,) * _nd)


def _rows(g, C):
    return pl.BlockSpec((g.R * g.SUB, C), lambda bi, ti, _nt=g.NT: (bi * _nt + ti, 0))


def _layer_state(g, l, tail):
    nd = len(tail)
    return pl.BlockSpec((None, g.Bb) + tuple(tail), lambda bi, ti, _l=l, _nd=nd: (_l, bi) + (0,) * _nd)


def _call_mixer(g, l, x, W, states, prev_outs, branches, merge, br_in=None):
    n = g.B * g.T
    tails = _state_tails()
    consts, snames, br_names, outs, scratch = _mixer_plan(g, branches, merge)
    ins = [x] + [W[c] for c in consts] + [states[s] for s in snames] + [br_in[b[3:]] for b in br_names]
    in_specs = ([_rows(g, D_MODEL)] + [_const(W[c].shape) for c in consts]
                + [_layer_state(g, l, tails[s]) for s in snames] + [_rows(g, _BRANCH_WIDTH[b[3:]]) for b in br_names])
    n_main = 1 if merge else len(branches)
    aliases = {}
    if prev_outs is not None:
        for j, s in enumerate(snames):
            aliases[len(ins)] = n_main + j
            ins.append(prev_outs[s])
            in_specs.append(pl.BlockSpec(memory_space=pl.ANY))
    if merge:
        main_specs = [_rows(g, D_MODEL)]
        main_shapes = [jax.ShapeDtypeStruct((n, D_MODEL), F32)]
    else:
        main_specs = [_rows(g, _BRANCH_WIDTH[b]) for b in branches]
        main_shapes = [jax.ShapeDtypeStruct((n, _BRANCH_WIDTH[b]), BF16) for b in branches]
    out = pl.pallas_call(
        functools.partial(_mixer_kernel, g, branches, merge, len(aliases)),
        grid=(g.B // g.Bb, g.NT),
        in_specs=in_specs,
        out_specs=main_specs + [_layer_state(g, l, tails[s]) for s in snames],
        out_shape=main_shapes + [jax.ShapeDtypeStruct((DEPTH, g.B) + tails[s], F32) for s in snames],
        scratch_shapes=sum(scratch.values(), []),
        input_output_aliases=aliases,
        compiler_params=pltpu.CompilerParams(dimension_semantics=("arbitrary", "arbitrary"),
                                             vmem_limit_bytes=VMEM_LIMIT),
        name="mixer_" + "_".join(branches + (("merge",) if merge else ())),
    )(*ins)
    main = out[0] if merge else dict(zip(branches, out[:n_main]))
    return main, dict(zip(snames, out[n_main:]))


def _call_mlp(x, gm, wu, wd, gf, final):
    n = x.shape[0]
    consts = [gm, wu, wd, gf]
    tok = pl.BlockSpec((MLP_TILE, D_MODEL), lambda i: (i, 0))
    return pl.pallas_call(
        functools.partial(_mlp_kernel, final),
        grid=(n // MLP_TILE,),
        in_specs=[tok] + [pl.BlockSpec(c.shape, lambda i: (0, 0)) for c in consts],
        out_specs=tok,
        out_shape=jax.ShapeDtypeStruct((n, D_MODEL), F32),
        compiler_params=pltpu.CompilerParams(dimension_semantics=("arbitrary",), vmem_limit_bytes=VMEM_LIMIT),
        name="mlp",
    )(x, *consts)


def _cols(w, *names):
    return [w[:, _OFF[n][0]:_OFF[n][1]] for n in names]


def _pad_cols(w, width):
    return jnp.pad(w, [(0, 0)] * (w.ndim - 1) + [(0, width - w.shape[-1])])


def _block_diag(w):
    nb, d, e = w.shape
    return (jnp.eye(nb, dtype=w.dtype)[:, None, :, None] * w[:, :, None, :]).reshape(nb * d, nb * e)


def _row(v):
    return v.reshape(1, -1).astype(F32)


def _pad8(conv_state):
    return jnp.pad(conv_state, ((0, 0), (0, 0), (SUBLANES - (CONV_W - 1), 0), (0, 0)))


def _layer_weights(l, P, lbs):
    w = P["w_in"][l]
    W = {}
    W["gm"] = _row(P["norm_mix"][l])
    W["w_gla"] = jnp.concatenate(_cols(w, "g_q", "g_k", "g_v", "g_g") + [_pad_cols(_cols(w, "g_lr")[0], LANES)],
                                 axis=1).astype(BF16)
    W["wlr"] = jnp.pad(P["gla_w_lr"][l], ((0, LANES - GLA_RANK), (0, 0))).astype(BF16)
    W["blr"] = _row(P["gla_b_lr"][l])
    W["gla_norm"] = _row(P["gla_norm"][l])
    W["w_rg"] = jnp.concatenate(_cols(w, "r_x", "r_y"), axis=1).astype(BF16)
    W["rg_cw"] = P["rg_conv_w"][l].astype(F32)
    W["rg_cb"] = _row(P["rg_conv_b"][l])
    W["rg_wa"] = _block_diag(P["rg_wa"][l]).astype(BF16)
    W["rg_ba"] = _row(P["rg_ba"][l])
    W["rg_wx"] = _block_diag(P["rg_wx"][l]).astype(BF16)
    W["rg_bx"] = _row(P["rg_bx"][l])
    W["rg_lam"] = _row(P["rg_lam"][l])
    W["w_ml"] = jnp.concatenate(_cols(w, "m_q", "m_k", "m_v", "m_o")
                                + [_pad_cols(_cols(w, "m_i")[0], LANES), _pad_cols(_cols(w, "m_f")[0], LANES)],
                                axis=1).astype(BF16)
    W["ml_cw"] = P["ml_conv_w"][l].astype(F32)
    W["ml_cb"] = _row(P["ml_conv_b"][l])
    W["ml_bi"] = _pad_cols(_row(P["ml_b_i"][l]), LANES)
    W["ml_bf"] = _pad_cols(_row(P["ml_b_f"][l]), LANES)
    W["ml_norm"] = _row(P["ml_norm"][l])
    W["w_hg"] = jnp.concatenate(_cols(w, "h_q", "h_f", "h_i", "h_g"), axis=1).astype(BF16)
    W["hg_lb"] = _row(lbs[l])
    W["hg_norm"] = _row(P["hg_norm"][l])
    W["w_gates"] = _cols(w, "gates")[0].astype(BF16)
    W["merge_b"] = P["merge_b"][l].astype(F32)
    for j, nm in enumerate(("w_br_gla", "w_br_rg", "w_br_ml", "w_br_hg")):
        W["w_br%d" % j] = P[nm][l].astype(BF16)
    W["w_out"] = P["w_out"][l].astype(BF16)
    W["gf"] = _row(P["norm_ffn"][l])
    W["w_up"] = P["w_up"][l].astype(BF16)
    W["w_down"] = P["w_down"][l].astype(BF16)
    return W


def _ones_block_diag(dk, dv):
    hpg = MXU_DIM // dk
    return (np.arange(MXU_DIM)[:, None] // dk == np.arange(hpg * dv)[None, :] // dv).astype(np.float32)


def _geom_consts(g):
    t = np.arange(g.R)[:, None]
    s = np.arange(g.R)[None, :]
    keep = (s <= t) & (t // g.SEG == s // g.SEG)
    lvl = np.full((g.R, g.R), -1, np.int32)
    lvl[keep & (t // BAND == s // BAND)] = 0
    w, i = BAND, 1
    while 2 * w <= g.SEG:
        lvl[keep & (t // (2 * w) == s // (2 * w)) & (t // w != s // w)] = i
        w, i = 2 * w, i + 1
    return {"tri": jnp.asarray(keep, BF16), "lvl": jnp.asarray(lvl),
            "ones_gla": jnp.asarray(_ones_block_diag(GLA_DK, GLA_DV), BF16),
            "ones_hg": jnp.asarray(_ones_block_diag(HG_DK, HG_DV), BF16)}


def _trunk(x3, st, LW, norm_final):
    B, T, _ = x3.shape
    g = _geom(B, T)
    x = x3.reshape(B * T, D_MODEL)
    s_gla, h_rg, conv_rg, c_ml, n_ml, m_ml, conv_ml, s_hg = st
    states = {"s_gla": s_gla, "h_rg": h_rg[:, :, None, :], "conv_rg": _pad8(conv_rg), "c_ml": c_ml, "n_ml": n_ml,
              "m_ml": _pad_cols(m_ml, LANES)[:, :, None, :], "conv_ml": _pad8(conv_ml), "s_hg": s_hg}
    outs = None
    gc = _geom_consts(g)
    for l in range(DEPTH):
        W = {**LW[l], **gc}
        if g.NSEG == 1:
            x, outs = _call_mixer(g, l, x, W, states, outs, _BRANCHES, True)
        else:
            new_outs, br = {}, {}
            for b in _BRANCHES:
                res, st_b = _call_mixer(g, l, x, W, states, outs, (b,), False)
                br.update(res)
                new_outs.update(st_b)
            x, _ = _call_mixer(g, l, x, W, states, None, (), True, br)
            outs = new_outs
        x = _call_mlp(x, W["gf"], W["w_up"], W["w_down"], _row(norm_final), final=(l == DEPTH - 1))
    tail3 = SUBLANES - (CONV_W - 1)
    new = (outs["s_gla"], outs["h_rg"][:, :, 0, :], outs["conv_rg"][:, :, tail3:, :], outs["c_ml"], outs["n_ml"],
           outs["m_ml"][:, :, 0, :ML_HEADS], outs["conv_ml"][:, :, tail3:, :], outs["s_hg"])
    return x.reshape(B, T, D_MODEL), new


def _zero_states(B):
    return (jnp.zeros((DEPTH, B, GLA_HEADS, GLA_DK, GLA_DV), F32),
            jnp.zeros((DEPTH, B, RG_WIDTH), F32),
            jnp.zeros((DEPTH, B, CONV_W - 1, RG_WIDTH), F32),
            jnp.zeros((DEPTH, B, ML_HEADS, ML_DH, ML_DH), F32),
            jnp.zeros((DEPTH, B, ML_HEADS, ML_DH), F32),
            jnp.zeros((DEPTH, B, ML_HEADS), F32),
            jnp.zeros((DEPTH, B, CONV_W - 1, 2 * ML_W), F32),
            jnp.zeros((DEPTH, B, HG_HEADS, HG_DK, HG_DV), F32))


def kernel(x_prompt, x_sample, state_gla, state_rglru_h, state_rglru_conv, state_mlstm_C, state_mlstm_n, state_mlstm_m, state_mlstm_conv, state_hgrn, norm_mix, w_in, gla_w_lr, gla_b_lr, gla_norm, rg_conv_w, rg_conv_b, rg_wa, rg_ba, rg_wx, rg_bx, rg_lam, ml_conv_w, ml_conv_b, ml_b_i, ml_b_f, ml_norm, hg_gamma, hg_norm, merge_b, w_br_gla, w_br_rg, w_br_ml, w_br_hg, w_out, norm_ffn, w_up, w_down, norm_final):
    P = dict(norm_mix=norm_mix, w_in=w_in, gla_w_lr=gla_w_lr, gla_b_lr=gla_b_lr, gla_norm=gla_norm,
             rg_conv_w=rg_conv_w, rg_conv_b=rg_conv_b, rg_wa=rg_wa, rg_ba=rg_ba, rg_wx=rg_wx, rg_bx=rg_bx,
             rg_lam=rg_lam, ml_conv_w=ml_conv_w, ml_conv_b=ml_conv_b, ml_b_i=ml_b_i, ml_b_f=ml_b_f,
             ml_norm=ml_norm, hg_norm=hg_norm, merge_b=merge_b, w_br_gla=w_br_gla, w_br_rg=w_br_rg,
             w_br_ml=w_br_ml, w_br_hg=w_br_hg, w_out=w_out, norm_ffn=norm_ffn, w_up=w_up, w_down=w_down)
    sm = jax.nn.softmax(hg_gamma.astype(F32), axis=0)
    lbs = jnp.concatenate([jnp.zeros_like(sm[:1]), jnp.cumsum(sm, axis=0)[:-1]], axis=0)
    lbs = jnp.clip(lbs, 0.0, 1.0 - 1e-6)
    LW = [_layer_weights(l, P, lbs) for l in range(DEPTH)]

    y_prompt, p_st = _trunk(x_prompt, _zero_states(x_prompt.shape[0]), LW, norm_final)
    s_in = (state_gla, state_rglru_h, state_rglru_conv, state_mlstm_C, state_mlstm_n, state_mlstm_m,
            state_mlstm_conv, state_hgrn)
    y_sample, s_st = _trunk(x_sample, s_in, LW, norm_final)
    return (y_prompt, y_sample) + tuple(p_st) + tuple(s_st)
```

```python
import collections
import functools

import jax
import jax.numpy as jnp
import numpy as np
from jax import lax
from jax.experimental import pallas as pl
from jax.experimental.pallas import tpu as pltpu

F32 = jnp.float32
BF16 = jnp.bfloat16

D_MODEL = 1024
DEPTH = 2
GLA_HEADS, GLA_DK, GLA_DV, GLA_RANK = 4, 64, 128, 16
GLA_KW, GLA_W = GLA_HEADS * GLA_DK, GLA_HEADS * GLA_DV
GLA_GATE_TAU = 16.0
RG_WIDTH, RG_BLOCKS, RG_C = 512, 8, 8.0
CONV_W = 4
ML_HEADS, ML_DH = 4, 128
ML_W = ML_HEADS * ML_DH
HG_HEADS, HG_DK, HG_DV = 4, 128, 128
HG_W = HG_HEADS * HG_DV
N_BRANCH = 4
D_FF = 4 * D_MODEL
EPS = 1e-6
NEG = -1e30

LANES = 128
SUBLANES = 8
MXU_DIM = 256
ROW_TILE = 256
MLP_TILE = 512
SUBTILES = 1
BAND = 4
LOG2E = 1.4426950408889634
SEG_UNROLL = 16
VMEM_LIMIT = 56 * 1024 * 1024

_OFF = {}
_o = 0
for _name, _size in (("g_q", GLA_KW), ("g_k", GLA_KW), ("g_v", GLA_W), ("g_lr", GLA_RANK), ("g_g", GLA_W),
                     ("r_x", RG_WIDTH), ("r_y", RG_WIDTH),
                     ("m_q", ML_W), ("m_k", ML_W), ("m_v", ML_W), ("m_o", ML_W), ("m_i", ML_HEADS), ("m_f", ML_HEADS),
                     ("h_q", HG_W), ("h_f", HG_W), ("h_i", HG_W), ("h_g", HG_W),
                     ("gates", N_BRANCH * D_MODEL)):
    _OFF[_name] = (_o, _o + _size)
    _o += _size

Geom = collections.namedtuple("Geom", "B T Bb Tt R SEG NSEG NT SUB")


def _geom(B, T):
    if T >= ROW_TILE:
        Bb, Tt = 1, ROW_TILE
        sub = SUBTILES if T % (ROW_TILE * SUBTILES) == 0 else 1
    else:
        Bb, Tt, sub = ROW_TILE // T, T, 1
    assert T % (Tt * sub) == 0 and B % Bb == 0 and Tt % SUBLANES == 0 and Tt >= SUBLANES
    return Geom(B, T, Bb, Tt, Bb * Tt, Tt, Bb, T // (Tt * sub), sub)


def _dot(a, b):
    return jnp.dot(a, b, preferred_element_type=F32)


def _dot_nt(a, b):
    return lax.dot_general(a, b, (((1,), (1,)), ((), ())), preferred_element_type=F32)


def _dot3(m01, x):
    hi = x.astype(BF16)
    r1 = x - hi.astype(F32)
    mid = r1.astype(BF16)
    lo = (r1 - mid.astype(F32)).astype(BF16)
    return _dot(m01, hi) + (_dot(m01, mid) + _dot(m01, lo))


def _iota(shape, dim):
    return lax.broadcasted_iota(jnp.int32, shape, dim)


def _div(x, n):
    assert n & (n - 1) == 0
    return x >> (n.bit_length() - 1)


def _log_sigmoid(x):
    return jnp.minimum(x, 0.0) - jnp.log(1.0 + jnp.exp(-jnp.abs(x)))


def _silu(x):
    h = 0.5 * x
    return h + h * jnp.tanh(h)


def _gelu_tanh(x):
    return 0.5 * x * (1.0 + jnp.tanh(0.7978845608028654 * (x + 0.044715 * (x * x * x))))


def _rmsnorm(x, g):
    return x * lax.rsqrt(jnp.mean(x * x, axis=-1, keepdims=True) + EPS) * g


def _head_rmsnorm(o, gn, H, d):
    outs = []
    for h in range(H):
        oh = o[:, h * d:(h + 1) * d]
        outs.append(oh * lax.rsqrt(jnp.mean(oh * oh, axis=-1, keepdims=True) + EPS) * gn[:, h * d:(h + 1) * d])
    return jnp.concatenate(outs, axis=1)


def _block_last(c, w):
    R, C = c.shape
    if w == R:
        return jnp.broadcast_to(c[R - 1:R, :], (R, C))
    c3 = c.reshape(R // w, w, C)
    return jnp.broadcast_to(c3[:, w - 1:w, :], (R // w, w, C)).reshape(R, C)


def _seg_rows(ref3, g):
    v = ref3[...]
    C = v.shape[-1]
    return jnp.broadcast_to(v, (g.NSEG, g.SEG, C)).reshape(g.R, C)


def _causal_conv(x, prev8, w_ref, b_ref, g):
    R, C = x.shape
    nb = R // SUBLANES
    sub = _iota((1, SUBLANES, 1), 1)
    x3 = x.reshape(nb, SUBLANES, C)
    p3 = prev8.reshape(nb, SUBLANES, C)
    y = b_ref[...] + x3 * w_ref[CONV_W - 1:CONV_W, :]
    for s in range(1, CONV_W):
        xs = pltpu.roll(jnp.where(sub >= SUBLANES - s, p3, x3), s, 1)
        y = y + xs * w_ref[CONV_W - 1 - s:CONV_W - s, :]
    return y.reshape(R, C)


def _conv_io(x, conv_in_ref, conv_out_ref, carry_ref, g):
    R = g.R
    if g.NSEG == 1:
        prev8 = jnp.concatenate([carry_ref[...], x[:R - SUBLANES, :]], axis=0)
        last8 = x[R - SUBLANES:, :]
        carry_ref[...] = last8
        conv_out_ref[0] = last8
    else:
        C = x.shape[-1]
        prev8 = conv_in_ref[...].reshape(R, C)
        conv_out_ref[...] = x.reshape(g.NSEG, g.SEG, C)
    return prev8


def _gl_core(q, k, v, la, s_ref, scr, g, H, dk, dv, tri, lvl, ones_bd):
    qd_s, kdT_s, v_s, o_s, decT_s = scr
    R, SEG, NSEG = g.R, g.SEG, g.NSEG
    HK = H * dk
    c = _dot3(tri, la)
    cl = _block_last(c, SEG)

    bw = BAND
    HV = H * dv
    c2 = c * LOG2E
    nb = R // SUBLANES
    sub = _iota((1, SUBLANES, 1), 1)
    sub_b = sub & (bw - 1)
    q3, k3, c3, v3 = (a.reshape(nb, SUBLANES, a.shape[1]) for a in (q, k, c2, v))
    o = None
    for d in range(bw):
        if d == 0:
            p3, vd = q3 * k3, v3
        else:
            diff = jnp.where(sub_b >= d, c3 - pltpu.roll(c3, d, 1), NEG)
            p3 = q3 * pltpu.roll(k3, d, 1) * jnp.exp2(diff)
            vd = pltpu.roll(v3, d, 1)
        pb = p3.reshape(R, HK).astype(BF16)
        parts = [_dot(pb[:, i * MXU_DIM:(i + 1) * MXU_DIM], ones_bd) for i in range(HK // MXU_DIM)]
        term = (parts[0] if len(parts) == 1 else jnp.concatenate(parts, axis=1)) * vd.reshape(R, HV)
        o = term if o is None else o + term

    vb = v.astype(BF16)
    w = bw
    a_off = [None] * H
    level = 0
    while 2 * w <= SEG:
        if w < SUBLANES:
            assert 2 * w == SUBLANES
            first = sub < w
            pmid = c3[:, w - 1:w, :]
            x = jnp.where(first, k3, q3) * jnp.exp2(jnp.where(first, pmid - c3, c3 - pmid))
            x = x.reshape(R, HK).astype(BF16)
        else:
            nb2 = R // (2 * w)
            c4 = c2.reshape(nb2, 2, w, HK)
            pmid = c4[:, 0:1, w - 1:w, :]
            xk = k.reshape(nb2, 2, w, HK)[:, 0:1] * jnp.exp2(pmid - c4[:, 0:1])
            xq = q.reshape(nb2, 2, w, HK)[:, 1:2] * jnp.exp2(c4[:, 1:2] - pmid)
            x = jnp.concatenate([xk, xq], axis=1).reshape(R, HK).astype(BF16)
        level += 1
        m = lvl == level
        for h in range(H):
            xh = x[:, h * dk:(h + 1) * dk]
            a = _dot_nt(xh, xh)
            a_off[h] = jnp.where(m, a, 0.0 if a_off[h] is None else a_off[h])
        w *= 2
    if a_off[0] is not None:
        o = o + jnp.concatenate([_dot(a_off[h].astype(BF16), vb[:, h * dv:(h + 1) * dv]) for h in range(H)], axis=1)

    qd_s[...] = q * jnp.exp(c)
    kdT_s[...] = (k * jnp.exp(cl - c)).T.astype(BF16)
    v_s[...] = vb
    decT_s[...] = jnp.exp(cl).T
    o_s[...] = o
    lane = _iota((1, R), 1)

    def seg_body(sg, carry):
        r0 = sg * SEG if isinstance(sg, int) else pl.multiple_of(sg * SEG, SEG)
        for h in range(H):
            S = s_ref[sg, h]
            qd = qd_s[pl.ds(r0, SEG), h * dk:(h + 1) * dk].astype(BF16)
            o_s[pl.ds(r0, SEG), h * dv:(h + 1) * dv] += _dot(qd, S.astype(BF16))
            kt = kdT_s[h * dk:(h + 1) * dk, :]
            if NSEG > 1:
                kt = jnp.where(_div(lane, SEG) == sg, kt, jnp.zeros_like(kt))
            U = _dot(kt, v_s[:, h * dv:(h + 1) * dv])
            dec = jnp.sum(jnp.where(lane == r0, decT_s[h * dk:(h + 1) * dk, :], 0.0), axis=1, keepdims=True)
            s_ref[sg, h] = dec * S + U
        return carry

    if NSEG == 1:
        seg_body(0, 0)
    else:
        lax.fori_loop(0, NSEG, seg_body, 0, unroll=SEG_UNROLL)
    return o_s[...]


def _gl_scratch(g, H, dk, dv):
    return [pltpu.VMEM((g.R, H * dk), F32), pltpu.VMEM((H * dk, g.R), BF16), pltpu.VMEM((g.R, H * dv), BF16),
            pltpu.VMEM((g.R, H * dv), F32), pltpu.VMEM((H * dk, g.R), F32)]


_PIECES = {"gla": (2 * GLA_KW, GLA_W, GLA_W, LANES), "rg": (RG_WIDTH, RG_WIDTH),
           "ml": (ML_W, ML_W, ML_W, ML_W, 2 * LANES), "hg": (HG_W, HG_W, HG_W, HG_W)}


def _gla_branch(g, z, r):
    qk, v, gate, lr = z
    q = qk[:, 0:GLA_KW] * (GLA_DK ** -0.5)
    k = qk[:, GLA_KW:]
    la = _log_sigmoid(_dot(lr.astype(BF16), r.wlr[...]) + r.blr[...]) * (1.0 / GLA_GATE_TAU)
    o = _gl_core(q, k, v, la, r.s_gla_o, r.gla_scr, g, GLA_HEADS, GLA_DK, GLA_DV, r.tri[...], r.lvl[...],
                 r.ones_gla[...])
    return _head_rmsnorm(o, r.gla_norm[...], GLA_HEADS, GLA_DV) * _silu(gate)


def _hgrn_branch(g, z, r):
    hq, fp, v, gate = z
    lb = r.hg_lb[...]
    ls = _log_sigmoid(fp)
    pos = lb > 0.0
    a_ = jnp.log(jnp.where(pos, lb, 1.0))
    b_ = jnp.log1p(-lb) + ls
    lae = jnp.maximum(a_, b_) + jnp.log(1.0 + jnp.exp(-jnp.abs(a_ - b_)))
    la = jnp.where(pos, lae, ls)
    k = (1.0 - lb) * jax.nn.sigmoid(-fp)
    q = _silu(hq)
    o = _gl_core(q, k, v, la, r.s_hg_o, r.hg_scr, g, HG_HEADS, HG_DK, HG_DV, r.tri[...], r.lvl[...],
                 r.ones_hg[...])
    return _head_rmsnorm(o, r.hg_norm[...], HG_HEADS, HG_DV) * _silu(gate)


def _rglru_branch(g, z, r):
    R, SEG = g.R, g.SEG
    rx, ry = z
    prev8 = _conv_io(rx, r.conv_rg, r.conv_rg_o, r.rg_carry, g)
    xc = _causal_conv(rx, prev8, r.rg_cw, r.rg_cb, g)
    xcb = xc.astype(BF16)
    rg = jax.nn.sigmoid(_dot(xcb, r.rg_wa[...]) + r.rg_ba[...])
    ig = jax.nn.sigmoid(_dot(xcb, r.rg_wx[...]) + r.rg_bx[...])
    log_a = RG_C * rg * _log_sigmoid(r.rg_lam[...])
    a = jnp.exp(log_a)
    u = jnp.sqrt(jnp.maximum(1.0 - a * a, 0.0)) * (ig * xc)
    nbs = SEG // SUBLANES
    sub = _iota((1, SUBLANES, 1), 1)
    a3 = a.reshape(R // SUBLANES, SUBLANES, RG_WIDTH)
    u3 = u.reshape(R // SUBLANES, SUBLANES, RG_WIDTH)
    d = 1
    while d < SUBLANES:
        ok = sub >= d
        a_s = jnp.where(ok, pltpu.roll(a3, d, 1), 1.0)
        u_s = jnp.where(ok, pltpu.roll(u3, d, 1), 0.0)
        u3 = a3 * u_s + u3
        a3 = a3 * a_s
        d *= 2
    a4 = a3.reshape(g.NSEG, nbs, SUBLANES, RG_WIDTH)
    u4 = u3.reshape(g.NSEG, nbs, SUBLANES, RG_WIDTH)
    hc = r.h_rg_o[...]
    blocks = []
    for j in range(nbs):
        hb = u4[:, j] + a4[:, j] * hc
        blocks.append(hb)
        hc = hb[:, SUBLANES - 1:SUBLANES, :]
    r.h_rg_o[...] = hc
    hcur = (blocks[0] if nbs == 1 else jnp.stack(blocks, axis=1)).reshape(R, RG_WIDTH)
    return hcur * _gelu_tanh(ry)


def _mlstm_branch(g, z, r):
    R, SEG, NSEG = g.R, g.SEG, g.NSEG
    H, dh = ML_HEADS, ML_DH
    q_s, wkT_s, wk_s, v_s, sc_s, qc_s, qn_s, hm_s = r.ml_scr
    cout_ref, nout_ref, mout_ref = r.c_ml_o, r.n_ml_o, r.m_ml_o
    zq, zk, mv, mo, zif = z
    qk_pre = jnp.concatenate([zq, zk], axis=1)
    ipre = zif[:, :LANES] + r.ml_bi[...]
    fpre = zif[:, LANES:] + r.ml_bf[...]
    prev8 = _conv_io(qk_pre, r.conv_ml, r.conv_ml_o, r.ml_carry, g)
    qk = _silu(_causal_conv(qk_pre, prev8, r.ml_cw, r.ml_cb, g))
    mq = qk[:, :ML_W]
    mk = qk[:, ML_W:] * (dh ** -0.5)

    keep = r.lvl[...] >= 0
    b = _dot3(r.tri[...], _log_sigmoid(fpre))
    m_rows = _seg_rows(mout_ref, g)
    prev = b + m_rows
    gT = (ipre - b).T
    bl = _block_last(b, SEG)
    wlog = bl - b + ipre
    wmax = jnp.broadcast_to(jnp.max(wlog.reshape(NSEG, SEG, LANES), axis=1, keepdims=True),
                            (NSEG, SEG, LANES)).reshape(R, LANES)
    m_new = jnp.maximum(bl + m_rows, wmax)
    wgt = jnp.exp(wlog - m_new)
    sc_s[...] = jnp.exp(bl + m_rows - m_new)
    mout_ref[...] = m_new.reshape(NSEG, SEG, LANES)[:, 0:1, :]

    q_s[...] = mq
    v_s[...] = mv.astype(BF16)
    wk = jnp.concatenate([wgt[:, h:h + 1] * mk[:, h * dh:(h + 1) * dh] for h in range(H)], axis=1)
    wk_s[...] = wk
    wkT_s[...] = wk.T.astype(BF16)
    lane = _iota((1, R), 1)

    for h in range(H):
        hs = slice(h * dh, (h + 1) * dh)
        ls = slice(h * LANES, (h + 1) * LANES)
        logd = jnp.where(keep, b[:, h:h + 1] + gT[h:h + 1, :], NEG)
        mt = jnp.maximum(prev[:, h:h + 1], jnp.max(logd, axis=1, keepdims=True))
        dm = jnp.where(keep, jnp.exp(logd - mt), 0.0)
        sp = jnp.exp(prev[:, h:h + 1] - mt)
        s_mat = _dot_nt(mq[:, hs].astype(BF16), mk[:, hs].astype(BF16)) * dm
        num = _dot(s_mat.astype(BF16), v_s[:, hs])
        den = jnp.sum(s_mat, axis=1, keepdims=True)

        def seg_body(sg, carry, h=h, hs=hs, ls=ls):
            r0 = sg * SEG if isinstance(sg, int) else pl.multiple_of(sg * SEG, SEG)
            C = cout_ref[sg, h]
            n = nout_ref[sg, pl.ds(h, 1), :]
            qb = q_s[pl.ds(r0, SEG), hs]
            qc_s[pl.ds(r0, SEG), hs] = _dot(qb.astype(BF16), C.astype(BF16))
            qn_s[pl.ds(r0, SEG), ls] = jnp.broadcast_to(jnp.sum(qb * n, axis=1, keepdims=True), (SEG, LANES))
            kt = wkT_s[hs, :]
            if NSEG > 1:
                kt = jnp.where(_div(lane, SEG) == sg, kt, jnp.zeros_like(kt))
            U = _dot(kt, v_s[:, hs])
            sc = sc_s[pl.ds(r0, 1), h:h + 1]
            cout_ref[sg, h] = sc * C + U
            nout_ref[sg, pl.ds(h, 1), :] = sc * n + jnp.sum(wk_s[pl.ds(r0, SEG), hs], axis=0, keepdims=True)
            return carry

        if NSEG == 1:
            seg_body(0, 0)
        else:
            lax.fori_loop(0, NSEG, seg_body, 0, unroll=SEG_UNROLL)
        num = num + sp * qc_s[:, hs]
        den = den + sp * qn_s[:, h * LANES:h * LANES + 1]
        hm_s[:, hs] = num / jnp.maximum(jnp.abs(den), jnp.exp(-mt))

    hm = jax.nn.sigmoid(mo) * hm_s[...]
    return _head_rmsnorm(hm, r.ml_norm[...], H, dh)


_BRANCHES = ("gla", "rg", "ml", "hg")
_BRANCH_FN = {"gla": _gla_branch, "rg": _rglru_branch, "ml": _mlstm_branch, "hg": _hgrn_branch}
_BRANCH_CONSTS = {
    "gla": ("w_gla", "wlr", "blr", "gla_norm", "ones_gla"),
    "rg": ("w_rg", "rg_cw", "rg_cb", "rg_wa", "rg_ba", "rg_wx", "rg_bx", "rg_lam"),
    "ml": ("w_ml", "ml_cw", "ml_cb", "ml_bi", "ml_bf", "ml_norm"),
    "hg": ("w_hg", "hg_lb", "hg_norm", "ones_hg"),
}
_MERGE_CONSTS = ("w_gates", "merge_b", "w_br0", "w_br1", "w_br2", "w_br3", "w_out")
_BRANCH_STATES = {"gla": ("s_gla",), "rg": ("h_rg", "conv_rg"), "ml": ("c_ml", "n_ml", "m_ml", "conv_ml"),
                  "hg": ("s_hg",)}
_CARRIED = ("s_gla", "h_rg", "c_ml", "n_ml", "m_ml", "s_hg")
_BRANCH_WIDTH = {"gla": GLA_W, "rg": RG_WIDTH, "ml": ML_W, "hg": HG_W}


def _state_tails():
    return {"s_gla": (GLA_HEADS, GLA_DK, GLA_DV), "h_rg": (1, RG_WIDTH), "conv_rg": (SUBLANES, RG_WIDTH),
            "c_ml": (ML_HEADS, ML_DH, ML_DH), "n_ml": (ML_HEADS, ML_DH), "m_ml": (1, LANES),
            "conv_ml": (SUBLANES, 2 * ML_W), "s_hg": (HG_HEADS, HG_DK, HG_DV)}


def _branch_scratch(g, b):
    R = g.R
    if b == "gla":
        return {"gla_scr": _gl_scratch(g, GLA_HEADS, GLA_DK, GLA_DV)}
    if b == "hg":
        return {"hg_scr": _gl_scratch(g, HG_HEADS, HG_DK, HG_DV)}
    if b == "rg":
        return {"rg_carry": [pltpu.VMEM((SUBLANES, RG_WIDTH), F32)]}
    return {"ml_carry": [pltpu.VMEM((SUBLANES, 2 * ML_W), F32)],
            "ml_scr": [pltpu.VMEM((R, ML_W), F32), pltpu.VMEM((ML_W, R), BF16), pltpu.VMEM((R, ML_W), F32),
                       pltpu.VMEM((R, ML_W), BF16), pltpu.VMEM((R, LANES), F32), pltpu.VMEM((R, ML_W), F32),
                       pltpu.VMEM((R, ML_HEADS * LANES), F32), pltpu.VMEM((R, ML_W), F32)]}


def _mixer_plan(g, branches, merge):
    consts = (("gm", "tri", "lvl") if branches else ("gm",)) + sum((_BRANCH_CONSTS[b] for b in branches), ())
    consts += _MERGE_CONSTS if merge else ()
    states = sum((_BRANCH_STATES[b] for b in branches), ())
    br_in = tuple("br_" + b for b in _BRANCHES if b not in branches) if merge else ()
    outs = (("y",) if merge else tuple("br_" + b + "_o" for b in branches)) + tuple(s + "_o" for s in states)
    scratch = {}
    for b in branches:
        scratch.update(_branch_scratch(g, b))
    return consts, states, br_in, outs, scratch


def _mixer_kernel(g, branches, merge, n_alias, *refs):
    consts, states, br_in, outs, scratch = _mixer_plan(g, branches, merge)
    names = ("x",) + consts + states + br_in
    r = dict(zip(names, refs[:len(names)]))
    pos = len(names) + n_alias
    r.update(zip(outs, refs[pos:pos + len(outs)]))
    pos += len(outs)
    for nm, shapes in scratch.items():
        grp = refs[pos:pos + len(shapes)]
        r[nm] = grp[0] if nm.endswith("carry") else grp
        pos += len(shapes)
    r = collections.namedtuple("Refs", r.keys())(**r)

    @pl.when(pl.program_id(1) == 0)
    def _():
        for nm in states:
            if nm in _CARRIED:
                getattr(r, nm + "_o")[...] = getattr(r, nm)[...]
        if g.NSEG == 1:
            for b, nm in (("rg", "conv_rg"), ("ml", "conv_ml")):
                if b in branches:
                    getattr(r, b + "_carry")[...] = getattr(r, nm)[0]

    for sub in range(g.SUB):
        _mixer_tile(g, branches, merge, r, slice(sub * g.R, (sub + 1) * g.R))


def _mixer_tile(g, branches, merge, r, rows):
    x = r.x[rows, :]
    xn = _rmsnorm(x, r.gm[...]).astype(BF16)

    def inproj_jobs(b):
        w_ref = getattr(r, _BRANCH_CONSTS[b][0])
        z[b] = [None] * len(_PIECES[b])
        jobs, lo = [], 0
        for i, wd in enumerate(_PIECES[b]):
            def job(i=i, lo=lo, hi=lo + wd):
                z[b][i] = _dot(xn, w_ref[:, lo:hi])
            jobs.append(job)
            lo += wd
        return jobs

    def gate_jobs(j):
        half = D_MODEL // 2
        gates[j] = [None, None]
        jobs = []
        for i in range(2):
            def job(i=i, lo=j * D_MODEL + i * half):
                pre = _dot(xn, r.w_gates[:, lo:lo + half]) + r.merge_b[j:j + 1, i * half:(i + 1) * half]
                gates[j][i] = jax.nn.sigmoid(pre).astype(BF16)
            jobs.append(job)
        return jobs

    def proj_job(j, b):
        def job():
            bv = br[b] if b in br else getattr(r, "br_" + b)[rows, :]
            gt = jnp.concatenate(gates[j], axis=1).astype(F32)
            merged.append(gt * _dot(bv, getattr(r, "w_br%d" % j)[...]))
        return job

    z, gates, br, merged = {}, {}, {}, []
    for b in sorted(branches, key=("ml", "rg", "hg", "gla").index):
        for jb in inproj_jobs(b):
            jb()
        br[b] = _BRANCH_FN[b](g, z[b], r).astype(BF16)
    if not merge:
        for b in branches:
            getattr(r, "br_" + b + "_o")[rows, :] = br[b]
        return
    for j, b in enumerate(_BRANCHES):
        for jb in gate_jobs(j):
            jb()
        proj_job(j, b)()
    total = merged[0]
    for term in merged[1:]:
        total = total + term
    r.y[rows, :] = x + _dot(total.astype(BF16), r.w_out[...])


def _mlp_kernel(final, x_ref, gm_ref, wu_ref, wd_ref, gf_ref, y_ref):
    x = x_ref[...]
    xn = _rmsnorm(x, gm_ref[...]).astype(BF16)
    hdn = jnp.square(jnp.maximum(_dot(xn, wu_ref[...]), 0.0))
    y = x + _dot(hdn.astype(BF16), wd_ref[...])
    if final:
        y = _rmsnorm(y, gf_ref[...])
    y_ref[...] = y


def _const(shape):
    nd = len(shape)
    return pl.BlockSpec(shape, lambda bi, ti, _nd=nd: (0,) * _nd)


def _rows(g, C):
    return pl.BlockSpec((g.R * g.SUB, C), lambda bi, ti, _nt=g.NT: (bi * _nt + ti, 0))


def _layer_state(g, l, tail):
    nd = len(tail)
    return pl.BlockSpec((None, g.Bb) + tuple(tail), lambda bi, ti, _l=l, _nd=nd: (_l, bi) + (0,) * _nd)


def _call_mixer(g, l, x, W, states, prev_outs, branches, merge, br_in=None):
    n = g.B * g.T
    tails = _state_tails()
    consts, snames, br_names, outs, scratch = _mixer_plan(g, branches, merge)
    ins = [x] + [W[c] for c in consts] + [states[s] for s in snames] + [br_in[b[3:]] for b in br_names]
    in_specs = ([_rows(g, D_MODEL)] + [_const(W[c].shape) for c in consts]
                + [_layer_state(g, l, tails[s]) for s in snames] + [_rows(g, _BRANCH_WIDTH[b[3:]]) for b in br_names])
    n_main = 1 if merge else len(branches)
    aliases = {}
    if prev_outs is not None:
        for j, s in enumerate(snames):
            aliases[len(ins)] = n_main + j
            ins.append(prev_outs[s])
            in_specs.append(pl.BlockSpec(memory_space=pl.ANY))
    if merge:
        main_specs = [_rows(g, D_MODEL)]
        main_shapes = [jax.ShapeDtypeStruct((n, D_MODEL), F32)]
    else:
        main_specs = [_rows(g, _BRANCH_WIDTH[b]) for b in branches]
        main_shapes = [jax.ShapeDtypeStruct((n, _BRANCH_WIDTH[b]), BF16) for b in branches]
    out = pl.pallas_call(
        functools.partial(_mixer_kernel, g, branches, merge, len(aliases)),
        grid=(g.B // g.Bb, g.NT),
        in_specs=in_specs,
        out_specs=main_specs + [_layer_state(g, l, tails[s]) for s in snames],
        out_shape=main_shapes + [jax.ShapeDtypeStruct((DEPTH, g.B) + tails[s], F32) for s in snames],
        scratch_shapes=sum(scratch.values(), []),
        input_output_aliases=aliases,
        compiler_params=pltpu.CompilerParams(dimension_semantics=("arbitrary", "arbitrary"),
                                             vmem_limit_bytes=VMEM_LIMIT),
        name="mixer_" + "_".join(branches + (("merge",) if merge else ())),
    )(*ins)
    main = out[0] if merge else dict(zip(branches, out[:n_main]))
    return main, dict(zip(snames, out[n_main:]))


def _call_mlp(x, gm, wu, wd, gf, final):
    n = x.shape[0]
    consts = [gm, wu, wd, gf]
    tok = pl.BlockSpec((MLP_TILE, D_MODEL), lambda i: (i, 0))
    return pl.pallas_call(
        functools.partial(_mlp_kernel, final),
        grid=(n // MLP_TILE,),
        in_specs=[tok] + [pl.BlockSpec(c.shape, lambda i: (0, 0)) for c in consts],
        out_specs=tok,
        out_shape=jax.ShapeDtypeStruct((n, D_MODEL), F32),
        compiler_params=pltpu.CompilerParams(dimension_semantics=("arbitrary",), vmem_limit_bytes=VMEM_LIMIT),
        name="mlp",
    )(x, *consts)


def _cols(w, *names):
    return [w[:, _OFF[n][0]:_OFF[n][1]] for n in names]


def _pad_cols(w, width):
    return jnp.pad(w, [(0, 0)] * (w.ndim - 1) + [(0, width - w.shape[-1])])


def _block_diag(w):
    nb, d, e = w.shape
    return (jnp.eye(nb, dtype=w.dtype)[:, None, :, None] * w[:, :, None, :]).reshape(nb * d, nb * e)


def _row(v):
    return v.reshape(1, -1).astype(F32)


def _pad8(conv_state):
    return jnp.pad(conv_state, ((0, 0), (0, 0), (SUBLANES - (CONV_W - 1), 0), (0, 0)))


def _layer_weights(l, P, lbs):
    w = P["w_in"][l]
    W = {}
    W["gm"] = _row(P["norm_mix"][l])
    W["w_gla"] = jnp.concatenate(_cols(w, "g_q", "g_k", "g_v", "g_g") + [_pad_cols(_cols(w, "g_lr")[0], LANES)],
                                 axis=1).astype(BF16)
    W["wlr"] = jnp.pad(P["gla_w_lr"][l], ((0, LANES - GLA_RANK), (0, 0))).astype(BF16)
    W["blr"] = _row(P["gla_b_lr"][l])
    W["gla_norm"] = _row(P["gla_norm"][l])
    W["w_rg"] = jnp.concatenate(_cols(w, "r_x", "r_y"), axis=1).astype(BF16)
    W["rg_cw"] = P["rg_conv_w"][l].astype(F32)
    W["rg_cb"] = _row(P["rg_conv_b"][l])
    W["rg_wa"] = _block_diag(P["rg_wa"][l]).astype(BF16)
    W["rg_ba"] = _row(P["rg_ba"][l])
    W["rg_wx"] = _block_diag(P["rg_wx"][l]).astype(BF16)
    W["rg_bx"] = _row(P["rg_bx"][l])
    W["rg_lam"] = _row(P["rg_lam"][l])
    W["w_ml"] = jnp.concatenate(_cols(w, "m_q", "m_k", "m_v", "m_o")
                                + [_pad_cols(_cols(w, "m_i")[0], LANES), _pad_cols(_cols(w, "m_f")[0], LANES)],
                                axis=1).astype(BF16)
    W["ml_cw"] = P["ml_conv_w"][l].astype(F32)
    W["ml_cb"] = _row(P["ml_conv_b"][l])
    W["ml_bi"] = _pad_cols(_row(P["ml_b_i"][l]), LANES)
    W["ml_bf"] = _pad_cols(_row(P["ml_b_f"][l]), LANES)
    W["ml_norm"] = _row(P["ml_norm"][l])
    W["w_hg"] = jnp.concatenate(_cols(w, "h_q", "h_f", "h_i", "h_g"), axis=1).astype(BF16)
    W["hg_lb"] = _row(lbs[l])
    W["hg_norm"] = _row(P["hg_norm"][l])
    W["w_gates"] = _cols(w, "gates")[0].astype(BF16)
    W["merge_b"] = P["merge_b"][l].astype(F32)
    for j, nm in enumerate(("w_br_gla", "w_br_rg", "w_br_ml", "w_br_hg")):
        W["w_br%d" % j] = P[nm][l].astype(BF16)
    W["w_out"] = P["w_out"][l].astype(BF16)
    W["gf"] = _row(P["norm_ffn"][l])
    W["w_up"] = P["w_up"][l].astype(BF16)
    W["w_down"] = P["w_down"][l].astype(BF16)
    return W


def _ones_block_diag(dk, dv):
    hpg = MXU_DIM // dk
    return (np.arange(MXU_DIM)[:, None] // dk == np.arange(hpg * dv)[None, :] // dv).astype(np.float32)


def _geom_consts(g):
    t = np.arange(g.R)[:, None]
    s = np.arange(g.R)[None, :]
    keep = (s <= t) & (t // g.SEG == s // g.SEG)
    lvl = np.full((g.R, g.R), -1, np.int32)
    lvl[keep & (t // BAND == s // BAND)] = 0
    w, i = BAND, 1
    while 2 * w <= g.SEG:
        lvl[keep & (t // (2 * w) == s // (2 * w)) & (t // w != s // w)] = i
        w, i = 2 * w, i + 1
    return {"tri": jnp.asarray(keep, BF16), "lvl": jnp.asarray(lvl),
            "ones_gla": jnp.asarray(_ones_block_diag(GLA_DK, GLA_DV), BF16),
            "ones_hg": jnp.asarray(_ones_block_diag(HG_DK, HG_DV), BF16)}


def _trunk(x3, st, LW, norm_final):
    B, T, _ = x3.shape
    g = _geom(B, T)
    x = x3.reshape(B * T, D_MODEL)
    s_gla, h_rg, conv_rg, c_ml, n_ml, m_ml, conv_ml, s_hg = st
    states = {"s_gla": s_gla, "h_rg": h_rg[:, :, None, :], "conv_rg": _pad8(conv_rg), "c_ml": c_ml, "n_ml": n_ml,
              "m_ml": _pad_cols(m_ml, LANES)[:, :, None, :], "conv_ml": _pad8(conv_ml), "s_hg": s_hg}
    outs = None
    gc = _geom_consts(g)
    for l in range(DEPTH):
        W = {**LW[l], **gc}
        if g.NSEG == 1:
            x, outs = _call_mixer(g, l, x, W, states, outs, _BRANCHES, True)
        else:
            new_outs, br = {}, {}
            for b in _BRANCHES:
                res, st_b = _call_mixer(g, l, x, W, states, outs, (b,), False)
                br.update(res)
                new_outs.update(st_b)
            x, _ = _call_mixer(g, l, x, W, states, None, (), True, br)
            outs = new_outs
        x = _call_mlp(x, W["gf"], W["w_up"], W["w_down"], _row(norm_final), final=(l == DEPTH - 1))
    tail3 = SUBLANES - (CONV_W - 1)
    new = (outs["s_gla"], outs["h_rg"][:, :, 0, :], outs["conv_rg"][:, :, tail3:, :], outs["c_ml"], outs["n_ml"],
           outs["m_ml"][:, :, 0, :ML_HEADS], outs["conv_ml"][:, :, tail3:, :], outs["s_hg"])
    return x.reshape(B, T, D_MODEL), new


def _zero_states(B):
    return (jnp.zeros((DEPTH, B, GLA_HEADS, GLA_DK, GLA_DV), F32),
            jnp.zeros((DEPTH, B, RG_WIDTH), F32),
            jnp.zeros((DEPTH, B, CONV_W - 1, RG_WIDTH), F32),
            jnp.zeros((DEPTH, B, ML_HEADS, ML_DH, ML_DH), F32),
            jnp.zeros((DEPTH, B, ML_HEADS, ML_DH), F32),
            jnp.zeros((DEPTH, B, ML_HEADS), F32),
            jnp.zeros((DEPTH, B, CONV_W - 1, 2 * ML_W), F32),
            jnp.zeros((DEPTH, B, HG_HEADS, HG_DK, HG_DV), F32))


def kernel(x_prompt, x_sample, state_gla, state_rglru_h, state_rglru_conv, state_mlstm_C, state_mlstm_n, state_mlstm_m, state_mlstm_conv, state_hgrn, norm_mix, w_in, gla_w_lr, gla_b_lr, gla_norm, rg_conv_w, rg_conv_b, rg_wa, rg_ba, rg_wx, rg_bx, rg_lam, ml_conv_w, ml_conv_b, ml_b_i, ml_b_f, ml_norm, hg_gamma, hg_norm, merge_b, w_br_gla, w_br_rg, w_br_ml, w_br_hg, w_out, norm_ffn, w_up, w_down, norm_final):
    P = dict(norm_mix=norm_mix, w_in=w_in, gla_w_lr=gla_w_lr, gla_b_lr=gla_b_lr, gla_norm=gla_norm,
             rg_conv_w=rg_conv_w, rg_conv_b=rg_conv_b, rg_wa=rg_wa, rg_ba=rg_ba, rg_wx=rg_wx, rg_bx=rg_bx,
             rg_lam=rg_lam, ml_conv_w=ml_conv_w, ml_conv_b=ml_conv_b, ml_b_i=ml_b_i, ml_b_f=ml_b_f,
             ml_norm=ml_norm, hg_norm=hg_norm, merge_b=merge_b, w_br_gla=w_br_gla, w_br_rg=w_br_rg,
             w_br_ml=w_br_ml, w_br_hg=w_br_hg, w_out=w_out, norm_ffn=norm_ffn, w_up=w_up, w_down=w_down)
    sm = jax.nn.softmax(hg_gamma.astype(F32), axis=0)
    lbs = jnp.concatenate([jnp.zeros_like(sm[:1]), jnp.cumsum(sm, axis=0)[:-1]], axis=0)
    lbs = jnp.clip(lbs, 0.0, 1.0 - 1e-6)
    LW = [_layer_weights(l, P, lbs) for l in range(DEPTH)]

    y_prompt, p_st = _trunk(x_prompt, _zero_states(x_prompt.shape[0]), LW, norm_final)
    s_in = (state_gla, state_rglru_h, state_rglru_conv, state_mlstm_C, state_mlstm_n, state_mlstm_m,
            state_mlstm_conv, state_hgrn)
    y_sample, s_st = _trunk(x_sample, s_in, LW, norm_final)
    return (y_prompt, y_sample) + tuple(p_st) + tuple(s_st)
```
